```python
import math
import jax, jax.numpy as jnp
from jax import lax
import numpy as np

D_MODEL = 1024
BATCH = 8
SEQ = 2048
DEPTH = 4

CHUNK = 64
N_MIXERS = 2
N_A = (DEPTH + 1) // 2
N_B = DEPTH // 2
NORM_EPS = 1e-6

MLA_HEADS = 8
QK_NOPE = 128
QK_ROPE = 64
QK_HEAD = QK_NOPE + QK_ROPE
V_HEAD = 128
Q_LORA = 384
KV_LORA = 256
ROPE_BASE = 10000.0
Q_BLOCK = 128

LRU_WIDTH = D_MODEL
LRU_HEADS = 8
LRU_BLOCK = LRU_WIDTH // LRU_HEADS
CONV_WIDTH = 4
LRU_C = 8.0

DENSE_FF = 2816
N_EXPERTS = 8
TOP_K = 2
EXPERT_FF = 3584

MAX_POS_OFFSET = 8192

kernel_name = "hybrid_mla_rglru_moe_stream_encoder"


def rms_norm(x, g):
    x32 = x.astype(jnp.float32)
    y = x32 * lax.rsqrt(jnp.mean(x32 * x32, axis=-1, keepdims=True) + NORM_EPS)
    return (y * g.astype(jnp.float32)).astype(x.dtype)


def rope_tables(positions):
    inv = 1.0 / (ROPE_BASE ** (jnp.arange(0, QK_ROPE, 2, dtype=jnp.float32) / QK_ROPE))
    ang = positions.astype(jnp.float32)[..., None] * inv
    return jnp.cos(ang), jnp.sin(ang)


def apply_rope(x, cos, sin):
    x32 = x.astype(jnp.float32)
    half = QK_ROPE // 2
    x1, x2 = x32[..., :half], x32[..., half:]
    c = cos[:, :, None, :]
    s = sin[:, :, None, :]
    return jnp.concatenate([x1 * c - x2 * s, x1 * s + x2 * c], axis=-1).astype(x.dtype)


def chunk_causal_attention(q, k, v):
    b, s, h, dq = q.shape
    nb = s // Q_BLOCK
    scale = 1.0 / math.sqrt(dq)
    q_blocks = q.reshape(b, nb, Q_BLOCK, h, dq).transpose(1, 0, 2, 3, 4)
    key_chunk = jnp.arange(s) // CHUNK

    def one_block(args):
        qb, blk = args
        q_chunk = (blk * Q_BLOCK + jnp.arange(Q_BLOCK)) // CHUNK
        mask = key_chunk[None, :] <= q_chunk[:, None]
        sc = jnp.einsum('bqhd,bkhd->bhqk', qb, k, preferred_element_type=jnp.float32) * scale
        sc = jnp.where(mask[None, None], sc, -jnp.inf)
        p = jax.nn.softmax(sc, axis=-1)
        return jnp.einsum('bhqk,bkhd->bqhd', p.astype(v.dtype), v)

    out = lax.map(one_block, (q_blocks, jnp.arange(nb)))
    return out.transpose(1, 0, 2, 3, 4).reshape(b, s, h, v.shape[-1])


def mla_mixer(x, cos, sin, w_in, g_qa, w_qb, g_kva, w_kvb, g_qn, g_kn, w_o):
    b, s, _ = x.shape
    hcat = x @ w_in
    cq = hcat[..., :Q_LORA]
    ckv = hcat[..., Q_LORA:Q_LORA + KV_LORA]
    k_rope = hcat[..., Q_LORA + KV_LORA:]
    q = (rms_norm(cq, g_qa) @ w_qb).reshape(b, s, MLA_HEADS, QK_HEAD)
    kv = (rms_norm(ckv, g_kva) @ w_kvb).reshape(b, s, MLA_HEADS, QK_NOPE + V_HEAD)
    k_nope, v = kv[..., :QK_NOPE], kv[..., QK_NOPE:]
    k = jnp.concatenate(
        [k_nope, jnp.broadcast_to(k_rope[:, :, None, :], (b, s, MLA_HEADS, QK_ROPE))], axis=-1)
    q = rms_norm(q, g_qn)
    k = rms_norm(k, g_kn)
    q = jnp.concatenate([q[..., :QK_NOPE], apply_rope(q[..., QK_NOPE:], cos, sin)], axis=-1)
    k = jnp.concatenate([k[..., :QK_NOPE], apply_rope(k[..., QK_NOPE:], cos, sin)], axis=-1)
    o = chunk_causal_attention(q, k, v)
    return o.reshape(b, s, MLA_HEADS * V_HEAD) @ w_o


def rglru_mixer(x, w_in, b_in, conv_w, conv_b, w_gate_a, b_gate_a, w_gate_i, b_gate_i,
                lam, w_out, b_out):
    b, s, _ = x.shape
    hcat = x @ w_in + b_in
    gate_branch = jax.nn.gelu(hcat[..., :LRU_WIDTH], approximate=True)
    xb = hcat[..., LRU_WIDTH:]
    xp = jnp.pad(xb, ((0, 0), (CONV_WIDTH - 1, 0), (0, 0)))
    xc = conv_b + xp[:, 0:s] * conv_w[0]
    for j in range(1, CONV_WIDTH):
        xc = xc + xp[:, j:j + s] * conv_w[j]
    xh = xc.reshape(b, s, LRU_HEADS, LRU_BLOCK)
    r = jax.nn.sigmoid(jnp.einsum('bshi,hij->bshj', xh, w_gate_a) + b_gate_a).reshape(b, s, LRU_WIDTH)
    i = jax.nn.sigmoid(jnp.einsum('bshi,hij->bshj', xh, w_gate_i) + b_gate_i).reshape(b, s, LRU_WIDTH)
    log_a = LRU_C * r.astype(jnp.float32) * jax.nn.log_sigmoid(lam.astype(jnp.float32))
    a = jnp.exp(log_a)
    mult = jnp.sqrt(-jnp.expm1(2.0 * log_a))
    u = mult * (i * xc).astype(jnp.float32)

    def step(h_prev, inp):
        a_t, u_t = inp
        h_new = a_t * h_prev + u_t
        return h_new, h_new

    h0 = jnp.zeros((b, LRU_WIDTH), jnp.float32)
    _, hs = lax.scan(step, h0, (a.transpose(1, 0, 2), u.transpose(1, 0, 2)))
    y = hs.transpose(1, 0, 2).astype(x.dtype)
    return (y * gate_branch) @ w_out + b_out


def dense_swiglu(x, w1, w3, w2):
    return (jax.nn.silu(x @ w1) * (x @ w3)) @ w2


def moe_swiglu(x, w_router, b_router, w1, w3, w2):
    b, s, d = x.shape
    t = x.reshape(b * s, d)
    logits = jnp.dot(t, w_router, preferred_element_type=jnp.float32) + b_router.astype(jnp.float32)
    top_val, top_idx = lax.top_k(logits, TOP_K)
    top_w = jax.nn.softmax(top_val, axis=-1)
    gates = jnp.sum(jax.nn.one_hot(top_idx, N_EXPERTS, dtype=jnp.float32) * top_w[..., None], axis=1)
    out = jnp.zeros_like(t)
    for e in range(N_EXPERTS):
        he = jax.nn.silu(t @ w1[e]) * (t @ w3[e])
        out = out + gates[:, e:e + 1].astype(t.dtype) * (he @ w2[e])
    return out.reshape(b, s, d)


def _normal(k, shape, fan_in):
    return jax.random.normal(k, shape, jnp.float32) * (float(fan_in) ** -0.5)


def _gain(k, shape):
    return 1.0 + 0.02 * jax.random.normal(k, shape, jnp.float32)


def _bias(k, shape, scale=0.02):
    return scale * jax.random.normal(k, shape, jnp.float32)


def setup_inputs(seed: int = 0) -> dict:
    key = jax.random.key(seed)
    ks = jax.random.split(key, 40)
    d = D_MODEL
    x = jax.random.normal(ks[0], (BATCH, SEQ, d), jnp.float32)
    offset = jax.random.randint(ks[1], (BATCH, 1), 0, MAX_POS_OFFSET)
    positions = (jnp.arange(SEQ, dtype=jnp.int32)[None, :] + offset).astype(jnp.int32)

    norm_mix = _gain(ks[2], (DEPTH, d))
    norm_ffn = _gain(ks[3], (DEPTH, d))

    mla_w_in = _normal(ks[4], (N_A, d, Q_LORA + KV_LORA + QK_ROPE), d)
    mla_g_qa = _gain(ks[5], (N_A, Q_LORA))
    mla_w_qb = _normal(ks[6], (N_A, Q_LORA, MLA_HEADS * QK_HEAD), Q_LORA)
    mla_g_kva = _gain(ks[7], (N_A, KV_LORA))
    mla_w_kvb = _normal(ks[8], (N_A, KV_LORA, MLA_HEADS * (QK_NOPE + V_HEAD)), KV_LORA)
    mla_g_qn = _gain(ks[9], (N_A, QK_HEAD))
    mla_g_kn = _gain(ks[10], (N_A, QK_HEAD))
    mla_w_o = _normal(ks[11], (N_A, MLA_HEADS * V_HEAD, d), MLA_HEADS * V_HEAD)

    lru_w_in = _normal(ks[12], (N_B, d, 2 * LRU_WIDTH), d)
    lru_b_in = _bias(ks[13], (N_B, 2 * LRU_WIDTH))
    lru_conv_w = _normal(ks[14], (N_B, CONV_WIDTH, LRU_WIDTH), CONV_WIDTH)
    lru_conv_b = _bias(ks[15], (N_B, LRU_WIDTH))
    lru_w_gate_a = _normal(ks[16], (N_B, LRU_HEADS, LRU_BLOCK, LRU_BLOCK), LRU_BLOCK)
    lru_b_gate_a = _bias(ks[17], (N_B, LRU_HEADS, LRU_BLOCK))
    lru_w_gate_i = _normal(ks[18], (N_B, LRU_HEADS, LRU_BLOCK, LRU_BLOCK), LRU_BLOCK)
    lru_b_gate_i = _bias(ks[19], (N_B, LRU_HEADS, LRU_BLOCK))
    u = jax.random.uniform(ks[20], (N_B, LRU_WIDTH), jnp.float32, minval=0.9, maxval=0.999)
    sig = u ** (1.0 / LRU_C)
    lru_lambda = jnp.log(sig) - jnp.log1p(-sig)
    lru_w_out = _normal(ks[21], (N_B, LRU_WIDTH, d), LRU_WIDTH)
    lru_b_out = _bias(ks[22], (N_B, d))

    ffn_w1 = _normal(ks[23], (N_A, d, DENSE_FF), d)
    ffn_w3 = _normal(ks[24], (N_A, d, DENSE_FF), d)
    ffn_w2 = _normal(ks[25], (N_A, DENSE_FF, d), DENSE_FF)

    moe_w_router = _normal(ks[26], (N_B, d, N_EXPERTS), d)
    moe_b_router = _bias(ks[27], (N_B, N_EXPERTS), 0.01)
    moe_w1 = _normal(ks[28], (N_B, N_EXPERTS, d, EXPERT_FF), d)
    moe_w3 = _normal(ks[29], (N_B, N_EXPERTS, d, EXPERT_FF), d)
    moe_w2 = _normal(ks[30], (N_B, N_EXPERTS, EXPERT_FF, d), EXPERT_FF)

    return {
        "x": x, "positions": positions, "norm_mix": norm_mix, "norm_ffn": norm_ffn,
        "mla_w_in": mla_w_in, "mla_g_qa": mla_g_qa, "mla_w_qb": mla_w_qb,
        "mla_g_kva": mla_g_kva, "mla_w_kvb": mla_w_kvb, "mla_g_qn": mla_g_qn,
        "mla_g_kn": mla_g_kn, "mla_w_o": mla_w_o,
        "lru_w_in": lru_w_in, "lru_b_in": lru_b_in, "lru_conv_w": lru_conv_w,
        "lru_conv_b": lru_conv_b, "lru_w_gate_a": lru_w_gate_a, "lru_b_gate_a": lru_b_gate_a,
        "lru_w_gate_i": lru_w_gate_i, "lru_b_gate_i": lru_b_gate_i, "lru_lambda": lru_lambda,
        "lru_w_out": lru_w_out, "lru_b_out": lru_b_out,
        "ffn_w1": ffn_w1, "ffn_w3": ffn_w3, "ffn_w2": ffn_w2,
        "moe_w_router": moe_w_router, "moe_b_router": moe_b_router,
        "moe_w1": moe_w1, "moe_w3": moe_w3, "moe_w2": moe_w2,
    }


def reference(x, positions, norm_mix, norm_ffn,
              mla_w_in, mla_g_qa, mla_w_qb, mla_g_kva, mla_w_kvb, mla_g_qn, mla_g_kn, mla_w_o,
              lru_w_in, lru_b_in, lru_conv_w, lru_conv_b, lru_w_gate_a, lru_b_gate_a,
              lru_w_gate_i, lru_b_gate_i, lru_lambda, lru_w_out, lru_b_out,
              ffn_w1, ffn_w3, ffn_w2,
              moe_w_router, moe_b_router, moe_w1, moe_w3, moe_w2):
    cos, sin = rope_tables(positions)
    for i in range(DEPTH):
        j = i // N_MIXERS
        xn = rms_norm(x, norm_mix[i])
        if i % N_MIXERS == 0:
            x = x + mla_mixer(xn, cos, sin, mla_w_in[j], mla_g_qa[j], mla_w_qb[j], mla_g_kva[j],
                              mla_w_kvb[j], mla_g_qn[j], mla_g_kn[j], mla_w_o[j])
        else:
            x = x + rglru_mixer(xn, lru_w_in[j], lru_b_in[j], lru_conv_w[j], lru_conv_b[j],
                                lru_w_gate_a[j], lru_b_gate_a[j], lru_w_gate_i[j], lru_b_gate_i[j],
                                lru_lambda[j], lru_w_out[j], lru_b_out[j])
        xn = rms_norm(x, norm_ffn[i])
        if i % 2 == 0:
            x = x + dense_swiglu(xn, ffn_w1[j], ffn_w3[j], ffn_w2[j])
        else:
            x = x + moe_swiglu(xn, moe_w_router[j], moe_b_router[j], moe_w1[j], moe_w3[j], moe_w2[j])
    return x
```

```python
import functools
import math

import numpy as np
import jax
import jax.numpy as jnp
from jax import lax
from jax.experimental import pallas as pl
from jax.experimental.pallas import tpu as pltpu
from jax.experimental.pallas import tpu_sc as plsc

F32 = jnp.float32
BF16 = jnp.bfloat16

NORM_EPS = 1e-6
CHUNK = 64
MLA_HEADS = 8
QK_NOPE = 128
QK_ROPE = 64
QK_HEAD = QK_NOPE + QK_ROPE
V_HEAD = 128
Q_LORA = 384
KV_LORA = 256
ROPE_BASE = 10000.0
LRU_HEADS = 8
CONV_WIDTH = 4
LRU_C = 8.0
N_EXPERTS = 8

LANES = 128
SUBLANES = 8
HEAD_PAD = 2 * LANES
VMEM_LIMIT = 56 * 1024 * 1024

TM_QKV = 256
TQ = 256
TM_OPROJ = 512
TM_FFN = 1024
TF_FFN = 256
T_LRU = 64
LRU_PITCH = T_LRU + SUBLANES
TR = 512
TM_MOE = 1024
TSUB_MOE = 256
TF_MOE = 256
SC_WINDOW = 128
SC_PLANES = 4


def _cparams(sem):
    return pltpu.CompilerParams(dimension_semantics=sem, vmem_limit_bytes=VMEM_LIMIT)


def _rms(x, g):
    return x * lax.rsqrt(jnp.mean(x * x, axis=-1, keepdims=True) + NORM_EPS) * g


def _dot(a, b):
    return jnp.dot(a, b, preferred_element_type=F32)


def _dot_nt(a, b):
    return lax.dot_general(a, b, (((1,), (1,)), ((), ())), preferred_element_type=F32)


def _rope_kernel(pos_ref, inv_ref, cs_ref):
    ang = inv_ref[...] * pos_ref[...].astype(F32)
    c = jnp.cos(ang).T
    s = jnp.sin(ang).T
    lane = lax.broadcasted_iota(jnp.int32, c.shape, 1)
    half = QK_ROPE // 2
    cs_ref[:, 0:LANES] = jnp.where(lane < QK_ROPE, c, 0.0)
    cs_ref[:, LANES:2 * LANES] = jnp.where(lane < half, -s, 0.0)
    cs_ref[:, 2 * LANES:3 * LANES] = jnp.where((lane >= half) & (lane < QK_ROPE), s, 0.0)


def _rope_tables(positions):
    t = positions.size
    tm = 512
    inv = 1.0 / (ROPE_BASE ** (np.arange(0, QK_ROPE, 2, dtype=np.float32) / QK_ROPE))
    inv_col = np.zeros((LANES, 1), np.float32)
    inv_col[:QK_ROPE // 2, 0] = inv
    inv_col[QK_ROPE // 2:QK_ROPE, 0] = inv
    return pl.pallas_call(
        _rope_kernel,
        grid=(t // tm,),
        in_specs=[pl.BlockSpec((1, tm), lambda i: (0, i)),
                  pl.BlockSpec((LANES, 1), lambda i: (0, 0))],
        out_specs=pl.BlockSpec((tm, 3 * LANES), lambda i: (i, 0)),
        out_shape=jax.ShapeDtypeStruct((t, 3 * LANES), F32),
        compiler_params=_cparams(("parallel",)),
        name="rope_tables",
    )(positions.reshape(1, t), jnp.asarray(inv_col))


def _rope_rot(x, cs):
    c = cs[:, 0:LANES]
    sa = cs[:, LANES:2 * LANES]
    sb = cs[:, 2 * LANES:3 * LANES]
    return (x * c + pltpu.roll(x, LANES - QK_ROPE // 2, 1) * sa
            + pltpu.roll(x, QK_ROPE // 2, 1) * sb)


def _mla_qkv_body(x, cs_ref, gmix_ref, win_ref, gqa_ref, wqb_ref, gkva_ref, wkvb_ref,
                  gqn_ref, gkn_ref, q_ref, k_ref, v_ref):
    xn = _rms(x, gmix_ref[...]).astype(BF16)
    hcat = _dot(xn, win_ref[...])
    cq = hcat[:, :Q_LORA]
    ckv = hcat[:, Q_LORA:Q_LORA + KV_LORA]
    kr = hcat[:, Q_LORA + KV_LORA + LANES:Q_LORA + KV_LORA + 2 * LANES]
    qf = _dot(_rms(cq, gqa_ref[...]).astype(BF16), wqb_ref[...])
    kvf = _dot(_rms(ckv, gkva_ref[...]).astype(BF16), wkvb_ref[...])
    cs = cs_ref[...]
    gqn_n, gqn_r = gqn_ref[:, 0:LANES], gqn_ref[:, LANES:2 * LANES]
    gkn_n, gkn_r = gkn_ref[:, 0:LANES], gkn_ref[:, LANES:2 * LANES]
    scale = 1.0 / math.sqrt(QK_HEAD)
    kr_ss = jnp.sum(kr * kr, axis=-1, keepdims=True)
    kr_rot = _rope_rot(kr * gkn_r, cs)
    for h in range(MLA_HEADS):
        qn = qf[:, h * HEAD_PAD:h * HEAD_PAD + LANES]
        qr = qf[:, h * HEAD_PAD + LANES:(h + 1) * HEAD_PAD]
        ss = jnp.sum(qn * qn + qr * qr, axis=-1, keepdims=True)
        inv = lax.rsqrt(ss * (1.0 / QK_HEAD) + NORM_EPS) * scale
        q_ref[:, h * HEAD_PAD:h * HEAD_PAD + LANES] = (qn * inv * gqn_n).astype(BF16)
        q_ref[:, h * HEAD_PAD + LANES:(h + 1) * HEAD_PAD] = (
            _rope_rot(qr * gqn_r, cs) * inv).astype(BF16)
        kn = kvf[:, h * LANES:(h + 1) * LANES]
        ssk = jnp.sum(kn * kn, axis=-1, keepdims=True) + kr_ss
        invk = lax.rsqrt(ssk * (1.0 / QK_HEAD) + NORM_EPS)
        k_ref[:, h * HEAD_PAD:h * HEAD_PAD + LANES] = (kn * invk * gkn_n).astype(BF16)
        k_ref[:, h * HEAD_PAD + LANES:(h + 1) * HEAD_PAD] = (kr_rot * invk).astype(BF16)
    v_ref[...] = kvf[:, MLA_HEADS * LANES:].astype(BF16)


def _gated_sum(x1_ref, gt_ref, y0_ref, y1_ref):
    gt = gt_ref[...]
    g0, g1 = gt[:, 0:1], gt[:, 1:2]
    y = jnp.concatenate([g0 * y0_ref[c] + g1 * y1_ref[c] for c in range(SC_PLANES)], axis=1)
    return x1_ref[...] + y


def _mla_qkv_kernel(x_ref, *rest):
    _mla_qkv_body(x_ref[...], *rest)


def _mla_qkv_combine_kernel(x1_ref, gt_ref, y0_ref, y1_ref, *rest):
    *mid, xo_ref, q_ref, k_ref, v_ref = rest
    x = _gated_sum(x1_ref, gt_ref, y0_ref, y1_ref)
    xo_ref[...] = x
    _mla_qkv_body(x, *mid, q_ref, k_ref, v_ref)


def _mla_qkv(x_parts, cs, gmix, w_in, g_qa, w_qb, g_kva, w_kvb, g_qn, g_kn):
    t, d = x_parts[0].shape
    tm = TM_QKV
    nt = t // tm
    row = lambda i: (i, 0)
    const = lambda i: (0, 0)
    h = MLA_HEADS
    weights = [gmix, w_in, g_qa, w_qb, g_kva, w_kvb, g_qn, g_kn]
    w_specs = [pl.BlockSpec(w.shape, const) for w in weights]
    qkv_specs = [pl.BlockSpec((tm, h * HEAD_PAD), row), pl.BlockSpec((tm, h * HEAD_PAD), row),
                 pl.BlockSpec((tm, h * V_HEAD), row)]
    qkv_shapes = [jax.ShapeDtypeStruct((t, h * HEAD_PAD), BF16),
                  jax.ShapeDtypeStruct((t, h * HEAD_PAD), BF16),
                  jax.ShapeDtypeStruct((t, h * V_HEAD), BF16)]
    cs_spec = pl.BlockSpec((tm, 3 * LANES), row)
    if len(x_parts) == 1:
        q, k, v = pl.pallas_call(
            _mla_qkv_kernel, grid=(nt,),
            in_specs=[pl.BlockSpec((tm, d), row), cs_spec] + w_specs,
            out_specs=qkv_specs, out_shape=qkv_shapes,
            compiler_params=_cparams(("parallel",)), name="mla_qkv",
        )(x_parts[0], cs, *weights)
        return x_parts[0], q, k, v
    x1, gates_tok, yg = x_parts
    x, q, k, v = pl.pallas_call(
        _mla_qkv_combine_kernel, grid=(nt,),
        in_specs=[pl.BlockSpec((tm, d), row), pl.BlockSpec((tm, LANES), row),
                  pl.BlockSpec((SC_PLANES, tm, d // SC_PLANES), lambda i: (0, i, 0)),
                  pl.BlockSpec((SC_PLANES, tm, d // SC_PLANES), lambda i: (0, i + nt, 0)),
                  cs_spec] + w_specs,
        out_specs=[pl.BlockSpec((tm, d), row)] + qkv_specs,
        out_shape=[jax.ShapeDtypeStruct((t, d), F32)] + qkv_shapes,
        compiler_params=_cparams(("parallel",)), name="mla_qkv_combine",
    )(x1, gates_tok, yg, yg, cs, *weights)
    return x, q, k, v


def _attn_kernel(q_ref, k_ref, v_ref, o_ref):
    i = pl.program_id(2)
    q = q_ref[...]

    def step(s, vj, m, l, acc):
        m_new = jnp.maximum(m, jnp.max(s, axis=-1, keepdims=True))
        p = jnp.exp(s - m_new)
        alpha = jnp.exp(m - m_new)
        l = alpha * l + jnp.sum(p, axis=-1, keepdims=True)
        acc = alpha * acc + _dot(p.astype(BF16), vj)
        return m_new, l, acc

    def full_tile(j, carry):
        off = pl.multiple_of(j * TQ, TQ)
        s = _dot_nt(q, k_ref[pl.ds(off, TQ), :])
        return step(s, v_ref[pl.ds(off, TQ), :], *carry)

    init = (jnp.full((TQ, 1), -jnp.inf, F32), jnp.zeros((TQ, 1), F32),
            jnp.zeros((TQ, V_HEAD), F32))
    carry = lax.fori_loop(0, i, full_tile, init)
    off = pl.multiple_of(i * TQ, TQ)
    s = _dot_nt(q, k_ref[pl.ds(off, TQ), :])
    qc = lax.broadcasted_iota(jnp.int32, (TQ, TQ), 0) // CHUNK
    kc = lax.broadcasted_iota(jnp.int32, (TQ, TQ), 1) // CHUNK
    s = jnp.where(kc <= qc, s, -jnp.inf)
    _, l, acc = step(s, v_ref[pl.ds(off, TQ), :], *carry)
    o_ref[...] = (acc / l).astype(BF16)


def _attention(q, k, v, batch, seq):
    h = MLA_HEADS
    nq = seq // TQ
    t = batch * seq
    return pl.pallas_call(
        _attn_kernel, grid=(batch, h, nq),
        in_specs=[pl.BlockSpec((TQ, HEAD_PAD), lambda b, hh, i: (b * nq + i, hh)),
                  pl.BlockSpec((seq, HEAD_PAD), lambda b, hh, i: (b, hh)),
                  pl.BlockSpec((seq, V_HEAD), lambda b, hh, i: (b, hh))],
        out_specs=pl.BlockSpec((TQ, V_HEAD), lambda b, hh, i: (b * nq + i, hh)),
        out_shape=jax.ShapeDtypeStruct((t, h * V_HEAD), BF16),
        compiler_params=_cparams(("parallel", "parallel", "arbitrary")), name="attention",
    )(q, k, v)


def _oproj_kernel(o_ref, x_ref, wo_ref, g_ref, x1_ref, xn_ref):
    x1 = x_ref[...] + _dot(o_ref[...], wo_ref[...])
    x1_ref[...] = x1
    xn_ref[...] = _rms(x1, g_ref[...]).astype(BF16)


def _oproj(o, x, w_o, g_ffn):
    t, d = x.shape
    tm = TM_OPROJ
    row = lambda i: (i, 0)
    const = lambda i: (0, 0)
    return pl.pallas_call(
        _oproj_kernel, grid=(t // tm,),
        in_specs=[pl.BlockSpec((tm, o.shape[1]), row), pl.BlockSpec((tm, d), row),
                  pl.BlockSpec(w_o.shape, const), pl.BlockSpec((1, d), const)],
        out_specs=[pl.BlockSpec((tm, d), row), pl.BlockSpec((tm, d), row)],
        out_shape=[jax.ShapeDtypeStruct((t, d), F32), jax.ShapeDtypeStruct((t, d), BF16)],
        compiler_params=_cparams(("parallel",)), name="attn_oproj",
    )(o, x, w_o, g_ffn)


def _swiglu_chunk(x, w1, w3, w2):
    a = _dot(x, w1.astype(BF16))
    b = _dot(x, w3.astype(BF16))
    hid = (a * jax.nn.sigmoid(a)) * b
    return _dot(hid.astype(BF16), w2.astype(BF16))


def _ffn_kernel(xn_ref, x1_ref, w1_ref, w3_ref, w2_ref, o_ref):
    @pl.when(pl.program_id(1) == 0)
    def _():
        o_ref[...] = x1_ref[...]

    o_ref[...] += _swiglu_chunk(xn_ref[...], w1_ref[...], w3_ref[...], w2_ref[...])


def _dense_ffn(xn, x1, w1, w3, w2):
    t, d = x1.shape
    ff = w1.shape[1]
    tm, tf = TM_FFN, TF_FFN
    return pl.pallas_call(
        _ffn_kernel, grid=(t // tm, ff // tf),
        in_specs=[pl.BlockSpec((tm, d), lambda i, f: (i, 0)),
                  pl.BlockSpec((tm, d), lambda i, f: (i, 0)),
                  pl.BlockSpec((d, tf), lambda i, f: (0, f)),
                  pl.BlockSpec((d, tf), lambda i, f: (0, f)),
                  pl.BlockSpec((tf, d), lambda i, f: (f, 0))],
        out_specs=pl.BlockSpec((tm, d), lambda i, f: (i, 0)),
        out_shape=jax.ShapeDtypeStruct((t, d), F32),
        compiler_params=_cparams(("parallel", "arbitrary")), name="dense_ffn",
    )(xn, x1, w1, w3, w2)


def _lru_kernel(x_ref, gmix_ref, win_ref, bin_ref, cw_ref, cb_ref, wga_ref, bga_ref,
                wgi_ref, bgi_ref, lam_ref, wout_ref, bout_ref, gffn_ref,
                x1_ref, xn_ref,
                xpad_ref, a_ref, u_ref, gate_ref, yg_ref, h_ref):
    nb, tt, d = x_ref.shape
    w = lam_ref.shape[1]
    m = nb * tt

    @pl.when(pl.program_id(0) == 0)
    def _():
        h_ref[...] = jnp.zeros_like(h_ref)
        xpad_ref[:, 0:SUBLANES, :] = jnp.zeros((nb, SUBLANES, w), F32)

    x = x_ref[...].reshape(m, d)
    xn = _rms(x, gmix_ref[...]).astype(BF16)
    hcat = _dot(xn, win_ref[...]) + bin_ref[...]
    gate_ref[...] = jax.nn.gelu(hcat[:, :w], approximate=True)
    xpad_ref[:, SUBLANES:, :] = hcat[:, w:].reshape(nb, tt, w)

    xc = cb_ref[...].reshape(1, 1, w) + jnp.zeros((nb, tt, w), F32)
    for j in range(CONV_WIDTH):
        lo = SUBLANES - (CONV_WIDTH - 1) + j
        xc = xc + xpad_ref[:, lo:lo + tt, :] * cw_ref[j:j + 1, :].reshape(1, 1, w)
    xpad_ref[:, 0:SUBLANES, :] = xpad_ref[:, tt:tt + SUBLANES, :]
    xc = xc.reshape(m, w)

    lam = lam_ref[...]
    log_sig = jnp.minimum(lam, 0.0) - jnp.log1p(jnp.exp(-jnp.abs(lam)))
    for hh in range(LRU_HEADS):
        sl = slice(hh * LANES, (hh + 1) * LANES)
        xh = xc[:, sl]
        xhb = xh.astype(BF16)
        r = jax.nn.sigmoid(_dot(xhb, wga_ref[hh]) + bga_ref[hh:hh + 1, :])
        ig = jax.nn.sigmoid(_dot(xhb, wgi_ref[hh]) + bgi_ref[hh:hh + 1, :])
        log_a = LRU_C * r * log_sig[:, sl]
        a = jnp.exp(log_a)
        u = jnp.sqrt(1.0 - a * a) * (ig * xh)
        for b in range(nb):
            a_ref[hh, b * LRU_PITCH:b * LRU_PITCH + tt, :] = a[b * tt:(b + 1) * tt, :]
            u_ref[hh, b * LRU_PITCH:b * LRU_PITCH + tt, :] = u[b * tt:(b + 1) * tt, :]

    def scan_step(t, hs):
        new = []
        for hh in range(LRU_HEADS):
            rows = pl.ds(t, nb, stride=LRU_PITCH)
            hv = a_ref[hh, rows, :] * hs[hh] + u_ref[hh, rows, :]
            u_ref[hh, rows, :] = hv
            new.append(hv)
        return tuple(new)

    hs = lax.fori_loop(0, tt, scan_step, tuple(h_ref[hh] for hh in range(LRU_HEADS)))
    for hh in range(LRU_HEADS):
        h_ref[hh] = hs[hh]

    for hh in range(LRU_HEADS):
        sl = slice(hh * LANES, (hh + 1) * LANES)
        for b in range(nb):
            y = u_ref[hh, b * LRU_PITCH:b * LRU_PITCH + tt, :]
            yg_ref[b * tt:(b + 1) * tt, sl] = (y * gate_ref[b * tt:(b + 1) * tt, sl]).astype(BF16)

    x1 = x + _dot(yg_ref[...], wout_ref[...]) + bout_ref[...]
    x1_ref[...] = x1.reshape(nb, tt, d)
    xn_ref[...] = _rms(x1, gffn_ref[...]).reshape(nb, tt, d)


def _lru_mixer(x3, gmix, w_in, b_in, conv_w, conv_b, wga, bga, wgi, bgi, lam, w_out, b_out, g_ffn):
    nb, seq, d = x3.shape
    w = lam.shape[1]
    tt = T_LRU
    m = nb * tt
    consts = [gmix, w_in, b_in, conv_w, conv_b, wga, bga, wgi, bgi, lam, w_out, b_out, g_ffn]
    const_specs = [pl.BlockSpec(c.shape, (lambda i, n=c.ndim: (0,) * n)) for c in consts]
    blk = pl.BlockSpec((nb, tt, d), lambda i: (0, i, 0))
    return pl.pallas_call(
        _lru_kernel, grid=(seq // tt,),
        in_specs=[blk] + const_specs,
        out_specs=[blk, blk],
        out_shape=[jax.ShapeDtypeStruct((nb, seq, d), F32), jax.ShapeDtypeStruct((nb, seq, d), F32)],
        scratch_shapes=[pltpu.VMEM((nb, tt + SUBLANES, w), F32),
                        pltpu.VMEM((LRU_HEADS, nb * LRU_PITCH, LANES), F32),
                        pltpu.VMEM((LRU_HEADS, nb * LRU_PITCH, LANES), F32),
                        pltpu.VMEM((m, w), F32),
                        pltpu.VMEM((m, w), BF16),
                        pltpu.VMEM((LRU_HEADS, nb, LANES), F32)],
        compiler_params=_cparams(("arbitrary",)), name="rglru_mixer",
    )(x3, *consts)


def _router_kernel(xn_ref, wh_ref, wl_ref, br_ref, slot_ref, gate_ref, te_ref, tv_ref,
                   cnt_ref, run_ref, start_ref, *, tile):
    phase = pl.program_id(0)
    j = pl.program_id(1)
    ne = N_EXPERTS
    tr = xn_ref.shape[0]

    @pl.when((phase == 0) & (j == 0))
    def _():
        cnt_ref[...] = jnp.zeros_like(cnt_ref)

    x = xn_ref[...]
    xh = x.astype(BF16)
    xl = (x - xh.astype(F32)).astype(BF16)
    wh = wh_ref[...]
    logits = _dot_nt(wh, xh) + _dot_nt(wh, xl) + _dot_nt(wl_ref[...], xh) + br_ref[...]

    eidx = lax.broadcasted_iota(jnp.int32, (ne, tr), 0)
    m1 = jnp.max(logits, axis=0, keepdims=True)
    i1 = jnp.min(jnp.where(logits == m1, eidx, ne), axis=0, keepdims=True)
    oh1 = eidx == i1
    rest = jnp.where(oh1, -jnp.inf, logits)
    m2 = jnp.max(rest, axis=0, keepdims=True)
    i2 = jnp.min(jnp.where(rest == m2, eidx, ne), axis=0, keepdims=True)
    oh2 = eidx == i2
    oh = jnp.where(oh1 | oh2, 1.0, 0.0)
    tile_cnt = jnp.sum(oh, axis=1, keepdims=True)

    @pl.when(phase == 0)
    def _():
        cnt_ref[...] += tile_cnt

    @pl.when((phase == 1) & (j == 0))
    def _():
        cnt = cnt_ref[...]
        padded = jnp.ceil(cnt * (1.0 / tile)) * tile
        sub = lax.broadcasted_iota(jnp.int32, (ne, 1), 0)
        start = jnp.zeros((ne, 1), F32)
        for e in range(ne - 1):
            start = start + jnp.where(sub > e, padded[e:e + 1, :], 0.0)
        start_ref[...] = start
        run_ref[...] = jnp.zeros_like(run_ref)
        tile_start = lax.broadcasted_iota(jnp.int32, (ne, LANES), 1).astype(F32) * tile
        owner = jnp.sum(jnp.where(tile_start >= start + padded, 1, 0), axis=0, keepdims=True)
        owner = jnp.minimum(owner, ne - 1)
        esub = lax.broadcasted_iota(jnp.int32, (ne, LANES), 0)
        real = jnp.clip(cnt - (tile_start - start), 0.0, float(tile))
        te_ref[...] = owner
        tv_ref[...] = jnp.sum(jnp.where(esub == owner, real, 0.0), axis=0,
                              keepdims=True).astype(jnp.int32)

    @pl.when(phase == 1)
    def _():
        tri = jnp.where(lax.broadcasted_iota(jnp.int32, (tr, tr), 0)
                        < lax.broadcasted_iota(jnp.int32, (tr, tr), 1), 1.0, 0.0).astype(BF16)
        before = _dot(oh.astype(BF16), tri)
        slot_e = start_ref[...] + run_ref[...] + before
        s1 = jnp.sum(jnp.where(oh1, slot_e, 0.0), axis=0, keepdims=True)
        s2 = jnp.sum(jnp.where(oh2, slot_e, 0.0), axis=0, keepdims=True)
        slot_ref[0:1, :] = s1.astype(jnp.int32)
        slot_ref[1:2, :] = s2.astype(jnp.int32)
        e21 = jnp.exp(m2 - m1)
        g1 = 1.0 / (1.0 + e21)
        g2 = e21 * g1
        rowi = lax.broadcasted_iota(jnp.int32, (LANES, tr), 0)
        gmat = jnp.where(rowi == 0, g1, jnp.where(rowi == 1, g2, 0.0))
        gate_ref[...] = gmat.T
        run_ref[...] += tile_cnt


def _router(xn, w_router, b_router, tile):
    t, d = xn.shape
    tr = TR
    nt = t // tr
    wt = w_router.T
    wh = wt.astype(BF16)
    wl = (wt - wh.astype(F32)).astype(BF16)
    const = lambda p, j: (0, 0)
    return pl.pallas_call(
        functools.partial(_router_kernel, tile=tile), grid=(2, nt),
        in_specs=[pl.BlockSpec((tr, d), lambda p, j: (j, 0)),
                  pl.BlockSpec(wh.shape, const), pl.BlockSpec(wl.shape, const),
                  pl.BlockSpec((N_EXPERTS, 1), const)],
        out_specs=[pl.BlockSpec((2, tr), lambda p, j: (0, j * p)),
                   pl.BlockSpec((tr, LANES), lambda p, j: (j * p, 0)),
                   pl.BlockSpec((1, LANES), const), pl.BlockSpec((1, LANES), const)],
        out_shape=[jax.ShapeDtypeStruct((2, t), jnp.int32),
                   jax.ShapeDtypeStruct((t, LANES), F32),
                   jax.ShapeDtypeStruct((1, LANES), jnp.int32),
                   jax.ShapeDtypeStruct((1, LANES), jnp.int32)],
        scratch_shapes=[pltpu.VMEM((N_EXPERTS, 1), F32), pltpu.VMEM((N_EXPERTS, 1), F32),
                        pltpu.VMEM((N_EXPERTS, 1), F32)],
        compiler_params=_cparams(("arbitrary", "arbitrary")), name="moe_router",
    )(xn, wh, wl, b_router.reshape(N_EXPERTS, 1))


def _sc_mesh():
    return plsc.VectorSubcoreMesh(core_axis_name="c", subcore_axis_name="s")


def _sc_dispatch(xn, slots, n_slots):
    t, d = xn.shape
    win = SC_WINDOW
    nwin = t // win
    dp = d // SC_PLANES
    idx = slots.reshape(1, 2 * t)

    @functools.partial(pl.kernel,
                       out_type=jax.ShapeDtypeStruct((SC_PLANES, n_slots, dp), xn.dtype),
                       mesh=_sc_mesh(), scratch_types=[], name="moe_dispatch")
    def run(x_hbm, i_hbm, o_hbm):
        for c in range(SC_PLANES):
            def body(x_vmem, i0_vmem, i1_vmem, c=c):
                pltpu.sync_copy(x_vmem, o_hbm.at[c].at[i0_vmem.at[0]])
                pltpu.sync_copy(x_vmem, o_hbm.at[c].at[i1_vmem.at[0]])

            pltpu.emit_pipeline(
                body, grid=(nwin,),
                in_specs=[pl.BlockSpec((win, dp), lambda i, c=c: (i, c)),
                          pl.BlockSpec((1, win), lambda i: (0, i)),
                          pl.BlockSpec((1, win), lambda i: (0, i + nwin))],
                out_specs=[],
                core_axis_name=("c", "s"),
                dimension_semantics=(pltpu.PARALLEL,),
            )(x_hbm, i_hbm, i_hbm)

    return run(xn, idx)


def _sc_combine(y, slots):
    n2 = slots.size
    dp = y.shape[2]
    win = SC_WINDOW
    idx = slots.reshape(1, n2)

    @functools.partial(pl.kernel, out_type=jax.ShapeDtypeStruct((SC_PLANES, n2, dp), y.dtype),
                       mesh=_sc_mesh(), scratch_types=[], name="moe_combine")
    def run(y_hbm, i_hbm, o_hbm):
        for c in range(SC_PLANES):
            def body(i_vmem, o_vmem, c=c):
                pltpu.sync_copy(y_hbm.at[c].at[i_vmem.at[0]], o_vmem)

            pltpu.emit_pipeline(
                body, grid=(n2 // win,),
                in_specs=[pl.BlockSpec((1, win), lambda i: (0, i))],
                out_specs=[pl.BlockSpec((win, dp), lambda i: (i, 0))],
                core_axis_name=("c", "s"),
                dimension_semantics=(pltpu.PARALLEL,),
            )(i_hbm, o_hbm.at[c])

    return run(y, idx)


def _moe_ffn_kernel(te_ref, tv_ref, xs_ref, w1_ref, w3_ref, w2_ref, y_ref, xb_ref):
    i = pl.program_id(0)
    f = pl.program_id(1)
    valid = tv_ref[i]
    tm, dp = xs_ref.shape[1:]
    ts = TSUB_MOE

    @pl.when(f == 0)
    def _():
        rows = lax.broadcasted_iota(jnp.int32, (tm, 1), 0)
        for c in range(SC_PLANES):
            xb_ref[:, c * dp:(c + 1) * dp] = jnp.where(rows < valid, xs_ref[c], 0.0).astype(BF16)
        y_ref[...] = jnp.zeros_like(y_ref)

    for s in range(tm // ts):
        @pl.when(s * ts < valid)
        def _():
            y = _swiglu_chunk(xb_ref[s * ts:(s + 1) * ts, :], w1_ref[...], w3_ref[...], w2_ref[...])
            for c in range(SC_PLANES):
                y_ref[c, s * ts:(s + 1) * ts, :] += y[:, c * dp:(c + 1) * dp]


def _moe_ffn(xs, tile_expert, tile_valid, w1, w3, w2):
    planes, ns, dp = xs.shape
    d = planes * dp
    ff = w1.shape[2]
    tm, tf = TM_MOE, TF_MOE
    grid_spec = pltpu.PrefetchScalarGridSpec(
        num_scalar_prefetch=2, grid=(ns // tm, ff // tf),
        in_specs=[pl.BlockSpec((planes, tm, dp), lambda i, f, te, tv: (0, i, 0)),
                  pl.BlockSpec((None, d, tf), lambda i, f, te, tv: (te[i], 0, f)),
                  pl.BlockSpec((None, d, tf), lambda i, f, te, tv: (te[i], 0, f)),
                  pl.BlockSpec((None, tf, d), lambda i, f, te, tv: (te[i], f, 0))],
        out_specs=pl.BlockSpec((planes, tm, dp), lambda i, f, te, tv: (0, i, 0)),
        scratch_shapes=[pltpu.VMEM((tm, d), BF16)])
    return pl.pallas_call(
        _moe_ffn_kernel, grid_spec=grid_spec,
        out_shape=jax.ShapeDtypeStruct((planes, ns, dp), F32),
        compiler_params=_cparams(("arbitrary", "arbitrary")), name="moe_ffn",
    )(tile_expert, tile_valid, xs, w1, w3, w2)


def _combine_kernel(x1_ref, gt_ref, y0_ref, y1_ref, o_ref):
    o_ref[...] = _gated_sum(x1_ref, gt_ref, y0_ref, y1_ref)


def _combine(x1, gates_tok, yg):
    t, d = x1.shape
    tm = 512
    nt = t // tm
    row = lambda i: (i, 0)
    return pl.pallas_call(
        _combine_kernel, grid=(nt,),
        in_specs=[pl.BlockSpec((tm, d), row), pl.BlockSpec((tm, LANES), row),
                  pl.BlockSpec((SC_PLANES, tm, d // SC_PLANES), lambda i: (0, i, 0)),
                  pl.BlockSpec((SC_PLANES, tm, d // SC_PLANES), lambda i: (0, i + nt, 0))],
        out_specs=pl.BlockSpec((tm, d), row),
        out_shape=jax.ShapeDtypeStruct((t, d), F32),
        compiler_params=_cparams(("parallel",)), name="moe_combine_residual",
    )(x1, gates_tok, yg, yg)


def _mla_weights(w_in, w_qb, w_kvb, g_qn, g_kn):
    d = w_in.shape[0]
    h = MLA_HEADS
    lat = Q_LORA + KV_LORA
    win = jnp.zeros((d, lat + HEAD_PAD), F32)
    win = win.at[:, :lat].set(w_in[:, :lat])
    win = win.at[:, lat + LANES:lat + LANES + QK_ROPE].set(w_in[:, lat:])
    wq = w_qb.reshape(Q_LORA, h, QK_HEAD)
    wq = jnp.pad(wq, ((0, 0), (0, 0), (0, HEAD_PAD - QK_HEAD))).reshape(Q_LORA, h * HEAD_PAD)
    wkv = w_kvb.reshape(KV_LORA, h, QK_NOPE + V_HEAD)
    wkv = jnp.concatenate([wkv[:, :, :QK_NOPE].reshape(KV_LORA, h * QK_NOPE),
                           wkv[:, :, QK_NOPE:].reshape(KV_LORA, h * V_HEAD)], axis=1)
    pad = lambda g: jnp.pad(g, (0, HEAD_PAD - QK_HEAD)).reshape(1, HEAD_PAD)
    return win.astype(BF16), wq.astype(BF16), wkv.astype(BF16), pad(g_qn), pad(g_kn)


def kernel(x, positions, norm_mix, norm_ffn, mla_w_in, mla_g_qa, mla_w_qb, mla_g_kva, mla_w_kvb, mla_g_qn, mla_g_kn, mla_w_o, lru_w_in, lru_b_in, lru_conv_w, lru_conv_b, lru_w_gate_a, lru_b_gate_a, lru_w_gate_i, lru_b_gate_i, lru_lambda, lru_w_out, lru_b_out, ffn_w1, ffn_w3, ffn_w2, moe_w_router, moe_b_router, moe_w1, moe_w3, moe_w2):
    batch, seq, d = x.shape
    t = batch * seq
    depth = norm_mix.shape[0]
    n_slots = 2 * t + N_EXPERTS * TM_MOE
    n_tiles = n_slots // TM_MOE
    row = lambda v: v.reshape(1, -1)

    cs = _rope_tables(positions)
    parts = (x.reshape(t, d),)
    for i in range(depth):
        j = i // 2
        if i % 2 == 0:
            w_in, w_qb, w_kvb, g_qn, g_kn = _mla_weights(
                mla_w_in[j], mla_w_qb[j], mla_w_kvb[j], mla_g_qn[j], mla_g_kn[j])
            xr, q, k, v = _mla_qkv(parts, cs, row(norm_mix[i]), w_in, row(mla_g_qa[j]), w_qb,
                                   row(mla_g_kva[j]), w_kvb, g_qn, g_kn)
            o = _attention(q, k, v, batch, seq)
            x1, xn = _oproj(o, xr, mla_w_o[j].astype(BF16), row(norm_ffn[i]))
            parts = (_dense_ffn(xn, x1, ffn_w1[j], ffn_w3[j], ffn_w2[j]),)
        else:
            (xr,) = parts
            x1, xn = _lru_mixer(
                xr.reshape(batch, seq, d), row(norm_mix[i]), lru_w_in[j].astype(BF16),
                row(lru_b_in[j]), lru_conv_w[j], row(lru_conv_b[j]),
                lru_w_gate_a[j].astype(BF16), lru_b_gate_a[j], lru_w_gate_i[j].astype(BF16),
                lru_b_gate_i[j], row(lru_lambda[j]), lru_w_out[j].astype(BF16),
                row(lru_b_out[j]), row(norm_ffn[i]))
            x1 = x1.reshape(t, d)
            xn = xn.reshape(t, d)
            slots, gates_tok, te, tv = _router(xn, moe_w_router[j], moe_b_router[j], TM_MOE)
            xs = _sc_dispatch(xn, slots, n_slots)
            y = _moe_ffn(xs, te[0, :n_tiles], tv[0, :n_tiles], moe_w1[j], moe_w3[j], moe_w2[j])
            yg = _sc_combine(y, slots)
            parts = (x1, gates_tok, yg)
    if len(parts) == 3:
        out = _combine(*parts)
    else:
        out = parts[0]
    return out.reshape(batch, seq, d)
```

```python
import functools
import math

import numpy as np
import jax
import jax.numpy as jnp
from jax import lax
from jax.experimental import pallas as pl
from jax.experimental.pallas import tpu as pltpu
from jax.experimental.pallas import tpu_sc as plsc

F32 = jnp.float32
BF16 = jnp.bfloat16

NORM_EPS = 1e-6
CHUNK = 64
MLA_HEADS = 8
QK_NOPE = 128
QK_ROPE = 64
QK_HEAD = QK_NOPE + QK_ROPE
V_HEAD = 128
Q_LORA = 384
KV_LORA = 256
ROPE_BASE = 10000.0
LRU_HEADS = 8
CONV_WIDTH = 4
LRU_C = 8.0
N_EXPERTS = 8

LANES = 128
SUBLANES = 8
HEAD_PAD = 2 * LANES
VMEM_LIMIT = 56 * 1024 * 1024

TQ = 256
TM_OPROJ = 512
TM_FFN = 1024
TF_FFN = 256
T_LRU = 64
LRU_PITCH = T_LRU + SUBLANES
TR = 512
TM_MOE = 1024
TSUB_MOE = 256
TF_MOE = 512
SC_WINDOW = 128
SC_PLANES = 4


def _cparams(sem):
    return pltpu.CompilerParams(dimension_semantics=sem, vmem_limit_bytes=VMEM_LIMIT)


def _rms(x, g):
    return x * lax.rsqrt(jnp.mean(x * x, axis=-1, keepdims=True) + NORM_EPS) * g


def _dot(a, b):
    return jnp.dot(a, b, preferred_element_type=F32)


def _dot_nt(a, b):
    return lax.dot_general(a, b, (((1,), (1,)), ((), ())), preferred_element_type=F32)


def _rope_kernel(pos_ref, inv_ref, cs_ref):
    ang = inv_ref[...] * pos_ref[...].astype(F32)
    c = jnp.cos(ang).T
    s = jnp.sin(ang).T
    lane = lax.broadcasted_iota(jnp.int32, c.shape, 1)
    half = QK_ROPE // 2
    cs_ref[:, 0:LANES] = jnp.where(lane < QK_ROPE, c, 0.0)
    cs_ref[:, LANES:2 * LANES] = jnp.where(lane < half, -s, 0.0)
    cs_ref[:, 2 * LANES:3 * LANES] = jnp.where((lane >= half) & (lane < QK_ROPE), s, 0.0)


def _rope_tables(positions):
    t = positions.size
    tm = 512
    inv = 1.0 / (ROPE_BASE ** (np.arange(0, QK_ROPE, 2, dtype=np.float32) / QK_ROPE))
    inv_col = np.zeros((LANES, 1), np.float32)
    inv_col[:QK_ROPE // 2, 0] = inv
    inv_col[QK_ROPE // 2:QK_ROPE, 0] = inv
    return pl.pallas_call(
        _rope_kernel,
        grid=(t // tm,),
        in_specs=[pl.BlockSpec((1, tm), lambda i: (0, i)),
                  pl.BlockSpec((LANES, 1), lambda i: (0, 0))],
        out_specs=pl.BlockSpec((tm, 3 * LANES), lambda i: (i, 0)),
        out_shape=jax.ShapeDtypeStruct((t, 3 * LANES), F32),
        compiler_params=_cparams(("parallel",)),
        name="rope_tables",
    )(positions.reshape(1, t), jnp.asarray(inv_col))


def _rope_rot(x, cs):
    c = cs[:, 0:LANES]
    sa = cs[:, LANES:2 * LANES]
    sb = cs[:, 2 * LANES:3 * LANES]
    return (x * c + pltpu.roll(x, LANES - QK_ROPE // 2, 1) * sa
            + pltpu.roll(x, QK_ROPE // 2, 1) * sb)


def _mla_qkv_body(x, cs_ref, gmix_ref, win_ref, gqa_ref, wqb_ref, gkva_ref, wkb_ref, wvt_ref,
                  gqn_ref, gkn_ref, q_ref, k_ref, vt_ref):
    xn = _rms(x, gmix_ref[...]).astype(BF16)
    hcat = _dot(xn, win_ref[...])
    cq = hcat[:, :Q_LORA]
    ckv = hcat[:, Q_LORA:Q_LORA + KV_LORA]
    kr = hcat[:, Q_LORA + KV_LORA + LANES:Q_LORA + KV_LORA + 2 * LANES]
    qf = _dot(_rms(cq, gqa_ref[...]).astype(BF16), wqb_ref[...])
    ckvn = _rms(ckv, gkva_ref[...]).astype(BF16)
    kvf = _dot(ckvn, wkb_ref[...])
    vt_ref[0] = _dot_nt(wvt_ref[...], ckvn).astype(BF16)
    cs = cs_ref[...]
    gqn_n, gqn_r = gqn_ref[:, 0:LANES], gqn_ref[:, LANES:2 * LANES]
    gkn_n, gkn_r = gkn_ref[:, 0:LANES], gkn_ref[:, LANES:2 * LANES]
    scale = 1.0 / math.sqrt(QK_HEAD)
    kr_ss = jnp.sum(kr * kr, axis=-1, keepdims=True)
    kr_rot = _rope_rot(kr * gkn_r, cs)
    for h in range(MLA_HEADS):
        qn = qf[:, h * HEAD_PAD:h * HEAD_PAD + LANES]
        qr = qf[:, h * HEAD_PAD + LANES:(h + 1) * HEAD_PAD]
        ss = jnp.sum(qn * qn + qr * qr, axis=-1, keepdims=True)
        inv = lax.rsqrt(ss * (1.0 / QK_HEAD) + NORM_EPS) * scale
        q_ref[:, h * HEAD_PAD:h * HEAD_PAD + LANES] = (qn * inv * gqn_n).astype(BF16)
        q_ref[:, h * HEAD_PAD + LANES:(h + 1) * HEAD_PAD] = (
            _rope_rot(qr * gqn_r, cs) * inv).astype(BF16)
        kn = kvf[:, h * LANES:(h + 1) * LANES]
        ssk = jnp.sum(kn * kn, axis=-1, keepdims=True) + kr_ss
        invk = lax.rsqrt(ssk * (1.0 / QK_HEAD) + NORM_EPS)
        k_ref[:, h * HEAD_PAD:h * HEAD_PAD + LANES] = (kn * invk * gkn_n).astype(BF16)
        k_ref[:, h * HEAD_PAD + LANES:(h + 1) * HEAD_PAD] = (kr_rot * invk).astype(BF16)


def _gated_sum(x1_ref, gt_ref, y0_ref, y1_ref):
    gt = gt_ref[...]
    g0, g1 = gt[:, 0:1], gt[:, 1:2]
    y = jnp.concatenate([g0 * y0_ref[c] + g1 * y1_ref[c] for c in range(SC_PLANES)], axis=1)
    return x1_ref[...] + y


def _mla_qkv_kernel(x_ref, *rest):
    _mla_qkv_body(x_ref[...], *rest)


def _mla_qkv_combine_kernel(x1_ref, gt_ref, y0_ref, y1_ref, *rest):
    *mid, xo_ref, q_ref, k_ref, v_ref = rest
    x = _gated_sum(x1_ref, gt_ref, y0_ref, y1_ref)
    xo_ref[...] = x
    _mla_qkv_body(x, *mid, q_ref, k_ref, v_ref)


def _mla_qkv(x_parts, cs, gmix, w_in, g_qa, w_qb, g_kva, w_kb, w_vt, g_qn, g_kn):
    t, d = x_parts[0].shape
    tm = TQ
    nt = t // tm
    row = lambda i: (i, 0)
    const = lambda i: (0, 0)
    h = MLA_HEADS
    weights = [gmix, w_in, g_qa, w_qb, g_kva, w_kb, w_vt, g_qn, g_kn]
    w_specs = [pl.BlockSpec(w.shape, const) for w in weights]
    qkv_specs = [pl.BlockSpec((tm, h * HEAD_PAD), row), pl.BlockSpec((tm, h * HEAD_PAD), row),
                 pl.BlockSpec((1, h * V_HEAD, tm), lambda i: (i, 0, 0))]
    qkv_shapes = [jax.ShapeDtypeStruct((t, h * HEAD_PAD), BF16),
                  jax.ShapeDtypeStruct((t, h * HEAD_PAD), BF16),
                  jax.ShapeDtypeStruct((nt, h * V_HEAD, tm), BF16)]
    cs_spec = pl.BlockSpec((tm, 3 * LANES), row)
    if len(x_parts) == 1:
        q, k, v = pl.pallas_call(
            _mla_qkv_kernel, grid=(nt,),
            in_specs=[pl.BlockSpec((tm, d), row), cs_spec] + w_specs,
            out_specs=qkv_specs, out_shape=qkv_shapes,
            compiler_params=_cparams(("parallel",)), name="mla_qkv",
        )(x_parts[0], cs, *weights)
        return x_parts[0], q, k, v
    x1, gates_tok, yg = x_parts
    x, q, k, v = pl.pallas_call(
        _mla_qkv_combine_kernel, grid=(nt,),
        in_specs=[pl.BlockSpec((tm, d), row), pl.BlockSpec((tm, LANES), row),
                  pl.BlockSpec((SC_PLANES, tm, d // SC_PLANES), lambda i: (0, i, 0)),
                  pl.BlockSpec((SC_PLANES, tm, d // SC_PLANES), lambda i: (0, i + nt, 0)),
                  cs_spec] + w_specs,
        out_specs=[pl.BlockSpec((tm, d), row)] + qkv_specs,
        out_shape=[jax.ShapeDtypeStruct((t, d), F32)] + qkv_shapes,
        compiler_params=_cparams(("parallel",)), name="mla_qkv_combine",
    )(x1, gates_tok, yg, yg, cs, *weights)
    return x, q, k, v


def _attn_kernel(q_ref, k_ref, vt_ref, o_ref, m_ref, l_ref, acc_ref):
    i = pl.program_id(1)
    m_ref[...] = jnp.full(m_ref.shape, -jnp.inf, F32)
    l_ref[...] = jnp.zeros(l_ref.shape, F32)
    acc_ref[...] = jnp.zeros(acc_ref.shape, F32)

    def head_step(h, j, masked):
        off = pl.multiple_of(j * TQ, TQ)
        qh = q_ref[:, h * HEAD_PAD:(h + 1) * HEAD_PAD]
        st = _dot_nt(k_ref[pl.ds(off, TQ), h * HEAD_PAD:(h + 1) * HEAD_PAD], qh)
        if masked:
            kc = lax.broadcasted_iota(jnp.int32, (TQ, TQ), 0) // CHUNK
            qc = lax.broadcasted_iota(jnp.int32, (TQ, TQ), 1) // CHUNK
            st = jnp.where(kc <= qc, st, -jnp.inf)
        m_old = m_ref[h]
        m_new = jnp.maximum(m_old, jnp.max(st, axis=0, keepdims=True))
        p = jnp.exp(st - m_new)
        alpha = jnp.exp(m_old - m_new)
        l_ref[h] = alpha * l_ref[h] + jnp.sum(p, axis=0, keepdims=True)
        pv = _dot(vt_ref[j, h * V_HEAD:(h + 1) * V_HEAD, :], p.astype(BF16))
        acc_ref[h] = alpha * acc_ref[h] + pv
        m_ref[h] = m_new

    def full_tile(j, carry):
        for h in range(MLA_HEADS):
            head_step(h, j, False)
        return carry

    lax.fori_loop(0, i, full_tile, 0)
    for h in range(MLA_HEADS):
        head_step(h, i, True)
        o_ref[:, h * V_HEAD:(h + 1) * V_HEAD] = (acc_ref[h] / l_ref[h]).T.astype(BF16)


def _attention(q, k, vt, batch, seq):
    h = MLA_HEADS
    nq = seq // TQ
    t = batch * seq
    return pl.pallas_call(
        _attn_kernel, grid=(batch, nq),
        in_specs=[pl.BlockSpec((TQ, h * HEAD_PAD), lambda b, i: (b * nq + i, 0)),
                  pl.BlockSpec((seq, h * HEAD_PAD), lambda b, i: (b, 0)),
                  pl.BlockSpec((nq, h * V_HEAD, TQ), lambda b, i: (b, 0, 0))],
        out_specs=pl.BlockSpec((TQ, h * V_HEAD), lambda b, i: (b * nq + i, 0)),
        out_shape=jax.ShapeDtypeStruct((t, h * V_HEAD), BF16),
        scratch_shapes=[pltpu.VMEM((h, 1, TQ), F32), pltpu.VMEM((h, 1, TQ), F32),
                        pltpu.VMEM((h, V_HEAD, TQ), F32)],
        compiler_params=_cparams(("parallel", "arbitrary")), name="attention",
    )(q, k, vt)


def _oproj_kernel(o_ref, x_ref, wo_ref, g_ref, x1_ref, xn_ref):
    x1 = x_ref[...] + _dot(o_ref[...], wo_ref[...])
    x1_ref[...] = x1
    xn_ref[...] = _rms(x1, g_ref[...]).astype(BF16)


def _oproj(o, x, w_o, g_ffn):
    t, d = x.shape
    tm = TM_OPROJ
    row = lambda i: (i, 0)
    const = lambda i: (0, 0)
    return pl.pallas_call(
        _oproj_kernel, grid=(t // tm,),
        in_specs=[pl.BlockSpec((tm, o.shape[1]), row), pl.BlockSpec((tm, d), row),
                  pl.BlockSpec(w_o.shape, const), pl.BlockSpec((1, d), const)],
        out_specs=[pl.BlockSpec((tm, d), row), pl.BlockSpec((tm, d), row)],
        out_shape=[jax.ShapeDtypeStruct((t, d), F32), jax.ShapeDtypeStruct((t, d), BF16)],
        compiler_params=_cparams(("parallel",)), name="attn_oproj",
    )(o, x, w_o, g_ffn)


def _swiglu_chunk(x, w1, w3, w2):
    a = _dot(x, w1)
    b = _dot(x, w3)
    hid = (a * jax.nn.sigmoid(a)) * b
    return _dot(hid.astype(BF16), w2)


def _ffn_kernel(xn_ref, x1_ref, w1_ref, w3_ref, w2_ref, o_ref):
    @pl.when(pl.program_id(1) == 0)
    def _():
        o_ref[...] = x1_ref[...]

    o_ref[...] += _swiglu_chunk(xn_ref[...], w1_ref[...].astype(BF16), w3_ref[...].astype(BF16),
                                w2_ref[...].astype(BF16))


def _dense_ffn(xn, x1, w1, w3, w2, layer):
    t, d = x1.shape
    ff = w1.shape[2]
    tm, tf = TM_FFN, TF_FFN
    return pl.pallas_call(
        _ffn_kernel, grid=(t // tm, ff // tf),
        in_specs=[pl.BlockSpec((tm, d), lambda i, f: (i, 0)),
                  pl.BlockSpec((tm, d), lambda i, f: (i, 0)),
                  pl.BlockSpec((None, d, tf), lambda i, f: (layer, 0, f)),
                  pl.BlockSpec((None, d, tf), lambda i, f: (layer, 0, f)),
                  pl.BlockSpec((None, tf, d), lambda i, f: (layer, f, 0))],
        out_specs=pl.BlockSpec((tm, d), lambda i, f: (i, 0)),
        out_shape=jax.ShapeDtypeStruct((t, d), F32),
        compiler_params=_cparams(("parallel", "arbitrary")), name="dense_ffn",
    )(xn, x1, w1, w3, w2)


def _lru_kernel(x_ref, gmix_ref, win_ref, bin_ref, cw_ref, cb_ref, wga_ref, bga_ref,
                wgi_ref, bgi_ref, lam_ref, wout_ref, bout_ref, gffn_ref,
                x1_ref, xn_ref,
                xpad_ref, a_ref, u_ref, gate_ref, yg_ref, h_ref):
    nb, tt, d = x_ref.shape
    w = lam_ref.shape[1]
    m = nb * tt

    @pl.when(pl.program_id(0) == 0)
    def _():
        h_ref[...] = jnp.zeros_like(h_ref)
        xpad_ref[:, 0:SUBLANES, :] = jnp.zeros((nb, SUBLANES, w), F32)

    x = x_ref[...].reshape(m, d)
    xn = _rms(x, gmix_ref[...]).astype(BF16)
    hcat = _dot(xn, win_ref[...]) + bin_ref[...]
    gate_ref[...] = jax.nn.gelu(hcat[:, :w], approximate=True)
    xpad_ref[:, SUBLANES:, :] = hcat[:, w:].reshape(nb, tt, w)

    xc = cb_ref[...].reshape(1, 1, w) + jnp.zeros((nb, tt, w), F32)
    for j in range(CONV_WIDTH):
        lo = SUBLANES - (CONV_WIDTH - 1) + j
        xc = xc + xpad_ref[:, lo:lo + tt, :] * cw_ref[j:j + 1, :].reshape(1, 1, w)
    xpad_ref[:, 0:SUBLANES, :] = xpad_ref[:, tt:tt + SUBLANES, :]
    xc = xc.reshape(m, w)

    lam = lam_ref[...]
    log_sig = jnp.minimum(lam, 0.0) - jnp.log1p(jnp.exp(-jnp.abs(lam)))
    for hh in range(LRU_HEADS):
        sl = slice(hh * LANES, (hh + 1) * LANES)
        xh = xc[:, sl]
        xhb = xh.astype(BF16)
        r = jax.nn.sigmoid(_dot(xhb, wga_ref[hh]) + bga_ref[hh:hh + 1, :])
        ig = jax.nn.sigmoid(_dot(xhb, wgi_ref[hh]) + bgi_ref[hh:hh + 1, :])
        log_a = LRU_C * r * log_sig[:, sl]
        a = jnp.exp(log_a)
        u = jnp.sqrt(1.0 - a * a) * (ig * xh)
        for b in range(nb):
            a_ref[hh, b * LRU_PITCH:b * LRU_PITCH + tt, :] = a[b * tt:(b + 1) * tt, :]
            u_ref[hh, b * LRU_PITCH:b * LRU_PITCH + tt, :] = u[b * tt:(b + 1) * tt, :]

    def scan_step(t, hs):
        new = []
        for hh in range(LRU_HEADS):
            rows = pl.ds(t, nb, stride=LRU_PITCH)
            hv = a_ref[hh, rows, :] * hs[hh] + u_ref[hh, rows, :]
            u_ref[hh, rows, :] = hv
            new.append(hv)
        return tuple(new)

    hs = lax.fori_loop(0, tt, scan_step, tuple(h_ref[hh] for hh in range(LRU_HEADS)))
    for hh in range(LRU_HEADS):
        h_ref[hh] = hs[hh]

    for hh in range(LRU_HEADS):
        sl = slice(hh * LANES, (hh + 1) * LANES)
        for b in range(nb):
            y = u_ref[hh, b * LRU_PITCH:b * LRU_PITCH + tt, :]
            yg_ref[b * tt:(b + 1) * tt, sl] = (y * gate_ref[b * tt:(b + 1) * tt, sl]).astype(BF16)

    x1 = x + _dot(yg_ref[...], wout_ref[...]) + bout_ref[...]
    x1_ref[...] = x1.reshape(nb, tt, d)
    xn_ref[...] = _rms(x1, gffn_ref[...]).reshape(nb, tt, d)


def _lru_mixer(x3, gmix, w_in, b_in, conv_w, conv_b, wga, bga, wgi, bgi, lam, w_out, b_out, g_ffn):
    nb, seq, d = x3.shape
    w = lam.shape[1]
    tt = T_LRU
    m = nb * tt
    consts = [gmix, w_in, b_in, conv_w, conv_b, wga, bga, wgi, bgi, lam, w_out, b_out, g_ffn]
    const_specs = [pl.BlockSpec(c.shape, (lambda i, n=c.ndim: (0,) * n)) for c in consts]
    blk = pl.BlockSpec((nb, tt, d), lambda i: (0, i, 0))
    return pl.pallas_call(
        _lru_kernel, grid=(seq // tt,),
        in_specs=[blk] + const_specs,
        out_specs=[blk, blk],
        out_shape=[jax.ShapeDtypeStruct((nb, seq, d), F32), jax.ShapeDtypeStruct((nb, seq, d), F32)],
        scratch_shapes=[pltpu.VMEM((nb, tt + SUBLANES, w), F32),
                        pltpu.VMEM((LRU_HEADS, nb * LRU_PITCH, LANES), F32),
                        pltpu.VMEM((LRU_HEADS, nb * LRU_PITCH, LANES), F32),
                        pltpu.VMEM((m, w), F32),
                        pltpu.VMEM((m, w), BF16),
                        pltpu.VMEM((LRU_HEADS, nb, LANES), F32)],
        compiler_params=_cparams(("arbitrary",)), name="rglru_mixer",
    )(x3, *consts)


def _router_kernel(xn_ref, wh_ref, wl_ref, br_ref, slot_ref, gate_ref, te_ref, tv_ref,
                   cnt_ref, run_ref, start_ref, *, tile):
    phase = pl.program_id(0)
    j = pl.program_id(1)
    ne = N_EXPERTS
    tr = xn_ref.shape[0]

    @pl.when((phase == 0) & (j == 0))
    def _():
        cnt_ref[...] = jnp.zeros_like(cnt_ref)

    x = xn_ref[...]
    xh = x.astype(BF16)
    xl = (x - xh.astype(F32)).astype(BF16)
    wh = wh_ref[...]
    logits = _dot_nt(wh, xh) + _dot_nt(wh, xl) + _dot_nt(wl_ref[...], xh) + br_ref[...]

    eidx = lax.broadcasted_iota(jnp.int32, (ne, tr), 0)
    m1 = jnp.max(logits, axis=0, keepdims=True)
    i1 = jnp.min(jnp.where(logits == m1, eidx, ne), axis=0, keepdims=True)
    oh1 = eidx == i1
    rest = jnp.where(oh1, -jnp.inf, logits)
    m2 = jnp.max(rest, axis=0, keepdims=True)
    i2 = jnp.min(jnp.where(rest == m2, eidx, ne), axis=0, keepdims=True)
    oh2 = eidx == i2
    oh = jnp.where(oh1 | oh2, 1.0, 0.0)
    tile_cnt = jnp.sum(oh, axis=1, keepdims=True)

    @pl.when(phase == 0)
    def _():
        cnt_ref[...] += tile_cnt

    @pl.when((phase == 1) & (j == 0))
    def _():
        cnt = cnt_ref[...]
        padded = jnp.ceil(cnt * (1.0 / tile)) * tile
        sub = lax.broadcasted_iota(jnp.int32, (ne, 1), 0)
        start = jnp.zeros((ne, 1), F32)
        for e in range(ne - 1):
            start = start + jnp.where(sub > e, padded[e:e + 1, :], 0.0)
        start_ref[...] = start
        run_ref[...] = jnp.zeros_like(run_ref)
        tile_start = lax.broadcasted_iota(jnp.int32, (ne, LANES), 1).astype(F32) * tile
        owner = jnp.sum(jnp.where(tile_start >= start + padded, 1, 0), axis=0, keepdims=True)
        owner = jnp.minimum(owner, ne - 1)
        esub = lax.broadcasted_iota(jnp.int32, (ne, LANES), 0)
        real = jnp.clip(cnt - (tile_start - start), 0.0, float(tile))
        te_ref[...] = owner
        tv_ref[...] = jnp.sum(jnp.where(esub == owner, real, 0.0), axis=0,
                              keepdims=True).astype(jnp.int32)

    @pl.when(phase == 1)
    def _():
        tri = jnp.where(lax.broadcasted_iota(jnp.int32, (tr, tr), 0)
                        < lax.broadcasted_iota(jnp.int32, (tr, tr), 1), 1.0, 0.0).astype(BF16)
        before = _dot(oh.astype(BF16), tri)
        slot_e = start_ref[...] + run_ref[...] + before
        s1 = jnp.sum(jnp.where(oh1, slot_e, 0.0), axis=0, keepdims=True)
        s2 = jnp.sum(jnp.where(oh2, slot_e, 0.0), axis=0, keepdims=True)
        slot_ref[0:1, :] = s1.astype(jnp.int32)
        slot_ref[1:2, :] = s2.astype(jnp.int32)
        e21 = jnp.exp(m2 - m1)
        g1 = 1.0 / (1.0 + e21)
        g2 = e21 * g1
        rowi = lax.broadcasted_iota(jnp.int32, (LANES, tr), 0)
        gmat = jnp.where(rowi == 0, g1, jnp.where(rowi == 1, g2, 0.0))
        gate_ref[...] = gmat.T
        run_ref[...] += tile_cnt


def _router(xn, w_router, b_router, tile):
    t, d = xn.shape
    tr = TR
    nt = t // tr
    wt = w_router.T
    wh = wt.astype(BF16)
    wl = (wt - wh.astype(F32)).astype(BF16)
    const = lambda p, j: (0, 0)
    return pl.pallas_call(
        functools.partial(_router_kernel, tile=tile), grid=(2, nt),
        in_specs=[pl.BlockSpec((tr, d), lambda p, j: (j, 0)),
                  pl.BlockSpec(wh.shape, const), pl.BlockSpec(wl.shape, const),
                  pl.BlockSpec((N_EXPERTS, 1), const)],
        out_specs=[pl.BlockSpec((2, tr), lambda p, j: (0, j * p)),
                   pl.BlockSpec((tr, LANES), lambda p, j: (j * p, 0)),
                   pl.BlockSpec((1, LANES), const), pl.BlockSpec((1, LANES), const)],
        out_shape=[jax.ShapeDtypeStruct((2, t), jnp.int32),
                   jax.ShapeDtypeStruct((t, LANES), F32),
                   jax.ShapeDtypeStruct((1, LANES), jnp.int32),
                   jax.ShapeDtypeStruct((1, LANES), jnp.int32)],
        scratch_shapes=[pltpu.VMEM((N_EXPERTS, 1), F32), pltpu.VMEM((N_EXPERTS, 1), F32),
                        pltpu.VMEM((N_EXPERTS, 1), F32)],
        compiler_params=_cparams(("arbitrary", "arbitrary")), name="moe_router",
    )(xn, wh, wl, b_router.reshape(N_EXPERTS, 1))


def _sc_mesh():
    return plsc.VectorSubcoreMesh(core_axis_name="c", subcore_axis_name="s")


def _sc_dispatch(xn, slots, n_slots):
    t, d = xn.shape
    win = SC_WINDOW
    nwin = t // win
    dp = d // SC_PLANES
    idx = slots.reshape(1, 2 * t)

    @functools.partial(pl.kernel,
                       out_type=jax.ShapeDtypeStruct((SC_PLANES, n_slots, dp), xn.dtype),
                       mesh=_sc_mesh(), scratch_types=[], name="moe_dispatch")
    def run(x_hbm, i_hbm, o_hbm):
        for c in range(SC_PLANES):
            def body(x_vmem, i0_vmem, i1_vmem, c=c):
                pltpu.sync_copy(x_vmem, o_hbm.at[c].at[i0_vmem.at[0]])
                pltpu.sync_copy(x_vmem, o_hbm.at[c].at[i1_vmem.at[0]])

            pltpu.emit_pipeline(
                body, grid=(nwin,),
                in_specs=[pl.BlockSpec((win, dp), lambda i, c=c: (i, c)),
                          pl.BlockSpec((1, win), lambda i: (0, i)),
                          pl.BlockSpec((1, win), lambda i: (0, i + nwin))],
                out_specs=[],
                core_axis_name=("c", "s"),
                dimension_semantics=(pltpu.PARALLEL,),
            )(x_hbm, i_hbm, i_hbm)

    return run(xn, idx)


def _sc_combine(y, slots):
    n2 = slots.size
    dp = y.shape[2]
    win = SC_WINDOW
    idx = slots.reshape(1, n2)

    @functools.partial(pl.kernel, out_type=jax.ShapeDtypeStruct((SC_PLANES, n2, dp), y.dtype),
                       mesh=_sc_mesh(), scratch_types=[], name="moe_combine")
    def run(y_hbm, i_hbm, o_hbm):
        for c in range(SC_PLANES):
            def body(i_vmem, o_vmem, c=c):
                pltpu.sync_copy(y_hbm.at[c].at[i_vmem.at[0]], o_vmem)

            pltpu.emit_pipeline(
                body, grid=(n2 // win,),
                in_specs=[pl.BlockSpec((1, win), lambda i: (0, i))],
                out_specs=[pl.BlockSpec((win, dp), lambda i: (i, 0))],
                core_axis_name=("c", "s"),
                dimension_semantics=(pltpu.PARALLEL,),
            )(i_hbm, o_hbm.at[c])

    return run(y, idx)


def _moe_ffn_kernel(te_ref, tv_ref, xs_ref, w1_ref, w3_ref, w2_ref, y_ref,
                    xb_ref, wb1_ref, wb3_ref, wb2_ref):
    i = pl.program_id(0)
    f = pl.program_id(1)
    valid = tv_ref[i]
    tm, dp = xs_ref.shape[1:]
    ts = TSUB_MOE

    @pl.when(f == 0)
    def _():
        rows = lax.broadcasted_iota(jnp.int32, (tm, 1), 0)
        for c in range(SC_PLANES):
            xb_ref[:, c * dp:(c + 1) * dp] = jnp.where(rows < valid, xs_ref[c], 0.0).astype(BF16)
        y_ref[...] = jnp.zeros_like(y_ref)

    def rows_block(lo, n, w1, w3, w2):
        y = _swiglu_chunk(xb_ref[lo:lo + n, :], w1, w3, w2)
        for c in range(SC_PLANES):
            y_ref[c, lo:lo + n, :] += y[:, c * dp:(c + 1) * dp]

    @pl.when(valid == tm)
    def _():
        rows_block(0, tm, w1_ref[...].astype(BF16), w3_ref[...].astype(BF16),
                   w2_ref[...].astype(BF16))

    @pl.when((valid > 0) & (valid < tm))
    def _():
        wb1_ref[...] = w1_ref[...].astype(BF16)
        wb3_ref[...] = w3_ref[...].astype(BF16)
        wb2_ref[...] = w2_ref[...].astype(BF16)
        for s in range(tm // ts):
            @pl.when(s * ts < valid)
            def _():
                rows_block(s * ts, ts, wb1_ref[...], wb3_ref[...], wb2_ref[...])


def _moe_ffn(xs, tile_expert, tile_valid, w1, w3, w2, layer):
    planes, ns, dp = xs.shape
    d = planes * dp
    ff = w1.shape[3]
    tm, tf = TM_MOE, TF_MOE

    def chunk(i, f, tv):
        return jnp.where(tv[i] > 0, f, 0)

    grid_spec = pltpu.PrefetchScalarGridSpec(
        num_scalar_prefetch=2, grid=(ns // tm, ff // tf),
        in_specs=[pl.BlockSpec((planes, tm, dp), lambda i, f, te, tv: (0, i, 0)),
                  pl.BlockSpec((None, None, d, tf),
                               lambda i, f, te, tv: (layer, te[i], 0, chunk(i, f, tv))),
                  pl.BlockSpec((None, None, d, tf),
                               lambda i, f, te, tv: (layer, te[i], 0, chunk(i, f, tv))),
                  pl.BlockSpec((None, None, tf, d),
                               lambda i, f, te, tv: (layer, te[i], chunk(i, f, tv), 0))],
        out_specs=pl.BlockSpec((planes, tm, dp), lambda i, f, te, tv: (0, i, 0)),
        scratch_shapes=[pltpu.VMEM((tm, d), BF16), pltpu.VMEM((d, tf), BF16),
                        pltpu.VMEM((d, tf), BF16), pltpu.VMEM((tf, d), BF16)])
    return pl.pallas_call(
        _moe_ffn_kernel, grid_spec=grid_spec,
        out_shape=jax.ShapeDtypeStruct((planes, ns, dp), F32),
        compiler_params=_cparams(("arbitrary", "arbitrary")), name="moe_ffn",
    )(tile_expert, tile_valid, xs, w1, w3, w2)


def _combine_kernel(x1_ref, gt_ref, y0_ref, y1_ref, o_ref):
    o_ref[...] = _gated_sum(x1_ref, gt_ref, y0_ref, y1_ref)


def _combine(x1, gates_tok, yg):
    t, d = x1.shape
    tm = 512
    nt = t // tm
    row = lambda i: (i, 0)
    return pl.pallas_call(
        _combine_kernel, grid=(nt,),
        in_specs=[pl.BlockSpec((tm, d), row), pl.BlockSpec((tm, LANES), row),
                  pl.BlockSpec((SC_PLANES, tm, d // SC_PLANES), lambda i: (0, i, 0)),
                  pl.BlockSpec((SC_PLANES, tm, d // SC_PLANES), lambda i: (0, i + nt, 0))],
        out_specs=pl.BlockSpec((tm, d), row),
        out_shape=jax.ShapeDtypeStruct((t, d), F32),
        compiler_params=_cparams(("parallel",)), name="moe_combine_residual",
    )(x1, gates_tok, yg, yg)


def _mla_weights(w_in, w_qb, w_kvb, g_qn, g_kn):
    d = w_in.shape[0]
    h = MLA_HEADS
    lat = Q_LORA + KV_LORA
    win = jnp.zeros((d, lat + HEAD_PAD), F32)
    win = win.at[:, :lat].set(w_in[:, :lat])
    win = win.at[:, lat + LANES:lat + LANES + QK_ROPE].set(w_in[:, lat:])
    wq = w_qb.reshape(Q_LORA, h, QK_HEAD)
    wq = jnp.pad(wq, ((0, 0), (0, 0), (0, HEAD_PAD - QK_HEAD))).reshape(Q_LORA, h * HEAD_PAD)
    wkv = w_kvb.reshape(KV_LORA, h, QK_NOPE + V_HEAD)
    wk = wkv[:, :, :QK_NOPE].reshape(KV_LORA, h * QK_NOPE)
    wvt = wkv[:, :, QK_NOPE:].reshape(KV_LORA, h * V_HEAD).T
    pad = lambda g: jnp.pad(g, (0, HEAD_PAD - QK_HEAD)).reshape(1, HEAD_PAD)
    return (win.astype(BF16), wq.astype(BF16), wk.astype(BF16), wvt.astype(BF16),
            pad(g_qn), pad(g_kn))


def kernel(x, positions, norm_mix, norm_ffn, mla_w_in, mla_g_qa, mla_w_qb, mla_g_kva, mla_w_kvb, mla_g_qn, mla_g_kn, mla_w_o, lru_w_in, lru_b_in, lru_conv_w, lru_conv_b, lru_w_gate_a, lru_b_gate_a, lru_w_gate_i, lru_b_gate_i, lru_lambda, lru_w_out, lru_b_out, ffn_w1, ffn_w3, ffn_w2, moe_w_router, moe_b_router, moe_w1, moe_w3, moe_w2):
    batch, seq, d = x.shape
    t = batch * seq
    depth = norm_mix.shape[0]
    n_slots = 2 * t + N_EXPERTS * TM_MOE
    n_tiles = n_slots // TM_MOE
    row = lambda v: v.reshape(1, -1)

    cs = _rope_tables(positions)
    parts = (x.reshape(t, d),)
    for i in range(depth):
        j = i // 2
        if i % 2 == 0:
            w_in, w_qb, w_kb, w_vt, g_qn, g_kn = _mla_weights(
                mla_w_in[j], mla_w_qb[j], mla_w_kvb[j], mla_g_qn[j], mla_g_kn[j])
            xr, q, k, vt = _mla_qkv(parts, cs, row(norm_mix[i]), w_in, row(mla_g_qa[j]), w_qb,
                                    row(mla_g_kva[j]), w_kb, w_vt, g_qn, g_kn)
            o = _attention(q, k, vt, batch, seq)
            x1, xn = _oproj(o, xr, mla_w_o[j].astype(BF16), row(norm_ffn[i]))
            parts = (_dense_ffn(xn, x1, ffn_w1, ffn_w3, ffn_w2, j),)
        else:
            (xr,) = parts
            x1, xn = _lru_mixer(
                xr.reshape(batch, seq, d), row(norm_mix[i]), lru_w_in[j].astype(BF16),
                row(lru_b_in[j]), lru_conv_w[j], row(lru_conv_b[j]),
                lru_w_gate_a[j].astype(BF16), lru_b_gate_a[j], lru_w_gate_i[j].astype(BF16),
                lru_b_gate_i[j], row(lru_lambda[j]), lru_w_out[j].astype(BF16),
                row(lru_b_out[j]), row(norm_ffn[i]))
            x1 = x1.reshape(t, d)
            xn = xn.reshape(t, d)
            slots, gates_tok, te, tv = _router(xn, moe_w_router[j], moe_b_router[j], TM_MOE)
            xs = _sc_dispatch(xn, slots, n_slots)
            y = _moe_ffn(xs, te[0, :n_tiles], tv[0, :n_tiles], moe_w1, moe_w3, moe_w2, j)
            yg = _sc_combine(y, slots)
            parts = (x1, gates_tok, yg)
    if len(parts) == 3:
        out = _combine(*parts)
    else:
        out = parts[0]
    return out.reshape(batch, seq, d)
```

```python
import functools
import math

import numpy as np
import jax
import jax.numpy as jnp
from jax import lax
from jax.experimental import pallas as pl
from jax.experimental.pallas import tpu as pltpu
from jax.experimental.pallas import tpu_sc as plsc

F32 = jnp.float32
BF16 = jnp.bfloat16

NORM_EPS = 1e-6
CHUNK = 64
MLA_HEADS = 8
QK_NOPE = 128
QK_ROPE = 64
QK_HEAD = QK_NOPE + QK_ROPE
V_HEAD = 128
Q_LORA = 384
KV_LORA = 256
ROPE_BASE = 10000.0
LRU_HEADS = 8
CONV_WIDTH = 4
LRU_C = 8.0
N_EXPERTS = 8

LANES = 128
SUBLANES = 8
HEAD_PAD = 2 * LANES
VMEM_LIMIT = 56 * 1024 * 1024

TQ = 256
TM_QKV = 2 * TQ
TM_OPROJ = 512
TM_FFN = 1024
TF_FFN = 256
T_LRU = 64
LRU_PITCH = T_LRU + SUBLANES
TR = 512
TM_MOE = 1024
TSUB_MOE = 256
TF_MOE = 512
SC_WINDOW = 128
SC_PLANES = 4


def _cparams(sem):
    return pltpu.CompilerParams(dimension_semantics=sem, vmem_limit_bytes=VMEM_LIMIT)


def _rms(x, g):
    return x * lax.rsqrt(jnp.mean(x * x, axis=-1, keepdims=True) + NORM_EPS) * g


def _dot(a, b):
    return jnp.dot(a, b, preferred_element_type=F32)


def _dot_nt(a, b):
    return lax.dot_general(a, b, (((1,), (1,)), ((), ())), preferred_element_type=F32)


def _rope_kernel(pos_ref, inv_ref, cs_ref):
    ang = inv_ref[...] * pos_ref[...].astype(F32)
    c = jnp.cos(ang).T
    s = jnp.sin(ang).T
    lane = lax.broadcasted_iota(jnp.int32, c.shape, 1)
    half = QK_ROPE // 2
    cs_ref[:, 0:LANES] = jnp.where(lane < QK_ROPE, c, 0.0)
    cs_ref[:, LANES:2 * LANES] = jnp.where(lane < half, -s, 0.0)
    cs_ref[:, 2 * LANES:3 * LANES] = jnp.where((lane >= half) & (lane < QK_ROPE), s, 0.0)


def _rope_tables(positions):
    t = positions.size
    tm = 512
    inv = 1.0 / (ROPE_BASE ** (np.arange(0, QK_ROPE, 2, dtype=np.float32) / QK_ROPE))
    inv_col = np.zeros((LANES, 1), np.float32)
    inv_col[:QK_ROPE // 2, 0] = inv
    inv_col[QK_ROPE // 2:QK_ROPE, 0] = inv
    return pl.pallas_call(
        _rope_kernel,
        grid=(t // tm,),
        in_specs=[pl.BlockSpec((1, tm), lambda i: (0, i)),
                  pl.BlockSpec((LANES, 1), lambda i: (0, 0))],
        out_specs=pl.BlockSpec((tm, 3 * LANES), lambda i: (i, 0)),
        out_shape=jax.ShapeDtypeStruct((t, 3 * LANES), F32),
        compiler_params=_cparams(("parallel",)),
        name="rope_tables",
    )(positions.reshape(1, t), jnp.asarray(inv_col))


def _rope_rot(x, cs):
    c = cs[:, 0:LANES]
    sa = cs[:, LANES:2 * LANES]
    sb = cs[:, 2 * LANES:3 * LANES]
    return (x * c + pltpu.roll(x, LANES - QK_ROPE // 2, 1) * sa
            + pltpu.roll(x, QK_ROPE // 2, 1) * sb)


def _mla_latent(x, gmix_ref, win_ref):
    return _dot(_rms(x, gmix_ref[...]).astype(BF16), win_ref[...])


def _mla_expand(hcat, gqa_ref, wqb_ref, gkva_ref, wkb_ref, wvt_ref):
    cq = hcat[:, :Q_LORA]
    ckv = hcat[:, Q_LORA:Q_LORA + KV_LORA]
    kr = hcat[:, Q_LORA + KV_LORA + LANES:Q_LORA + KV_LORA + 2 * LANES]
    qf = _dot(_rms(cq, gqa_ref[...]).astype(BF16), wqb_ref[...])
    ckvn = _rms(ckv, gkva_ref[...]).astype(BF16)
    kf = _dot(ckvn, wkb_ref[...])
    vt = _dot_nt(wvt_ref[...], ckvn).astype(BF16)
    return qf, kf, kr, vt


def _mla_heads(qf, kf, kr, cs, gqn_ref, gkn_ref, q_ref, k_ref, rows):
    gqn_n, gqn_r = gqn_ref[:, 0:LANES], gqn_ref[:, LANES:2 * LANES]
    gkn_n, gkn_r = gkn_ref[:, 0:LANES], gkn_ref[:, LANES:2 * LANES]
    scale = math.log2(math.e) / math.sqrt(QK_HEAD)
    kr_ss = jnp.sum(kr * kr, axis=-1, keepdims=True)
    kr_rot = _rope_rot(kr * gkn_r, cs)
    for h in range(MLA_HEADS):
        qn = qf[:, h * HEAD_PAD:h * HEAD_PAD + LANES]
        qr = qf[:, h * HEAD_PAD + LANES:(h + 1) * HEAD_PAD]
        ss = jnp.sum(qn * qn + qr * qr, axis=-1, keepdims=True)
        inv = lax.rsqrt(ss * (1.0 / QK_HEAD) + NORM_EPS) * scale
        q_ref[rows, h * HEAD_PAD:h * HEAD_PAD + LANES] = (qn * inv * gqn_n).astype(BF16)
        q_ref[rows, h * HEAD_PAD + LANES:(h + 1) * HEAD_PAD] = (
            _rope_rot(qr * gqn_r, cs) * inv).astype(BF16)
        kn = kf[:, h * LANES:(h + 1) * LANES]
        ssk = jnp.sum(kn * kn, axis=-1, keepdims=True) + kr_ss
        invk = lax.rsqrt(ssk * (1.0 / QK_HEAD) + NORM_EPS)
        k_ref[rows, h * HEAD_PAD:h * HEAD_PAD + LANES] = (kn * invk * gkn_n).astype(BF16)
        k_ref[rows, h * HEAD_PAD + LANES:(h + 1) * HEAD_PAD] = (kr_rot * invk).astype(BF16)


def _mla_qkv_body(x, cs_ref, gmix_ref, win_ref, gqa_ref, wqb_ref, gkva_ref, wkb_ref, wvt_ref,
                  gqn_ref, gkn_ref, q_ref, k_ref, vt_ref):
    nsub = x.shape[0] // TQ
    rows = [slice(s * TQ, (s + 1) * TQ) for s in range(nsub)]
    hcats = [_mla_latent(x[r], gmix_ref, win_ref) for r in rows]
    mids = [_mla_expand(hc, gqa_ref, wqb_ref, gkva_ref, wkb_ref, wvt_ref) for hc in hcats]
    for s, (qf, kf, kr, vt) in enumerate(mids):
        vt_ref[s] = vt
    for r, (qf, kf, kr, vt) in zip(rows, mids):
        _mla_heads(qf, kf, kr, cs_ref[r, :], gqn_ref, gkn_ref, q_ref, k_ref, r)


def _gated_sum(x1_ref, gt_ref, y0_ref, y1_ref):
    gt = gt_ref[...]
    g0, g1 = gt[:, 0:1], gt[:, 1:2]
    y = jnp.concatenate([g0 * y0_ref[c] + g1 * y1_ref[c] for c in range(SC_PLANES)], axis=1)
    return x1_ref[...] + y


def _mla_qkv_kernel(x_ref, *rest):
    _mla_qkv_body(x_ref[...], *rest)


def _mla_qkv_combine_kernel(x1_ref, gt_ref, y0_ref, y1_ref, *rest):
    *mid, xo_ref, q_ref, k_ref, v_ref = rest
    x = _gated_sum(x1_ref, gt_ref, y0_ref, y1_ref)
    xo_ref[...] = x
    _mla_qkv_body(x, *mid, q_ref, k_ref, v_ref)


def _mla_qkv(x_parts, cs, gmix, w_in, g_qa, w_qb, g_kva, w_kb, w_vt, g_qn, g_kn):
    t, d = x_parts[0].shape
    tm = TM_QKV
    nsub = tm // TQ
    nt = t // tm
    row = lambda i: (i, 0)
    const = lambda i: (0, 0)
    h = MLA_HEADS
    weights = [gmix, w_in, g_qa, w_qb, g_kva, w_kb, w_vt, g_qn, g_kn]
    w_specs = [pl.BlockSpec(w.shape, const) for w in weights]
    qkv_specs = [pl.BlockSpec((tm, h * HEAD_PAD), row), pl.BlockSpec((tm, h * HEAD_PAD), row),
                 pl.BlockSpec((nsub, h * V_HEAD, TQ), lambda i: (i, 0, 0))]
    qkv_shapes = [jax.ShapeDtypeStruct((t, h * HEAD_PAD), BF16),
                  jax.ShapeDtypeStruct((t, h * HEAD_PAD), BF16),
                  jax.ShapeDtypeStruct((t // TQ, h * V_HEAD, TQ), BF16)]
    cs_spec = pl.BlockSpec((tm, 3 * LANES), row)
    if len(x_parts) == 1:
        q, k, v = pl.pallas_call(
            _mla_qkv_kernel, grid=(nt,),
            in_specs=[pl.BlockSpec((tm, d), row), cs_spec] + w_specs,
            out_specs=qkv_specs, out_shape=qkv_shapes,
            compiler_params=_cparams(("parallel",)), name="mla_qkv",
        )(x_parts[0], cs, *weights)
        return x_parts[0], q, k, v
    x1, gates_tok, yg = x_parts
    x, q, k, v = pl.pallas_call(
        _mla_qkv_combine_kernel, grid=(nt,),
        in_specs=[pl.BlockSpec((tm, d), row), pl.BlockSpec((tm, LANES), row),
                  pl.BlockSpec((SC_PLANES, tm, d // SC_PLANES), lambda i: (0, i, 0)),
                  pl.BlockSpec((SC_PLANES, tm, d // SC_PLANES), lambda i: (0, i + nt, 0)),
                  cs_spec] + w_specs,
        out_specs=[pl.BlockSpec((tm, d), row)] + qkv_specs,
        out_shape=[jax.ShapeDtypeStruct((t, d), F32)] + qkv_shapes,
        compiler_params=_cparams(("parallel",)), name="mla_qkv_combine",
    )(x1, gates_tok, yg, yg, cs, *weights)
    return x, q, k, v


def _attn_kernel(q_ref, k_ref, vt_ref, o_ref, s_ref, m_ref, l_ref, acc_ref):
    i = pl.program_id(1)
    m_ref[...] = jnp.full(m_ref.shape, -jnp.inf, F32)
    l_ref[...] = jnp.zeros(l_ref.shape, F32)
    acc_ref[...] = jnp.zeros(acc_ref.shape, F32)

    def scores(h, j, slot):
        off = pl.multiple_of(j * TQ, TQ)
        hs = slice(h * HEAD_PAD, (h + 1) * HEAD_PAD)
        s_ref[slot, h] = _dot_nt(k_ref[pl.ds(off, TQ), hs], q_ref[:, hs])

    def update(h, j, slot, masked):
        st = s_ref[slot, h]
        if masked:
            kc = lax.broadcasted_iota(jnp.int32, (TQ, TQ), 0) // CHUNK
            qc = lax.broadcasted_iota(jnp.int32, (TQ, TQ), 1) // CHUNK
            st = jnp.where(kc <= qc, st, -jnp.inf)
        m_old = m_ref[h]
        m_new = jnp.maximum(m_old, jnp.max(st, axis=0, keepdims=True))
        p = jnp.exp2(st - m_new)
        alpha = jnp.exp2(m_old - m_new)
        l_ref[h] = alpha * l_ref[h] + jnp.sum(p, axis=0, keepdims=True)
        pv = _dot(vt_ref[j, h * V_HEAD:(h + 1) * V_HEAD, :], p.astype(BF16))
        acc_ref[h] = alpha * acc_ref[h] + pv
        m_ref[h] = m_new

    def full_tile(j, carry):
        for h in range(MLA_HEADS):
            scores(h, j, 0)
        for h in range(MLA_HEADS):
            update(h, j, 0, False)
        return carry

    lax.fori_loop(0, i, full_tile, 0)
    for h in range(MLA_HEADS):
        scores(h, i, 0)
    for h in range(MLA_HEADS):
        update(h, i, 0, True)
        o_ref[:, h * V_HEAD:(h + 1) * V_HEAD] = (acc_ref[h] / l_ref[h]).T.astype(BF16)


def _attention(q, k, vt, batch, seq):
    h = MLA_HEADS
    nq = seq // TQ
    t = batch * seq
    return pl.pallas_call(
        _attn_kernel, grid=(batch, nq),
        in_specs=[pl.BlockSpec((TQ, h * HEAD_PAD), lambda b, i: (b * nq + i, 0)),
                  pl.BlockSpec((seq, h * HEAD_PAD), lambda b, i: (b, 0)),
                  pl.BlockSpec((nq, h * V_HEAD, TQ), lambda b, i: (b, 0, 0))],
        out_specs=pl.BlockSpec((TQ, h * V_HEAD), lambda b, i: (b * nq + i, 0)),
        out_shape=jax.ShapeDtypeStruct((t, h * V_HEAD), BF16),
        scratch_shapes=[pltpu.VMEM((2, h, TQ, TQ), F32),
                        pltpu.VMEM((h, 1, TQ), F32), pltpu.VMEM((h, 1, TQ), F32),
                        pltpu.VMEM((h, V_HEAD, TQ), F32)],
        compiler_params=_cparams(("parallel", "arbitrary")), name="attention",
    )(q, k, vt)


def _oproj_kernel(o_ref, x_ref, wo_ref, g_ref, x1_ref, xn_ref):
    x1 = x_ref[...] + _dot(o_ref[...], wo_ref[...])
    x1_ref[...] = x1
    xn_ref[...] = _rms(x1, g_ref[...]).astype(BF16)


def _oproj(o, x, w_o, g_ffn):
    t, d = x.shape
    tm = TM_OPROJ
    row = lambda i: (i, 0)
    const = lambda i: (0, 0)
    return pl.pallas_call(
        _oproj_kernel, grid=(t // tm,),
        in_specs=[pl.BlockSpec((tm, o.shape[1]), row), pl.BlockSpec((tm, d), row),
                  pl.BlockSpec(w_o.shape, const), pl.BlockSpec((1, d), const)],
        out_specs=[pl.BlockSpec((tm, d), row), pl.BlockSpec((tm, d), row)],
        out_shape=[jax.ShapeDtypeStruct((t, d), F32), jax.ShapeDtypeStruct((t, d), BF16)],
        compiler_params=_cparams(("parallel",)), name="attn_oproj",
    )(o, x, w_o, g_ffn)


def _swiglu_chunk(x, w1, w3, w2):
    a = _dot(x, w1)
    b = _dot(x, w3)
    hid = (a * jax.nn.sigmoid(a)) * b
    return _dot(hid.astype(BF16), w2)


def _ffn_kernel(xn_ref, x1_ref, w1_ref, w3_ref, w2_ref, o_ref):
    @pl.when(pl.program_id(1) == 0)
    def _():
        o_ref[...] = x1_ref[...]

    o_ref[...] += _swiglu_chunk(xn_ref[...], w1_ref[...].astype(BF16), w3_ref[...].astype(BF16),
                                w2_ref[...].astype(BF16))


def _dense_ffn(xn, x1, w1, w3, w2, layer):
    t, d = x1.shape
    ff = w1.shape[2]
    tm, tf = TM_FFN, TF_FFN
    return pl.pallas_call(
        _ffn_kernel, grid=(t // tm, ff // tf),
        in_specs=[pl.BlockSpec((tm, d), lambda i, f: (i, 0)),
                  pl.BlockSpec((tm, d), lambda i, f: (i, 0)),
                  pl.BlockSpec((None, d, tf), lambda i, f: (layer, 0, f)),
                  pl.BlockSpec((None, d, tf), lambda i, f: (layer, 0, f)),
                  pl.BlockSpec((None, tf, d), lambda i, f: (layer, f, 0))],
        out_specs=pl.BlockSpec((tm, d), lambda i, f: (i, 0)),
        out_shape=jax.ShapeDtypeStruct((t, d), F32),
        compiler_params=_cparams(("parallel", "arbitrary")), name="dense_ffn",
    )(xn, x1, w1, w3, w2)


def _lru_kernel(x_ref, gmix_ref, win_ref, bin_ref, cw_ref, cb_ref, wga_ref, bga_ref,
                wgi_ref, bgi_ref, lam_ref, wout_ref, bout_ref, gffn_ref,
                x1_ref, xn_ref,
                xpad_ref, a_ref, u_ref, gate_ref, yg_ref, h_ref):
    nb, tt, d = x_ref.shape
    w = lam_ref.shape[1]
    m = nb * tt

    @pl.when(pl.program_id(0) == 0)
    def _():
        h_ref[...] = jnp.zeros_like(h_ref)
        xpad_ref[:, 0:SUBLANES, :] = jnp.zeros((nb, SUBLANES, w), F32)

    x = x_ref[...].reshape(m, d)
    xn = _rms(x, gmix_ref[...]).astype(BF16)
    hcat = _dot(xn, win_ref[...]) + bin_ref[...]
    gate_ref[...] = jax.nn.gelu(hcat[:, :w], approximate=True)
    xpad_ref[:, SUBLANES:, :] = hcat[:, w:].reshape(nb, tt, w)

    xc = cb_ref[...].reshape(1, 1, w) + jnp.zeros((nb, tt, w), F32)
    for j in range(CONV_WIDTH):
        lo = SUBLANES - (CONV_WIDTH - 1) + j
        xc = xc + xpad_ref[:, lo:lo + tt, :] * cw_ref[j:j + 1, :].reshape(1, 1, w)
    xpad_ref[:, 0:SUBLANES, :] = xpad_ref[:, tt:tt + SUBLANES, :]
    xc = xc.reshape(m, w)

    lam = lam_ref[...]
    log_sig = jnp.minimum(lam, 0.0) - jnp.log1p(jnp.exp(-jnp.abs(lam)))
    for hh in range(LRU_HEADS):
        sl = slice(hh * LANES, (hh + 1) * LANES)
        xh = xc[:, sl]
        xhb = xh.astype(BF16)
        r = jax.nn.sigmoid(_dot(xhb, wga_ref[hh]) + bga_ref[hh:hh + 1, :])
        ig = jax.nn.sigmoid(_dot(xhb, wgi_ref[hh]) + bgi_ref[hh:hh + 1, :])
        log_a = LRU_C * r * log_sig[:, sl]
        a = jnp.exp(log_a)
        u = jnp.sqrt(1.0 - a * a) * (ig * xh)
        for b in range(nb):
            a_ref[hh, b * LRU_PITCH:b * LRU_PITCH + tt, :] = a[b * tt:(b + 1) * tt, :]
            u_ref[hh, b * LRU_PITCH:b * LRU_PITCH + tt, :] = u[b * tt:(b + 1) * tt, :]

    def scan_step(t, hs):
        new = []
        for hh in range(LRU_HEADS):
            rows = pl.ds(t, nb, stride=LRU_PITCH)
            hv = a_ref[hh, rows, :] * hs[hh] + u_ref[hh, rows, :]
            u_ref[hh, rows, :] = hv
            new.append(hv)
        return tuple(new)

    hs = lax.fori_loop(0, tt, scan_step, tuple(h_ref[hh] for hh in range(LRU_HEADS)), unroll=4)
    for hh in range(LRU_HEADS):
        h_ref[hh] = hs[hh]

    for hh in range(LRU_HEADS):
        sl = slice(hh * LANES, (hh + 1) * LANES)
        for b in range(nb):
            y = u_ref[hh, b * LRU_PITCH:b * LRU_PITCH + tt, :]
            yg_ref[b * tt:(b + 1) * tt, sl] = (y * gate_ref[b * tt:(b + 1) * tt, sl]).astype(BF16)

    x1 = x + _dot(yg_ref[...], wout_ref[...]) + bout_ref[...]
    x1_ref[...] = x1.reshape(nb, tt, d)
    xn_ref[...] = _rms(x1, gffn_ref[...]).reshape(nb, tt, d)


def _lru_mixer(x3, gmix, w_in, b_in, conv_w, conv_b, wga, bga, wgi, bgi, lam, w_out, b_out, g_ffn):
    nb, seq, d = x3.shape
    w = lam.shape[1]
    tt = T_LRU
    m = nb * tt
    consts = [gmix, w_in, b_in, conv_w, conv_b, wga, bga, wgi, bgi, lam, w_out, b_out, g_ffn]
    const_specs = [pl.BlockSpec(c.shape, (lambda i, n=c.ndim: (0,) * n)) for c in consts]
    blk = pl.BlockSpec((nb, tt, d), lambda i: (0, i, 0))
    return pl.pallas_call(
        _lru_kernel, grid=(seq // tt,),
        in_specs=[blk] + const_specs,
        out_specs=[blk, blk],
        out_shape=[jax.ShapeDtypeStruct((nb, seq, d), F32), jax.ShapeDtypeStruct((nb, seq, d), F32)],
        scratch_shapes=[pltpu.VMEM((nb, tt + SUBLANES, w), F32),
                        pltpu.VMEM((LRU_HEADS, nb * LRU_PITCH, LANES), F32),
                        pltpu.VMEM((LRU_HEADS, nb * LRU_PITCH, LANES), F32),
                        pltpu.VMEM((m, w), F32),
                        pltpu.VMEM((m, w), BF16),
                        pltpu.VMEM((LRU_HEADS, nb, LANES), F32)],
        compiler_params=_cparams(("arbitrary",)), name="rglru_mixer",
    )(x3, *consts)


def _router_kernel(xn_ref, wh_ref, wl_ref, br_ref, slot_ref, gate_ref, te_ref, tv_ref,
                   lg_ref, tri_ref, cnt_ref, run_ref, start_ref, *, tile):
    phase = pl.program_id(0)
    j = pl.program_id(1)
    ne = N_EXPERTS
    tr = xn_ref.shape[0]

    @pl.when((phase == 0) & (j == 0))
    def _():
        cnt_ref[...] = jnp.zeros_like(cnt_ref)
        tri_ref[...] = jnp.where(lax.broadcasted_iota(jnp.int32, (tr, tr), 0)
                                 < lax.broadcasted_iota(jnp.int32, (tr, tr), 1),
                                 1.0, 0.0).astype(BF16)

    @pl.when(phase == 0)
    def _():
        x = xn_ref[...]
        xh = x.astype(BF16)
        xl = (x - xh.astype(F32)).astype(BF16)
        wh = wh_ref[...]
        nat = _dot(xh, wh) + _dot(xl, wh) + _dot(xh, wl_ref[...])
        lg_ref[j] = nat.T[0:ne, :] + br_ref[...]

    logits = lg_ref[j]
    eidx = lax.broadcasted_iota(jnp.int32, (ne, tr), 0)
    m1 = jnp.max(logits, axis=0, keepdims=True)
    i1 = jnp.min(jnp.where(logits == m1, eidx, ne), axis=0, keepdims=True)
    oh1 = eidx == i1
    rest = jnp.where(oh1, -jnp.inf, logits)
    m2 = jnp.max(rest, axis=0, keepdims=True)
    i2 = jnp.min(jnp.where(rest == m2, eidx, ne), axis=0, keepdims=True)
    oh2 = eidx == i2
    oh = jnp.where(oh1 | oh2, 1.0, 0.0)
    tile_cnt = jnp.sum(oh, axis=1, keepdims=True)

    @pl.when(phase == 0)
    def _():
        cnt_ref[...] += tile_cnt

    @pl.when((phase == 1) & (j == 0))
    def _():
        cnt = cnt_ref[...]
        padded = jnp.ceil(cnt * (1.0 / tile)) * tile
        sub = lax.broadcasted_iota(jnp.int32, (ne, 1), 0)
        start = jnp.zeros((ne, 1), F32)
        for e in range(ne - 1):
            start = start + jnp.where(sub > e, padded[e:e + 1, :], 0.0)
        start_ref[...] = start
        run_ref[...] = jnp.zeros_like(run_ref)
        tile_start = lax.broadcasted_iota(jnp.int32, (ne, LANES), 1).astype(F32) * tile
        owner = jnp.sum(jnp.where(tile_start >= start + padded, 1, 0), axis=0, keepdims=True)
        owner = jnp.minimum(owner, ne - 1)
        esub = lax.broadcasted_iota(jnp.int32, (ne, LANES), 0)
        real = jnp.clip(cnt - (tile_start - start), 0.0, float(tile))
        te_ref[...] = owner
        tv_ref[...] = jnp.sum(jnp.where(esub == owner, real, 0.0), axis=0,
                              keepdims=True).astype(jnp.int32)

    @pl.when(phase == 1)
    def _():
        before = _dot(oh.astype(BF16), tri_ref[...])
        slot_e = start_ref[...] + run_ref[...] + before
        s1 = jnp.sum(jnp.where(oh1, slot_e, 0.0), axis=0, keepdims=True)
        s2 = jnp.sum(jnp.where(oh2, slot_e, 0.0), axis=0, keepdims=True)
        slot_ref[0:1, :] = s1.astype(jnp.int32)
        slot_ref[1:2, :] = s2.astype(jnp.int32)
        e21 = jnp.exp(m2 - m1)
        g1 = 1.0 / (1.0 + e21)
        g2 = e21 * g1
        rowi = lax.broadcasted_iota(jnp.int32, (LANES, tr), 0)
        gmat = jnp.where(rowi == 0, g1, jnp.where(rowi == 1, g2, 0.0))
        gate_ref[...] = gmat.T
        run_ref[...] += tile_cnt


def _router(xn, w_router, b_router, tile):
    t, d = xn.shape
    tr = TR
    nt = t // tr
    wp = jnp.pad(w_router, ((0, 0), (0, LANES - N_EXPERTS)))
    wh = wp.astype(BF16)
    wl = (wp - wh.astype(F32)).astype(BF16)
    const = lambda p, j: (0, 0)
    return pl.pallas_call(
        functools.partial(_router_kernel, tile=tile), grid=(2, nt),
        in_specs=[pl.BlockSpec((tr, d), lambda p, j: (j * (1 - p) + (nt - 1) * p, 0)),
                  pl.BlockSpec(wh.shape, const), pl.BlockSpec(wl.shape, const),
                  pl.BlockSpec((N_EXPERTS, 1), const)],
        out_specs=[pl.BlockSpec((2, tr), lambda p, j: (0, j * p)),
                   pl.BlockSpec((tr, LANES), lambda p, j: (j * p, 0)),
                   pl.BlockSpec((1, LANES), const), pl.BlockSpec((1, LANES), const)],
        out_shape=[jax.ShapeDtypeStruct((2, t), jnp.int32),
                   jax.ShapeDtypeStruct((t, LANES), F32),
                   jax.ShapeDtypeStruct((1, LANES), jnp.int32),
                   jax.ShapeDtypeStruct((1, LANES), jnp.int32)],
        scratch_shapes=[pltpu.VMEM((nt, N_EXPERTS, tr), F32), pltpu.VMEM((tr, tr), BF16),
                        pltpu.VMEM((N_EXPERTS, 1), F32), pltpu.VMEM((N_EXPERTS, 1), F32),
                        pltpu.VMEM((N_EXPERTS, 1), F32)],
        compiler_params=_cparams(("arbitrary", "arbitrary")), name="moe_router",
    )(xn, wh, wl, b_router.reshape(N_EXPERTS, 1))


def _sc_mesh():
    return plsc.VectorSubcoreMesh(core_axis_name="c", subcore_axis_name="s")


def _sc_dispatch(xn, slots, n_slots):
    t, d = xn.shape
    win = SC_WINDOW
    nwin = t // win
    dp = d // SC_PLANES
    idx = slots.reshape(1, 2 * t)

    @functools.partial(pl.kernel,
                       out_type=jax.ShapeDtypeStruct((SC_PLANES, n_slots, dp), xn.dtype),
                       mesh=_sc_mesh(), scratch_types=[], name="moe_dispatch")
    def run(x_hbm, i_hbm, o_hbm):
        for c in range(SC_PLANES):
            def body(x_vmem, i0_vmem, i1_vmem, c=c):
                pltpu.sync_copy(x_vmem, o_hbm.at[c].at[i0_vmem.at[0]])
                pltpu.sync_copy(x_vmem, o_hbm.at[c].at[i1_vmem.at[0]])

            pltpu.emit_pipeline(
                body, grid=(nwin,),
                in_specs=[pl.BlockSpec((win, dp), lambda i, c=c: (i, c)),
                          pl.BlockSpec((1, win), lambda i: (0, i)),
                          pl.BlockSpec((1, win), lambda i: (0, i + nwin))],
                out_specs=[],
                core_axis_name=("c", "s"),
                dimension_semantics=(pltpu.PARALLEL,),
            )(x_hbm, i_hbm, i_hbm)

    return run(xn, idx)


def _sc_combine(y, slots):
    n2 = slots.size
    dp = y.shape[2]
    win = SC_WINDOW
    idx = slots.reshape(1, n2)

    @functools.partial(pl.kernel, out_type=jax.ShapeDtypeStruct((SC_PLANES, n2, dp), y.dtype),
                       mesh=_sc_mesh(), scratch_types=[], name="moe_combine")
    def run(y_hbm, i_hbm, o_hbm):
        for c in range(SC_PLANES):
            def body(i_vmem, o_vmem, c=c):
                pltpu.sync_copy(y_hbm.at[c].at[i_vmem.at[0]], o_vmem)

            pltpu.emit_pipeline(
                body, grid=(n2 // win,),
                in_specs=[pl.BlockSpec((1, win), lambda i: (0, i))],
                out_specs=[pl.BlockSpec((win, dp), lambda i: (i, 0))],
                core_axis_name=("c", "s"),
                dimension_semantics=(pltpu.PARALLEL,),
            )(i_hbm, o_hbm.at[c])

    return run(y, idx)


def _moe_ffn_kernel(te_ref, tv_ref, xs_ref, w1_ref, w3_ref, w2_ref, y_ref,
                    xb_ref, wb1_ref, wb3_ref, wb2_ref):
    i = pl.program_id(0)
    f = pl.program_id(1)
    valid = tv_ref[i]
    tm, dp = xs_ref.shape[1:]
    ts = TSUB_MOE

    @pl.when(f == 0)
    def _():
        rows = lax.broadcasted_iota(jnp.int32, (tm, 1), 0)
        for c in range(SC_PLANES):
            xb_ref[:, c * dp:(c + 1) * dp] = jnp.where(rows < valid, xs_ref[c], 0.0).astype(BF16)
        y_ref[...] = jnp.zeros_like(y_ref)

    def rows_block(lo, n, w1, w3, w2):
        y = _swiglu_chunk(xb_ref[lo:lo + n, :], w1, w3, w2)
        for c in range(SC_PLANES):
            y_ref[c, lo:lo + n, :] += y[:, c * dp:(c + 1) * dp]

    @pl.when(valid == tm)
    def _():
        rows_block(0, tm, w1_ref[...].astype(BF16), w3_ref[...].astype(BF16),
                   w2_ref[...].astype(BF16))

    @pl.when((valid > 0) & (valid < tm))
    def _():
        wb1_ref[...] = w1_ref[...].astype(BF16)
        wb3_ref[...] = w3_ref[...].astype(BF16)
        wb2_ref[...] = w2_ref[...].astype(BF16)
        for s in range(tm // ts):
            @pl.when(s * ts < valid)
            def _():
                rows_block(s * ts, ts, wb1_ref[...], wb3_ref[...], wb2_ref[...])


def _moe_ffn(xs, tile_expert, tile_valid, w1, w3, w2, layer):
    planes, ns, dp = xs.shape
    d = planes * dp
    ff = w1.shape[3]
    tm, tf = TM_MOE, TF_MOE

    def chunk(i, f, tv):
        return jnp.where(tv[i] > 0, f, 0)

    grid_spec = pltpu.PrefetchScalarGridSpec(
        num_scalar_prefetch=2, grid=(ns // tm, ff // tf),
        in_specs=[pl.BlockSpec((planes, tm, dp), lambda i, f, te, tv: (0, i, 0)),
                  pl.BlockSpec((None, None, d, tf),
                               lambda i, f, te, tv: (layer, te[i], 0, chunk(i, f, tv))),
                  pl.BlockSpec((None, None, d, tf),
                               lambda i, f, te, tv: (layer, te[i], 0, chunk(i, f, tv))),
                  pl.BlockSpec((None, None, tf, d),
                               lambda i, f, te, tv: (layer, te[i], chunk(i, f, tv), 0))],
        out_specs=pl.BlockSpec((planes, tm, dp), lambda i, f, te, tv: (0, i, 0)),
        scratch_shapes=[pltpu.VMEM((tm, d), BF16), pltpu.VMEM((d, tf), BF16),
                        pltpu.VMEM((d, tf), BF16), pltpu.VMEM((tf, d), BF16)])
    return pl.pallas_call(
        _moe_ffn_kernel, grid_spec=grid_spec,
        out_shape=jax.ShapeDtypeStruct((planes, ns, dp), F32),
        compiler_params=_cparams(("arbitrary", "arbitrary")), name="moe_ffn",
    )(tile_expert, tile_valid, xs, w1, w3, w2)


def _combine_kernel(x1_ref, gt_ref, y0_ref, y1_ref, o_ref):
    o_ref[...] = _gated_sum(x1_ref, gt_ref, y0_ref, y1_ref)


def _combine(x1, gates_tok, yg):
    t, d = x1.shape
    tm = 512
    nt = t // tm
    row = lambda i: (i, 0)
    return pl.pallas_call(
        _combine_kernel, grid=(nt,),
        in_specs=[pl.BlockSpec((tm, d), row), pl.BlockSpec((tm, LANES), row),
                  pl.BlockSpec((SC_PLANES, tm, d // SC_PLANES), lambda i: (0, i, 0)),
                  pl.BlockSpec((SC_PLANES, tm, d // SC_PLANES), lambda i: (0, i + nt, 0))],
        out_specs=pl.BlockSpec((tm, d), row),
        out_shape=jax.ShapeDtypeStruct((t, d), F32),
        compiler_params=_cparams(("parallel",)), name="moe_combine_residual",
    )(x1, gates_tok, yg, yg)


def _mla_weights(w_in, w_qb, w_kvb, g_qn, g_kn):
    d = w_in.shape[0]
    h = MLA_HEADS
    lat = Q_LORA + KV_LORA
    win = jnp.zeros((d, lat + HEAD_PAD), F32)
    win = win.at[:, :lat].set(w_in[:, :lat])
    win = win.at[:, lat + LANES:lat + LANES + QK_ROPE].set(w_in[:, lat:])
    wq = w_qb.reshape(Q_LORA, h, QK_HEAD)
    wq = jnp.pad(wq, ((0, 0), (0, 0), (0, HEAD_PAD - QK_HEAD))).reshape(Q_LORA, h * HEAD_PAD)
    wkv = w_kvb.reshape(KV_LORA, h, QK_NOPE + V_HEAD)
    wk = wkv[:, :, :QK_NOPE].reshape(KV_LORA, h * QK_NOPE)
    wvt = wkv[:, :, QK_NOPE:].reshape(KV_LORA, h * V_HEAD).T
    pad = lambda g: jnp.pad(g, (0, HEAD_PAD - QK_HEAD)).reshape(1, HEAD_PAD)
    return (win.astype(BF16), wq.astype(BF16), wk.astype(BF16), wvt.astype(BF16),
            pad(g_qn), pad(g_kn))


def kernel(x, positions, norm_mix, norm_ffn, mla_w_in, mla_g_qa, mla_w_qb, mla_g_kva, mla_w_kvb, mla_g_qn, mla_g_kn, mla_w_o, lru_w_in, lru_b_in, lru_conv_w, lru_conv_b, lru_w_gate_a, lru_b_gate_a, lru_w_gate_i, lru_b_gate_i, lru_lambda, lru_w_out, lru_b_out, ffn_w1, ffn_w3, ffn_w2, moe_w_router, moe_b_router, moe_w1, moe_w3, moe_w2):
    batch, seq, d = x.shape
    t = batch * seq
    depth = norm_mix.shape[0]
    n_slots = 2 * t + N_EXPERTS * TM_MOE
    n_tiles = n_slots // TM_MOE
    row = lambda v: v.reshape(1, -1)

    cs = _rope_tables(positions)
    parts = (x.reshape(t, d),)
    for i in range(depth):
        j = i // 2
        if i % 2 == 0:
            w_in, w_qb, w_kb, w_vt, g_qn, g_kn = _mla_weights(
                mla_w_in[j], mla_w_qb[j], mla_w_kvb[j], mla_g_qn[j], mla_g_kn[j])
            xr, q, k, vt = _mla_qkv(parts, cs, row(norm_mix[i]), w_in, row(mla_g_qa[j]), w_qb,
                                    row(mla_g_kva[j]), w_kb, w_vt, g_qn, g_kn)
            o = _attention(q, k, vt, batch, seq)
            x1, xn = _oproj(o, xr, mla_w_o[j].astype(BF16), row(norm_ffn[i]))
            parts = (_dense_ffn(xn, x1, ffn_w1, ffn_w3, ffn_w2, j),)
        else:
            (xr,) = parts
            x1, xn = _lru_mixer(
                xr.reshape(batch, seq, d), row(norm_mix[i]), lru_w_in[j].astype(BF16),
                row(lru_b_in[j]), lru_conv_w[j], row(lru_conv_b[j]),
                lru_w_gate_a[j].astype(BF16), lru_b_gate_a[j], lru_w_gate_i[j].astype(BF16),
                lru_b_gate_i[j], row(lru_lambda[j]), lru_w_out[j].astype(BF16),
                row(lru_b_out[j]), row(norm_ffn[i]))
            x1 = x1.reshape(t, d)
            xn = xn.reshape(t, d)
            slots, gates_tok, te, tv = _router(xn, moe_w_router[j], moe_b_router[j], TM_MOE)
            xs = _sc_dispatch(xn, slots, n_slots)
            y = _moe_ffn(xs, te[0, :n_tiles], tv[0, :n_tiles], moe_w1, moe_w3, moe_w2, j)
            yg = _sc_combine(y, slots)
            parts = (x1, gates_tok, yg)
    if len(parts) == 3:
        out = _combine(*parts)
    else:
        out = parts[0]
    return out.reshape(batch, seq, d)
```

```python
import functools
import math

import numpy as np
import jax
import jax.numpy as jnp
from jax import lax
from jax.experimental import pallas as pl
from jax.experimental.pallas import tpu as pltpu
from jax.experimental.pallas import tpu_sc as plsc

F32 = jnp.float32
BF16 = jnp.bfloat16

NORM_EPS = 1e-6
CHUNK = 64
MLA_HEADS = 8
QK_NOPE = 128
QK_ROPE = 64
QK_HEAD = QK_NOPE + QK_ROPE
V_HEAD = 128
Q_LORA = 384
KV_LORA = 256
ROPE_BASE = 10000.0
LRU_HEADS = 8
CONV_WIDTH = 4
LRU_C = 8.0
N_EXPERTS = 8

LANES = 128
SUBLANES = 8
HEAD_PAD = 2 * LANES
VMEM_LIMIT = 56 * 1024 * 1024

TQ = 256
TM_QKV = 2 * TQ
TM_OPROJ = 512
TM_FFN = 1024
TF_FFN = 256
T_LRU = 64
LRU_PITCH = T_LRU + SUBLANES
TR = 512
TM_MOE = 1024
TSUB_MOE = 256
TF_MOE = 512
SC_WINDOW = 128
SC_PLANES = 4


def _cparams(sem):
    return pltpu.CompilerParams(dimension_semantics=sem, vmem_limit_bytes=VMEM_LIMIT)


def _rms(x, g):
    return x * lax.rsqrt(jnp.mean(x * x, axis=-1, keepdims=True) + NORM_EPS) * g


def _dot(a, b):
    return jnp.dot(a, b, preferred_element_type=F32)


def _dot_nt(a, b):
    return lax.dot_general(a, b, (((1,), (1,)), ((), ())), preferred_element_type=F32)


def _rope_kernel(pos_ref, inv_ref, cs_ref, cst_ref):
    ang = inv_ref[...] * pos_ref[...].astype(F32)
    half = QK_ROPE // 2
    ct = jnp.cos(ang)
    st = jnp.sin(ang)
    cst_ref[0:half, :] = ct[0:half, :]
    cst_ref[half:QK_ROPE, :] = st[0:half, :]
    c = ct.T
    s = st.T
    lane = lax.broadcasted_iota(jnp.int32, c.shape, 1)
    cs_ref[:, 0:LANES] = jnp.where(lane < QK_ROPE, c, 0.0)
    cs_ref[:, LANES:2 * LANES] = jnp.where(lane < half, -s, 0.0)
    cs_ref[:, 2 * LANES:3 * LANES] = jnp.where((lane >= half) & (lane < QK_ROPE), s, 0.0)


def _rope_tables(positions):
    t = positions.size
    tm = 512
    inv = 1.0 / (ROPE_BASE ** (np.arange(0, QK_ROPE, 2, dtype=np.float32) / QK_ROPE))
    inv_col = np.zeros((LANES, 1), np.float32)
    inv_col[:QK_ROPE // 2, 0] = inv
    inv_col[QK_ROPE // 2:QK_ROPE, 0] = inv
    return pl.pallas_call(
        _rope_kernel,
        grid=(t // tm,),
        in_specs=[pl.BlockSpec((1, tm), lambda i: (0, i)),
                  pl.BlockSpec((LANES, 1), lambda i: (0, 0))],
        out_specs=[pl.BlockSpec((tm, 3 * LANES), lambda i: (i, 0)),
                   pl.BlockSpec((QK_ROPE, tm), lambda i: (0, i))],
        out_shape=[jax.ShapeDtypeStruct((t, 3 * LANES), F32),
                   jax.ShapeDtypeStruct((QK_ROPE, t), F32)],
        compiler_params=_cparams(("parallel",)),
        name="rope_tables",
    )(positions.reshape(1, t), jnp.asarray(inv_col))


def _rope_rot(x, cs):
    c = cs[:, 0:LANES]
    sa = cs[:, LANES:2 * LANES]
    sb = cs[:, 2 * LANES:3 * LANES]
    return (x * c + pltpu.roll(x, LANES - QK_ROPE // 2, 1) * sa
            + pltpu.roll(x, QK_ROPE // 2, 1) * sb)


def _mla_latent(x, gmix_ref, win_ref):
    return _dot(_rms(x, gmix_ref[...]).astype(BF16), win_ref[...])


def _mla_expand(hcat, gqa_ref, wqb_ref, gkva_ref, wkb_ref, wvt_ref):
    cq = hcat[:, :Q_LORA]
    ckv = hcat[:, Q_LORA:Q_LORA + KV_LORA]
    kr = hcat[:, Q_LORA + KV_LORA + LANES:Q_LORA + KV_LORA + 2 * LANES]
    qft = _dot_nt(wqb_ref[...], _rms(cq, gqa_ref[...]).astype(BF16))
    ckvn = _rms(ckv, gkva_ref[...]).astype(BF16)
    kf = _dot(ckvn, wkb_ref[...])
    vt = _dot_nt(wvt_ref[...], ckvn).astype(BF16)
    return qft, kf, kr, vt


def _mla_q_heads(qft, cst, gqn_ref, qt_ref, s):
    half = QK_ROPE // 2
    c, sn = cst[0:half, :], cst[half:QK_ROPE, :]
    g = gqn_ref[...]
    zero = jnp.zeros((HEAD_PAD - QK_HEAD, qft.shape[1]), BF16)
    for h in range(MLA_HEADS):
        blk = qft[h * HEAD_PAD:h * HEAD_PAD + QK_HEAD, :]
        ss = jnp.sum(blk * blk, axis=0, keepdims=True)
        inv = lax.rsqrt(ss * (1.0 / QK_HEAD) + NORM_EPS)
        y = blk * g * inv
        x1, x2 = y[QK_NOPE:QK_NOPE + half, :], y[QK_NOPE + half:QK_HEAD, :]
        base = h * HEAD_PAD
        qt_ref[s, base:base + QK_NOPE, :] = y[0:QK_NOPE, :].astype(BF16)
        qt_ref[s, base + QK_NOPE:base + QK_NOPE + half, :] = (x1 * c - x2 * sn).astype(BF16)
        qt_ref[s, base + QK_NOPE + half:base + QK_HEAD, :] = (x1 * sn + x2 * c).astype(BF16)
        qt_ref[s, base + QK_HEAD:base + HEAD_PAD, :] = zero


def _mla_k_heads(kf, kr, cs, gkn_ref, k_ref, rows):
    gkn_n, gkn_r = gkn_ref[:, 0:LANES], gkn_ref[:, LANES:2 * LANES]
    kr_ss = jnp.sum(kr * kr, axis=-1, keepdims=True)
    kr_rot = _rope_rot(kr * gkn_r, cs)
    for h in range(MLA_HEADS):
        kn = kf[:, h * LANES:(h + 1) * LANES]
        ssk = jnp.sum(kn * kn, axis=-1, keepdims=True) + kr_ss
        invk = lax.rsqrt(ssk * (1.0 / QK_HEAD) + NORM_EPS)
        k_ref[rows, h * HEAD_PAD:h * HEAD_PAD + LANES] = (kn * invk * gkn_n).astype(BF16)
        k_ref[rows, h * HEAD_PAD + LANES:(h + 1) * HEAD_PAD] = (kr_rot * invk).astype(BF16)


def _mla_qkv_body(x, cs_ref, cst_ref, gmix_ref, win_ref, gqa_ref, wqb_ref, gkva_ref, wkb_ref,
                  wvt_ref, gqn_ref, gkn_ref, qt_ref, k_ref, vt_ref):
    nsub = x.shape[0] // TQ
    rows = [slice(s * TQ, (s + 1) * TQ) for s in range(nsub)]
    hcats = [_mla_latent(x[r], gmix_ref, win_ref) for r in rows]
    mids = [_mla_expand(hc, gqa_ref, wqb_ref, gkva_ref, wkb_ref, wvt_ref) for hc in hcats]
    for s, (qft, kf, kr, vt) in enumerate(mids):
        vt_ref[s] = vt
    for s, (r, (qft, kf, kr, vt)) in enumerate(zip(rows, mids)):
        _mla_q_heads(qft, cst_ref[:, r], gqn_ref, qt_ref, s)
        _mla_k_heads(kf, kr, cs_ref[r, :], gkn_ref, k_ref, r)


def _gated_sum(x1_ref, gt_ref, y0_ref, y1_ref):
    gt = gt_ref[...]
    g0, g1 = gt[:, 0:1], gt[:, 1:2]
    y = jnp.concatenate([g0 * y0_ref[c] + g1 * y1_ref[c] for c in range(SC_PLANES)], axis=1)
    return x1_ref[...] + y


def _mla_qkv_kernel(x_ref, *rest):
    _mla_qkv_body(x_ref[...], *rest)


def _mla_qkv_combine_kernel(x1_ref, gt_ref, y0_ref, y1_ref, *rest):
    *mid, xo_ref, q_ref, k_ref, v_ref = rest
    x = _gated_sum(x1_ref, gt_ref, y0_ref, y1_ref)
    xo_ref[...] = x
    _mla_qkv_body(x, *mid, q_ref, k_ref, v_ref)


def _mla_qkv(x_parts, cs, cst, gmix, w_in, g_qa, w_qbt, g_kva, w_kb, w_vt, g_qn, g_kn):
    t, d = x_parts[0].shape
    tm = TM_QKV
    nsub = tm // TQ
    nt = t // tm
    row = lambda i: (i, 0)
    const = lambda i: (0, 0)
    h = MLA_HEADS
    weights = [gmix, w_in, g_qa, w_qbt, g_kva, w_kb, w_vt, g_qn, g_kn]
    w_specs = [pl.BlockSpec(w.shape, const) for w in weights]
    qkv_specs = [pl.BlockSpec((nsub, h * HEAD_PAD, TQ), lambda i: (i, 0, 0)),
                 pl.BlockSpec((tm, h * HEAD_PAD), row),
                 pl.BlockSpec((nsub, h * V_HEAD, TQ), lambda i: (i, 0, 0))]
    qkv_shapes = [jax.ShapeDtypeStruct((t // TQ, h * HEAD_PAD, TQ), BF16),
                  jax.ShapeDtypeStruct((t, h * HEAD_PAD), BF16),
                  jax.ShapeDtypeStruct((t // TQ, h * V_HEAD, TQ), BF16)]
    cs_specs = [pl.BlockSpec((tm, 3 * LANES), row), pl.BlockSpec((QK_ROPE, tm), lambda i: (0, i))]
    if len(x_parts) == 1:
        q, k, v = pl.pallas_call(
            _mla_qkv_kernel, grid=(nt,),
            in_specs=[pl.BlockSpec((tm, d), row)] + cs_specs + w_specs,
            out_specs=qkv_specs, out_shape=qkv_shapes,
            compiler_params=_cparams(("parallel",)), name="mla_qkv",
        )(x_parts[0], cs, cst, *weights)
        return x_parts[0], q, k, v
    x1, gates_tok, yg = x_parts
    x, q, k, v = pl.pallas_call(
        _mla_qkv_combine_kernel, grid=(nt,),
        in_specs=[pl.BlockSpec((tm, d), row), pl.BlockSpec((tm, LANES), row),
                  pl.BlockSpec((SC_PLANES, tm, d // SC_PLANES), lambda i: (0, i, 0)),
                  pl.BlockSpec((SC_PLANES, tm, d // SC_PLANES), lambda i: (0, i + nt, 0)),
                  ] + cs_specs + w_specs,
        out_specs=[pl.BlockSpec((tm, d), row)] + qkv_specs,
        out_shape=[jax.ShapeDtypeStruct((t, d), F32)] + qkv_shapes,
        compiler_params=_cparams(("parallel",)), name="mla_qkv_combine",
    )(x1, gates_tok, yg, yg, cs, cst, *weights)
    return x, q, k, v


def _attn_kernel(q_ref, k_ref, vt_ref, o_ref, s_ref, m_ref, l_ref, acc_ref):
    i = pl.program_id(1)
    m_ref[...] = jnp.full(m_ref.shape, -jnp.inf, F32)
    l_ref[...] = jnp.zeros(l_ref.shape, F32)
    acc_ref[...] = jnp.zeros(acc_ref.shape, F32)

    def scores(h, j, slot):
        off = pl.multiple_of(j * TQ, TQ)
        hs = slice(h * HEAD_PAD, (h + 1) * HEAD_PAD)
        s_ref[slot, h] = _dot(k_ref[pl.ds(off, TQ), hs], q_ref[0, hs, :])

    def update(h, j, slot, masked):
        st = s_ref[slot, h]
        if masked:
            kc = lax.broadcasted_iota(jnp.int32, (TQ, TQ), 0) // CHUNK
            qc = lax.broadcasted_iota(jnp.int32, (TQ, TQ), 1) // CHUNK
            st = jnp.where(kc <= qc, st, -jnp.inf)
        m_old = m_ref[h]
        m_new = jnp.maximum(m_old, jnp.max(st, axis=0, keepdims=True))
        p = jnp.exp2(st - m_new)
        alpha = jnp.exp2(m_old - m_new)
        l_ref[h] = alpha * l_ref[h] + jnp.sum(p, axis=0, keepdims=True)
        pv = _dot(vt_ref[j, h * V_HEAD:(h + 1) * V_HEAD, :], p.astype(BF16))
        acc_ref[h] = alpha * acc_ref[h] + pv
        m_ref[h] = m_new

    def full_tile(j, carry):
        for h in range(MLA_HEADS):
            scores(h, j, 0)
        for h in range(MLA_HEADS):
            update(h, j, 0, False)
        return carry

    lax.fori_loop(0, i, full_tile, 0)
    for h in range(MLA_HEADS):
        scores(h, i, 0)
    for h in range(MLA_HEADS):
        update(h, i, 0, True)
        o_ref[:, h * V_HEAD:(h + 1) * V_HEAD] = (acc_ref[h] / l_ref[h]).T.astype(BF16)


def _attention(q, k, vt, batch, seq):
    h = MLA_HEADS
    nq = seq // TQ
    t = batch * seq
    return pl.pallas_call(
        _attn_kernel, grid=(batch, nq),
        in_specs=[pl.BlockSpec((1, h * HEAD_PAD, TQ), lambda b, i: (b * nq + i, 0, 0)),
                  pl.BlockSpec((seq, h * HEAD_PAD), lambda b, i: (b, 0)),
                  pl.BlockSpec((nq, h * V_HEAD, TQ), lambda b, i: (b, 0, 0))],
        out_specs=pl.BlockSpec((TQ, h * V_HEAD), lambda b, i: (b * nq + i, 0)),
        out_shape=jax.ShapeDtypeStruct((t, h * V_HEAD), BF16),
        scratch_shapes=[pltpu.VMEM((2, h, TQ, TQ), F32),
                        pltpu.VMEM((h, 1, TQ), F32), pltpu.VMEM((h, 1, TQ), F32),
                        pltpu.VMEM((h, V_HEAD, TQ), F32)],
        compiler_params=_cparams(("parallel", "arbitrary")), name="attention",
    )(q, k, vt)


def _oproj_kernel(o_ref, x_ref, wo_ref, g_ref, x1_ref, xn_ref):
    x1 = x_ref[...] + _dot(o_ref[...], wo_ref[...])
    x1_ref[...] = x1
    xn_ref[...] = _rms(x1, g_ref[...]).astype(BF16)


def _oproj(o, x, w_o, g_ffn):
    t, d = x.shape
    tm = TM_OPROJ
    row = lambda i: (i, 0)
    const = lambda i: (0, 0)
    return pl.pallas_call(
        _oproj_kernel, grid=(t // tm,),
        in_specs=[pl.BlockSpec((tm, o.shape[1]), row), pl.BlockSpec((tm, d), row),
                  pl.BlockSpec(w_o.shape, const), pl.BlockSpec((1, d), const)],
        out_specs=[pl.BlockSpec((tm, d), row), pl.BlockSpec((tm, d), row)],
        out_shape=[jax.ShapeDtypeStruct((t, d), F32), jax.ShapeDtypeStruct((t, d), BF16)],
        compiler_params=_cparams(("parallel",)), name="attn_oproj",
    )(o, x, w_o, g_ffn)


def _swiglu_chunk(x, w1, w3, w2):
    a = _dot(x, w1)
    b = _dot(x, w3)
    hid = (a * jax.nn.sigmoid(a)) * b
    return _dot(hid.astype(BF16), w2)


def _ffn_kernel(xn_ref, x1_ref, w1_ref, w3_ref, w2_ref, o_ref):
    @pl.when(pl.program_id(1) == 0)
    def _():
        o_ref[...] = x1_ref[...]

    o_ref[...] += _swiglu_chunk(xn_ref[...], w1_ref[...].astype(BF16), w3_ref[...].astype(BF16),
                                w2_ref[...].astype(BF16))


def _dense_ffn(xn, x1, w1, w3, w2, layer):
    t, d = x1.shape
    ff = w1.shape[2]
    tm, tf = TM_FFN, TF_FFN
    return pl.pallas_call(
        _ffn_kernel, grid=(t // tm, ff // tf),
        in_specs=[pl.BlockSpec((tm, d), lambda i, f: (i, 0)),
                  pl.BlockSpec((tm, d), lambda i, f: (i, 0)),
                  pl.BlockSpec((None, d, tf), lambda i, f: (layer, 0, f)),
                  pl.BlockSpec((None, d, tf), lambda i, f: (layer, 0, f)),
                  pl.BlockSpec((None, tf, d), lambda i, f: (layer, f, 0))],
        out_specs=pl.BlockSpec((tm, d), lambda i, f: (i, 0)),
        out_shape=jax.ShapeDtypeStruct((t, d), F32),
        compiler_params=_cparams(("parallel", "arbitrary")), name="dense_ffn",
    )(xn, x1, w1, w3, w2)


def _lru_kernel(x_ref, gmix_ref, win_ref, bin_ref, cw_ref, cb_ref, wga_ref, bga_ref,
                wgi_ref, bgi_ref, lam_ref, wout_ref, bout_ref, gffn_ref,
                x1_ref, xn_ref,
                xpad_ref, a_ref, u_ref, gate_ref, yg_ref, h_ref):
    nb, tt, d = x_ref.shape
    w = lam_ref.shape[1]
    m = nb * tt

    @pl.when(pl.program_id(0) == 0)
    def _():
        h_ref[...] = jnp.zeros_like(h_ref)
        xpad_ref[:, 0:SUBLANES, :] = jnp.zeros((nb, SUBLANES, w), F32)

    x = x_ref[...].reshape(m, d)
    xn = _rms(x, gmix_ref[...]).astype(BF16)
    hcat = _dot(xn, win_ref[...]) + bin_ref[...]
    gate_ref[...] = jax.nn.gelu(hcat[:, :w], approximate=True)
    xpad_ref[:, SUBLANES:, :] = hcat[:, w:].reshape(nb, tt, w)

    xc = cb_ref[...].reshape(1, 1, w) + jnp.zeros((nb, tt, w), F32)
    for j in range(CONV_WIDTH):
        lo = SUBLANES - (CONV_WIDTH - 1) + j
        xc = xc + xpad_ref[:, lo:lo + tt, :] * cw_ref[j:j + 1, :].reshape(1, 1, w)
    xpad_ref[:, 0:SUBLANES, :] = xpad_ref[:, tt:tt + SUBLANES, :]
    xc = xc.reshape(m, w)

    lam = lam_ref[...]
    log_sig = jnp.minimum(lam, 0.0) - jnp.log1p(jnp.exp(-jnp.abs(lam)))
    for hh in range(LRU_HEADS):
        sl = slice(hh * LANES, (hh + 1) * LANES)
        xh = xc[:, sl]
        xhb = xh.astype(BF16)
        r = jax.nn.sigmoid(_dot(xhb, wga_ref[hh]) + bga_ref[hh:hh + 1, :])
        ig = jax.nn.sigmoid(_dot(xhb, wgi_ref[hh]) + bgi_ref[hh:hh + 1, :])
        log_a = LRU_C * r * log_sig[:, sl]
        a = jnp.exp(log_a)
        z = 1.0 - a * a
        u = (z * lax.rsqrt(jnp.maximum(z, 1e-30))) * (ig * xh)
        for b in range(nb):
            a_ref[hh, b * LRU_PITCH:b * LRU_PITCH + tt, :] = a[b * tt:(b + 1) * tt, :]
            u_ref[hh, b * LRU_PITCH:b * LRU_PITCH + tt, :] = u[b * tt:(b + 1) * tt, :]

    def scan_step(t, hs):
        new = []
        for hh in range(LRU_HEADS):
            rows = pl.ds(t, nb, stride=LRU_PITCH)
            hv = a_ref[hh, rows, :] * hs[hh] + u_ref[hh, rows, :]
            u_ref[hh, rows, :] = hv
            new.append(hv)
        return tuple(new)

    hs = lax.fori_loop(0, tt, scan_step, tuple(h_ref[hh] for hh in range(LRU_HEADS)), unroll=4)
    for hh in range(LRU_HEADS):
        h_ref[hh] = hs[hh]

    for hh in range(LRU_HEADS):
        sl = slice(hh * LANES, (hh + 1) * LANES)
        for b in range(nb):
            y = u_ref[hh, b * LRU_PITCH:b * LRU_PITCH + tt, :]
            yg_ref[b * tt:(b + 1) * tt, sl] = (y * gate_ref[b * tt:(b + 1) * tt, sl]).astype(BF16)

    x1 = x + _dot(yg_ref[...], wout_ref[...]) + bout_ref[...]
    x1_ref[...] = x1.reshape(nb, tt, d)
    xn_ref[...] = _rms(x1, gffn_ref[...]).reshape(nb, tt, d)


def _lru_mixer(x3, gmix, w_in, b_in, conv_w, conv_b, wga, bga, wgi, bgi, lam, w_out, b_out, g_ffn):
    nb, seq, d = x3.shape
    w = lam.shape[1]
    tt = T_LRU
    m = nb * tt
    consts = [gmix, w_in, b_in, conv_w, conv_b, wga, bga, wgi, bgi, lam, w_out, b_out, g_ffn]
    const_specs = [pl.BlockSpec(c.shape, (lambda i, n=c.ndim: (0,) * n)) for c in consts]
    blk = pl.BlockSpec((nb, tt, d), lambda i: (0, i, 0))
    return pl.pallas_call(
        _lru_kernel, grid=(seq // tt,),
        in_specs=[blk] + const_specs,
        out_specs=[blk, blk],
        out_shape=[jax.ShapeDtypeStruct((nb, seq, d), F32), jax.ShapeDtypeStruct((nb, seq, d), F32)],
        scratch_shapes=[pltpu.VMEM((nb, tt + SUBLANES, w), F32),
                        pltpu.VMEM((LRU_HEADS, nb * LRU_PITCH, LANES), F32),
                        pltpu.VMEM((LRU_HEADS, nb * LRU_PITCH, LANES), F32),
                        pltpu.VMEM((m, w), F32),
                        pltpu.VMEM((m, w), BF16),
                        pltpu.VMEM((LRU_HEADS, nb, LANES), F32)],
        compiler_params=_cparams(("arbitrary",)), name="rglru_mixer",
    )(x3, *consts)


def _router_kernel(xn_ref, wh_ref, wl_ref, br_ref, slot_ref, gate_ref, te_ref, tv_ref,
                   lg_ref, tri_ref, cnt_ref, run_ref, start_ref, *, tile):
    phase = pl.program_id(0)
    j = pl.program_id(1)
    ne = N_EXPERTS
    tr = xn_ref.shape[0]

    @pl.when((phase == 0) & (j == 0))
    def _():
        cnt_ref[...] = jnp.zeros_like(cnt_ref)
        tri_ref[...] = jnp.where(lax.broadcasted_iota(jnp.int32, (tr, tr), 0)
                                 < lax.broadcasted_iota(jnp.int32, (tr, tr), 1),
                                 1.0, 0.0).astype(BF16)

    @pl.when(phase == 0)
    def _():
        x = xn_ref[...]
        xh = x.astype(BF16)
        xl = (x - xh.astype(F32)).astype(BF16)
        wh = wh_ref[...]
        nat = _dot(xh, wh) + _dot(xl, wh) + _dot(xh, wl_ref[...])
        lg_ref[j] = nat.T[0:ne, :] + br_ref[...]

    logits = lg_ref[j]
    eidx = lax.broadcasted_iota(jnp.int32, (ne, tr), 0)
    m1 = jnp.max(logits, axis=0, keepdims=True)
    i1 = jnp.min(jnp.where(logits == m1, eidx, ne), axis=0, keepdims=True)
    oh1 = eidx == i1
    rest = jnp.where(oh1, -jnp.inf, logits)
    m2 = jnp.max(rest, axis=0, keepdims=True)
    i2 = jnp.min(jnp.where(rest == m2, eidx, ne), axis=0, keepdims=True)
    oh2 = eidx == i2
    oh = jnp.where(oh1 | oh2, 1.0, 0.0)
    tile_cnt = jnp.sum(oh, axis=1, keepdims=True)

    @pl.when(phase == 0)
    def _():
        cnt_ref[...] += tile_cnt

    @pl.when((phase == 1) & (j == 0))
    def _():
        cnt = cnt_ref[...]
        padded = jnp.ceil(cnt * (1.0 / tile)) * tile
        sub = lax.broadcasted_iota(jnp.int32, (ne, 1), 0)
        start = jnp.zeros((ne, 1), F32)
        for e in range(ne - 1):
            start = start + jnp.where(sub > e, padded[e:e + 1, :], 0.0)
        start_ref[...] = start
        run_ref[...] = jnp.zeros_like(run_ref)
        tile_start = lax.broadcasted_iota(jnp.int32, (ne, LANES), 1).astype(F32) * tile
        owner = jnp.sum(jnp.where(tile_start >= start + padded, 1, 0), axis=0, keepdims=True)
        owner = jnp.minimum(owner, ne - 1)
        esub = lax.broadcasted_iota(jnp.int32, (ne, LANES), 0)
        real = jnp.clip(cnt - (tile_start - start), 0.0, float(tile))
        te_ref[...] = owner
        tv_ref[...] = jnp.sum(jnp.where(esub == owner, real, 0.0), axis=0,
                              keepdims=True).astype(jnp.int32)

    @pl.when(phase == 1)
    def _():
        before = _dot(oh.astype(BF16), tri_ref[...])
        slot_e = start_ref[...] + run_ref[...] + before
        s1 = jnp.sum(jnp.where(oh1, slot_e, 0.0), axis=0, keepdims=True)
        s2 = jnp.sum(jnp.where(oh2, slot_e, 0.0), axis=0, keepdims=True)
        slot_ref[0:1, :] = s1.astype(jnp.int32)
        slot_ref[1:2, :] = s2.astype(jnp.int32)
        e21 = jnp.exp(m2 - m1)
        g1 = 1.0 / (1.0 + e21)
        g2 = e21 * g1
        rowi = lax.broadcasted_iota(jnp.int32, (LANES, tr), 0)
        gmat = jnp.where(rowi == 0, g1, jnp.where(rowi == 1, g2, 0.0))
        gate_ref[...] = gmat.T
        run_ref[...] += tile_cnt


def _router(xn, w_router, b_router, tile):
    t, d = xn.shape
    tr = TR
    nt = t // tr
    wp = jnp.pad(w_router, ((0, 0), (0, LANES - N_EXPERTS)))
    wh = wp.astype(BF16)
    wl = (wp - wh.astype(F32)).astype(BF16)
    const = lambda p, j: (0, 0)
    return pl.pallas_call(
        functools.partial(_router_kernel, tile=tile), grid=(2, nt),
        in_specs=[pl.BlockSpec((tr, d), lambda p, j: (j * (1 - p) + (nt - 1) * p, 0)),
                  pl.BlockSpec(wh.shape, const), pl.BlockSpec(wl.shape, const),
                  pl.BlockSpec((N_EXPERTS, 1), const)],
        out_specs=[pl.BlockSpec((2, tr), lambda p, j: (0, j * p)),
                   pl.BlockSpec((tr, LANES), lambda p, j: (j * p, 0)),
                   pl.BlockSpec((1, LANES), const), pl.BlockSpec((1, LANES), const)],
        out_shape=[jax.ShapeDtypeStruct((2, t), jnp.int32),
                   jax.ShapeDtypeStruct((t, LANES), F32),
                   jax.ShapeDtypeStruct((1, LANES), jnp.int32),
                   jax.ShapeDtypeStruct((1, LANES), jnp.int32)],
        scratch_shapes=[pltpu.VMEM((nt, N_EXPERTS, tr), F32), pltpu.VMEM((tr, tr), BF16),
                        pltpu.VMEM((N_EXPERTS, 1), F32), pltpu.VMEM((N_EXPERTS, 1), F32),
                        pltpu.VMEM((N_EXPERTS, 1), F32)],
        compiler_params=_cparams(("arbitrary", "arbitrary")), name="moe_router",
    )(xn, wh, wl, b_router.reshape(N_EXPERTS, 1))


def _sc_mesh():
    return plsc.VectorSubcoreMesh(core_axis_name="c", subcore_axis_name="s")


def _sc_dispatch(xn, slots, n_slots):
    t, d = xn.shape
    win = SC_WINDOW
    nwin = t // win
    dp = d // SC_PLANES
    idx = slots.reshape(1, 2 * t)

    @functools.partial(pl.kernel,
                       out_type=jax.ShapeDtypeStruct((SC_PLANES, n_slots, dp), xn.dtype),
                       mesh=_sc_mesh(), scratch_types=[], name="moe_dispatch")
    def run(x_hbm, i_hbm, o_hbm):
        for c in range(SC_PLANES):
            def body(x_vmem, i0_vmem, i1_vmem, c=c):
                pltpu.sync_copy(x_vmem, o_hbm.at[c].at[i0_vmem.at[0]])
                pltpu.sync_copy(x_vmem, o_hbm.at[c].at[i1_vmem.at[0]])

            pltpu.emit_pipeline(
                body, grid=(nwin,),
                in_specs=[pl.BlockSpec((win, dp), lambda i, c=c: (i, c)),
                          pl.BlockSpec((1, win), lambda i: (0, i)),
                          pl.BlockSpec((1, win), lambda i: (0, i + nwin))],
                out_specs=[],
                core_axis_name=("c", "s"),
                dimension_semantics=(pltpu.PARALLEL,),
            )(x_hbm, i_hbm, i_hbm)

    return run(xn, idx)


def _sc_combine(y, slots):
    n2 = slots.size
    dp = y.shape[2]
    win = SC_WINDOW
    idx = slots.reshape(1, n2)

    @functools.partial(pl.kernel, out_type=jax.ShapeDtypeStruct((SC_PLANES, n2, dp), y.dtype),
                       mesh=_sc_mesh(), scratch_types=[], name="moe_combine")
    def run(y_hbm, i_hbm, o_hbm):
        for c in range(SC_PLANES):
            def body(i_vmem, o_vmem, c=c):
                pltpu.sync_copy(y_hbm.at[c].at[i_vmem.at[0]], o_vmem)

            pltpu.emit_pipeline(
                body, grid=(n2 // win,),
                in_specs=[pl.BlockSpec((1, win), lambda i: (0, i))],
                out_specs=[pl.BlockSpec((win, dp), lambda i: (i, 0))],
                core_axis_name=("c", "s"),
                dimension_semantics=(pltpu.PARALLEL,),
            )(i_hbm, o_hbm.at[c])

    return run(y, idx)


def _moe_ffn_kernel(te_ref, tv_ref, xs_ref, w1_ref, w3_ref, w2_ref, y_ref,
                    xb_ref, wb1_ref, wb3_ref, wb2_ref):
    i = pl.program_id(0)
    f = pl.program_id(1)
    valid = tv_ref[i]
    tm, dp = xs_ref.shape[1:]
    ts = TSUB_MOE

    @pl.when(f == 0)
    def _():
        rows = lax.broadcasted_iota(jnp.int32, (tm, 1), 0)
        for c in range(SC_PLANES):
            xb_ref[:, c * dp:(c + 1) * dp] = jnp.where(rows < valid, xs_ref[c], 0.0).astype(BF16)
        y_ref[...] = jnp.zeros_like(y_ref)

    def rows_block(lo, n, w1, w3, w2):
        y = _swiglu_chunk(xb_ref[lo:lo + n, :], w1, w3, w2)
        for c in range(SC_PLANES):
            y_ref[c, lo:lo + n, :] += y[:, c * dp:(c + 1) * dp]

    @pl.when(valid == tm)
    def _():
        rows_block(0, tm, w1_ref[...].astype(BF16), w3_ref[...].astype(BF16),
                   w2_ref[...].astype(BF16))

    @pl.when((valid > 0) & (valid < tm))
    def _():
        wb1_ref[...] = w1_ref[...].astype(BF16)
        wb3_ref[...] = w3_ref[...].astype(BF16)
        wb2_ref[...] = w2_ref[...].astype(BF16)
        for s in range(tm // ts):
            @pl.when(s * ts < valid)
            def _():
                rows_block(s * ts, ts, wb1_ref[...], wb3_ref[...], wb2_ref[...])


def _moe_ffn(xs, tile_expert, tile_valid, w1, w3, w2, layer):
    planes, ns, dp = xs.shape
    d = planes * dp
    ff = w1.shape[3]
    tm, tf = TM_MOE, TF_MOE

    def chunk(i, f, tv):
        return jnp.where(tv[i] > 0, f, 0)

    grid_spec = pltpu.PrefetchScalarGridSpec(
        num_scalar_prefetch=2, grid=(ns // tm, ff // tf),
        in_specs=[pl.BlockSpec((planes, tm, dp), lambda i, f, te, tv: (0, i, 0)),
                  pl.BlockSpec((None, None, d, tf),
                               lambda i, f, te, tv: (layer, te[i], 0, chunk(i, f, tv))),
                  pl.BlockSpec((None, None, d, tf),
                               lambda i, f, te, tv: (layer, te[i], 0, chunk(i, f, tv))),
                  pl.BlockSpec((None, None, tf, d),
                               lambda i, f, te, tv: (layer, te[i], chunk(i, f, tv), 0))],
        out_specs=pl.BlockSpec((planes, tm, dp), lambda i, f, te, tv: (0, i, 0)),
        scratch_shapes=[pltpu.VMEM((tm, d), BF16), pltpu.VMEM((d, tf), BF16),
                        pltpu.VMEM((d, tf), BF16), pltpu.VMEM((tf, d), BF16)])
    return pl.pallas_call(
        _moe_ffn_kernel, grid_spec=grid_spec,
        out_shape=jax.ShapeDtypeStruct((planes, ns, dp), F32),
        compiler_params=_cparams(("arbitrary", "arbitrary")), name="moe_ffn",
    )(tile_expert, tile_valid, xs, w1, w3, w2)


def _combine_kernel(x1_ref, gt_ref, y0_ref, y1_ref, o_ref):
    o_ref[...] = _gated_sum(x1_ref, gt_ref, y0_ref, y1_ref)


def _combine(x1, gates_tok, yg):
    t, d = x1.shape
    tm = 512
    nt = t // tm
    row = lambda i: (i, 0)
    return pl.pallas_call(
        _combine_kernel, grid=(nt,),
        in_specs=[pl.BlockSpec((tm, d), row), pl.BlockSpec((tm, LANES), row),
                  pl.BlockSpec((SC_PLANES, tm, d // SC_PLANES), lambda i: (0, i, 0)),
                  pl.BlockSpec((SC_PLANES, tm, d // SC_PLANES), lambda i: (0, i + nt, 0))],
        out_specs=pl.BlockSpec((tm, d), row),
        out_shape=jax.ShapeDtypeStruct((t, d), F32),
        compiler_params=_cparams(("parallel",)), name="moe_combine_residual",
    )(x1, gates_tok, yg, yg)


def _mla_weights(w_in, w_qb, w_kvb, g_qn, g_kn):
    d = w_in.shape[0]
    h = MLA_HEADS
    lat = Q_LORA + KV_LORA
    win = jnp.zeros((d, lat + HEAD_PAD), F32)
    win = win.at[:, :lat].set(w_in[:, :lat])
    win = win.at[:, lat + LANES:lat + LANES + QK_ROPE].set(w_in[:, lat:])
    wq = w_qb.reshape(Q_LORA, h, QK_HEAD)
    wq = jnp.pad(wq, ((0, 0), (0, 0), (0, HEAD_PAD - QK_HEAD))).reshape(Q_LORA, h * HEAD_PAD)
    wkv = w_kvb.reshape(KV_LORA, h, QK_NOPE + V_HEAD)
    wk = wkv[:, :, :QK_NOPE].reshape(KV_LORA, h * QK_NOPE)
    wvt = wkv[:, :, QK_NOPE:].reshape(KV_LORA, h * V_HEAD).T
    scale = math.log2(math.e) / math.sqrt(QK_HEAD)
    gq = jnp.broadcast_to((g_qn * scale).reshape(QK_HEAD, 1), (QK_HEAD, TQ))
    gk = jnp.pad(g_kn, (0, HEAD_PAD - QK_HEAD)).reshape(1, HEAD_PAD)
    return (win.astype(BF16), wq.T.astype(BF16), wk.astype(BF16), wvt.astype(BF16), gq, gk)


def kernel(x, positions, norm_mix, norm_ffn, mla_w_in, mla_g_qa, mla_w_qb, mla_g_kva, mla_w_kvb, mla_g_qn, mla_g_kn, mla_w_o, lru_w_in, lru_b_in, lru_conv_w, lru_conv_b, lru_w_gate_a, lru_b_gate_a, lru_w_gate_i, lru_b_gate_i, lru_lambda, lru_w_out, lru_b_out, ffn_w1, ffn_w3, ffn_w2, moe_w_router, moe_b_router, moe_w1, moe_w3, moe_w2):
    batch, seq, d = x.shape
    t = batch * seq
    depth = norm_mix.shape[0]
    n_slots = 2 * t + N_EXPERTS * TM_MOE
    n_tiles = n_slots // TM_MOE
    row = lambda v: v.reshape(1, -1)

    cs, cst = _rope_tables(positions)
    parts = (x.reshape(t, d),)
    for i in range(depth):
        j = i // 2
        if i % 2 == 0:
            w_in, w_qbt, w_kb, w_vt, g_qn, g_kn = _mla_weights(
                mla_w_in[j], mla_w_qb[j], mla_w_kvb[j], mla_g_qn[j], mla_g_kn[j])
            xr, q, k, vt = _mla_qkv(parts, cs, cst, row(norm_mix[i]), w_in, row(mla_g_qa[j]),
                                    w_qbt, row(mla_g_kva[j]), w_kb, w_vt, g_qn, g_kn)
            o = _attention(q, k, vt, batch, seq)
            x1, xn = _oproj(o, xr, mla_w_o[j].astype(BF16), row(norm_ffn[i]))
            parts = (_dense_ffn(xn, x1, ffn_w1, ffn_w3, ffn_w2, j),)
        else:
            (xr,) = parts
            x1, xn = _lru_mixer(
                xr.reshape(batch, seq, d), row(norm_mix[i]), lru_w_in[j].astype(BF16),
                row(lru_b_in[j]), lru_conv_w[j], row(lru_conv_b[j]),
                lru_w_gate_a[j].astype(BF16), lru_b_gate_a[j], lru_w_gate_i[j].astype(BF16),
                lru_b_gate_i[j], row(lru_lambda[j]), lru_w_out[j].astype(BF16),
                row(lru_b_out[j]), row(norm_ffn[i]))
            x1 = x1.reshape(t, d)
            xn = xn.reshape(t, d)
            slots, gates_tok, te, tv = _router(xn, moe_w_router[j], moe_b_router[j], TM_MOE)
            xs = _sc_dispatch(xn, slots, n_slots)
            y = _moe_ffn(xs, te[0, :n_tiles], tv[0, :n_tiles], moe_w1, moe_w3, moe_w2, j)
            yg = _sc_combine(y, slots)
            parts = (x1, gates_tok, yg)
    if len(parts) == 3:
        out = _combine(*parts)
    else:
        out = parts[0]
    return out.reshape(batch, seq, d)
```

```python
import functools
import math

import numpy as np
import jax
import jax.numpy as jnp
from jax import lax
from jax.experimental import pallas as pl
from jax.experimental.pallas import tpu as pltpu
from jax.experimental.pallas import tpu_sc as plsc

F32 = jnp.float32
BF16 = jnp.bfloat16

NORM_EPS = 1e-6
CHUNK = 64
MLA_HEADS = 8
QK_NOPE = 128
QK_ROPE = 64
QK_HEAD = QK_NOPE + QK_ROPE
V_HEAD = 128
Q_LORA = 384
KV_LORA = 256
ROPE_BASE = 10000.0
LRU_HEADS = 8
CONV_WIDTH = 4
LRU_C = 8.0
N_EXPERTS = 8

LANES = 128
SUBLANES = 8
HEAD_PAD = 2 * LANES
VMEM_LIMIT = 56 * 1024 * 1024

TQ = 256
TM_QKV = 2 * TQ
TM_FFN = 1024
TF_FFN = 256
T_LRU = 64
LRU_PITCH = T_LRU + SUBLANES
TR = 512
TM_MOE = 1024
TSUB_MOE = 256
TF_MOE = 512
SC_WINDOW = 128
SC_PLANES = 2


def _cparams(sem):
    return pltpu.CompilerParams(dimension_semantics=sem, vmem_limit_bytes=VMEM_LIMIT)


def _rms(x, g):
    return x * lax.rsqrt(jnp.mean(x * x, axis=-1, keepdims=True) + NORM_EPS) * g


def _dot(a, b):
    return jnp.dot(a, b, preferred_element_type=F32)


def _dot_nt(a, b):
    return lax.dot_general(a, b, (((1,), (1,)), ((), ())), preferred_element_type=F32)


def _pack_pairs(x):
    n = x.shape[1] // 2
    bits = lax.bitcast_convert_type(x.astype(BF16).astype(F32), jnp.int32)
    return bits[:, :n] | lax.shift_right_logical(bits[:, n:], 16)


def _unpack_pairs(word):
    hi = lax.bitcast_convert_type(word & jnp.int32(-65536), F32)
    lo = lax.bitcast_convert_type(lax.shift_left(word, 16), F32)
    return hi, lo


def _unpack_planes(ref):
    halves = [_unpack_pairs(ref[c]) for c in range(SC_PLANES)]
    return [h for h, _ in halves] + [l for _, l in halves]


def _rope_kernel(pos_ref, inv_ref, cs_ref, cst_ref):
    ang = inv_ref[...] * pos_ref[...].astype(F32)
    half = QK_ROPE // 2
    ct = jnp.cos(ang)
    st = jnp.sin(ang)
    cst_ref[0:half, :] = ct[0:half, :]
    cst_ref[half:QK_ROPE, :] = st[0:half, :]
    c = ct.T
    s = st.T
    lane = lax.broadcasted_iota(jnp.int32, c.shape, 1)
    cs_ref[:, 0:LANES] = jnp.where(lane < QK_ROPE, c, 0.0)
    cs_ref[:, LANES:2 * LANES] = jnp.where(lane < half, -s, 0.0)
    cs_ref[:, 2 * LANES:3 * LANES] = jnp.where((lane >= half) & (lane < QK_ROPE), s, 0.0)


def _rope_tables(positions):
    t = positions.size
    tm = 512
    inv = 1.0 / (ROPE_BASE ** (np.arange(0, QK_ROPE, 2, dtype=np.float32) / QK_ROPE))
    inv_col = np.zeros((LANES, 1), np.float32)
    inv_col[:QK_ROPE // 2, 0] = inv
    inv_col[QK_ROPE // 2:QK_ROPE, 0] = inv
    return pl.pallas_call(
        _rope_kernel,
        grid=(t // tm,),
        in_specs=[pl.BlockSpec((1, tm), lambda i: (0, i)),
                  pl.BlockSpec((LANES, 1), lambda i: (0, 0))],
        out_specs=[pl.BlockSpec((tm, 3 * LANES), lambda i: (i, 0)),
                   pl.BlockSpec((QK_ROPE, tm), lambda i: (0, i))],
        out_shape=[jax.ShapeDtypeStruct((t, 3 * LANES), F32),
                   jax.ShapeDtypeStruct((QK_ROPE, t), F32)],
        compiler_params=_cparams(("parallel",)),
        name="rope_tables",
    )(positions.reshape(1, t), jnp.asarray(inv_col))


def _rope_rot(x, cs):
    c = cs[:, 0:LANES]
    sa = cs[:, LANES:2 * LANES]
    sb = cs[:, 2 * LANES:3 * LANES]
    return (x * c + pltpu.roll(x, LANES - QK_ROPE // 2, 1) * sa
            + pltpu.roll(x, QK_ROPE // 2, 1) * sb)


def _mla_latent(x, gmix_ref, win_ref):
    return _dot(_rms(x, gmix_ref[...]).astype(BF16), win_ref[...])


def _mla_expand(hcat, gqa_ref, wqb_ref, gkva_ref, wkb_ref, wvt_ref):
    cq = hcat[:, :Q_LORA]
    ckv = hcat[:, Q_LORA:Q_LORA + KV_LORA]
    kr = hcat[:, Q_LORA + KV_LORA + LANES:Q_LORA + KV_LORA + 2 * LANES]
    qft = _dot_nt(wqb_ref[...], _rms(cq, gqa_ref[...]).astype(BF16))
    ckvn = _rms(ckv, gkva_ref[...]).astype(BF16)
    kf = _dot(ckvn, wkb_ref[...])
    vt = _dot_nt(wvt_ref[...], ckvn).astype(BF16)
    return qft, kf, kr, vt


def _mla_q_heads(qft, cst, gqn_ref, qt_ref, s):
    half = QK_ROPE // 2
    c, sn = cst[0:half, :], cst[half:QK_ROPE, :]
    g = gqn_ref[...]
    zero = jnp.zeros((HEAD_PAD - QK_HEAD, qft.shape[1]), BF16)
    for h in range(MLA_HEADS):
        blk = qft[h * HEAD_PAD:h * HEAD_PAD + QK_HEAD, :]
        ss = jnp.sum(blk * blk, axis=0, keepdims=True)
        inv = lax.rsqrt(ss * (1.0 / QK_HEAD) + NORM_EPS)
        y = blk * g * inv
        x1, x2 = y[QK_NOPE:QK_NOPE + half, :], y[QK_NOPE + half:QK_HEAD, :]
        base = h * HEAD_PAD
        qt_ref[s, base:base + QK_NOPE, :] = y[0:QK_NOPE, :].astype(BF16)
        qt_ref[s, base + QK_NOPE:base + QK_NOPE + half, :] = (x1 * c - x2 * sn).astype(BF16)
        qt_ref[s, base + QK_NOPE + half:base + QK_HEAD, :] = (x1 * sn + x2 * c).astype(BF16)
        qt_ref[s, base + QK_HEAD:base + HEAD_PAD, :] = zero


def _mla_k_heads(kf, kr, cs, gkn_ref, k_ref, rows):
    gkn_n, gkn_r = gkn_ref[:, 0:LANES], gkn_ref[:, LANES:2 * LANES]
    kr_ss = jnp.sum(kr * kr, axis=-1, keepdims=True)
    kr_rot = _rope_rot(kr * gkn_r, cs)
    for h in range(MLA_HEADS):
        kn = kf[:, h * LANES:(h + 1) * LANES]
        ssk = jnp.sum(kn * kn, axis=-1, keepdims=True) + kr_ss
        invk = lax.rsqrt(ssk * (1.0 / QK_HEAD) + NORM_EPS)
        k_ref[rows, h * HEAD_PAD:h * HEAD_PAD + LANES] = (kn * invk * gkn_n).astype(BF16)
        k_ref[rows, h * HEAD_PAD + LANES:(h + 1) * HEAD_PAD] = (kr_rot * invk).astype(BF16)


def _mla_qkv_body(x, cs_ref, cst_ref, gmix_ref, win_ref, gqa_ref, wqb_ref, gkva_ref, wkb_ref,
                  wvt_ref, gqn_ref, gkn_ref, qt_ref, k_ref, vt_ref):
    nsub = x.shape[0] // TQ
    rows = [slice(s * TQ, (s + 1) * TQ) for s in range(nsub)]
    hcats = [_mla_latent(x[r], gmix_ref, win_ref) for r in rows]
    mids = [_mla_expand(hc, gqa_ref, wqb_ref, gkva_ref, wkb_ref, wvt_ref) for hc in hcats]
    for s, (qft, kf, kr, vt) in enumerate(mids):
        vt_ref[s] = vt
    for s, (r, (qft, kf, kr, vt)) in enumerate(zip(rows, mids)):
        _mla_q_heads(qft, cst_ref[:, r], gqn_ref, qt_ref, s)
        _mla_k_heads(kf, kr, cs_ref[r, :], gkn_ref, k_ref, r)


def _gated_sum(x1_ref, gt_ref, y0_ref, y1_ref):
    gt = gt_ref[...]
    g0, g1 = gt[:, 0:1], gt[:, 1:2]
    y = jnp.concatenate([g0 * a + g1 * b
                         for a, b in zip(_unpack_planes(y0_ref), _unpack_planes(y1_ref))], axis=1)
    return x1_ref[...] + y


def _mla_qkv_kernel(x_ref, *rest):
    _mla_qkv_body(x_ref[...], *rest)


def _mla_qkv_combine_kernel(x1_ref, gt_ref, y0_ref, y1_ref, *rest):
    *mid, xo_ref, q_ref, k_ref, v_ref = rest
    x = _gated_sum(x1_ref, gt_ref, y0_ref, y1_ref)
    xo_ref[...] = x
    _mla_qkv_body(x, *mid, q_ref, k_ref, v_ref)


def _mla_qkv(x_parts, cs, cst, gmix, w_in, g_qa, w_qbt, g_kva, w_kb, w_vt, g_qn, g_kn):
    t, d = x_parts[0].shape
    tm = TM_QKV
    nsub = tm // TQ
    nt = t // tm
    row = lambda i: (i, 0)
    const = lambda i: (0, 0)
    h = MLA_HEADS
    weights = [gmix, w_in, g_qa, w_qbt, g_kva, w_kb, w_vt, g_qn, g_kn]
    w_specs = [pl.BlockSpec(w.shape, const) for w in weights]
    qkv_specs = [pl.BlockSpec((nsub, h * HEAD_PAD, TQ), lambda i: (i, 0, 0)),
                 pl.BlockSpec((tm, h * HEAD_PAD), row),
                 pl.BlockSpec((nsub, h * V_HEAD, TQ), lambda i: (i, 0, 0))]
    qkv_shapes = [jax.ShapeDtypeStruct((t // TQ, h * HEAD_PAD, TQ), BF16),
                  jax.ShapeDtypeStruct((t, h * HEAD_PAD), BF16),
                  jax.ShapeDtypeStruct((t // TQ, h * V_HEAD, TQ), BF16)]
    cs_specs = [pl.BlockSpec((tm, 3 * LANES), row), pl.BlockSpec((QK_ROPE, tm), lambda i: (0, i))]
    if len(x_parts) == 1:
        q, k, v = pl.pallas_call(
            _mla_qkv_kernel, grid=(nt,),
            in_specs=[pl.BlockSpec((tm, d), row)] + cs_specs + w_specs,
            out_specs=qkv_specs, out_shape=qkv_shapes,
            compiler_params=_cparams(("parallel",)), name="mla_qkv",
        )(x_parts[0], cs, cst, *weights)
        return x_parts[0], q, k, v
    x1, gates_tok, yg = x_parts
    x, q, k, v = pl.pallas_call(
        _mla_qkv_combine_kernel, grid=(nt,),
        in_specs=[pl.BlockSpec((tm, d), row), pl.BlockSpec((tm, LANES), row),
                  pl.BlockSpec((SC_PLANES, tm, d // (2 * SC_PLANES)), lambda i: (0, i, 0)),
                  pl.BlockSpec((SC_PLANES, tm, d // (2 * SC_PLANES)), lambda i: (0, i + nt, 0)),
                  ] + cs_specs + w_specs,
        out_specs=[pl.BlockSpec((tm, d), row)] + qkv_specs,
        out_shape=[jax.ShapeDtypeStruct((t, d), F32)] + qkv_shapes,
        compiler_params=_cparams(("parallel",)), name="mla_qkv_combine",
    )(x1, gates_tok, yg, yg, cs, cst, *weights)
    return x, q, k, v


def _attn_kernel(q_ref, k_ref, vt_ref, x_ref, wo_ref, g_ref, x1_ref, xn_ref,
                 s_ref, m_ref, l_ref, acc_ref, o_ref):
    i = pl.program_id(1)
    m_ref[...] = jnp.full(m_ref.shape, -jnp.inf, F32)
    l_ref[...] = jnp.zeros(l_ref.shape, F32)
    acc_ref[...] = jnp.zeros(acc_ref.shape, F32)

    def scores(h, j):
        off = pl.multiple_of(j * TQ, TQ)
        hs = slice(h * HEAD_PAD, (h + 1) * HEAD_PAD)
        s_ref[h] = _dot(k_ref[pl.ds(off, TQ), hs], q_ref[0, hs, :])

    def update(h, j, masked):
        st = s_ref[h]
        if masked:
            kc = lax.broadcasted_iota(jnp.int32, (TQ, TQ), 0) // CHUNK
            qc = lax.broadcasted_iota(jnp.int32, (TQ, TQ), 1) // CHUNK
            st = jnp.where(kc <= qc, st, -jnp.inf)
        m_old = m_ref[h]
        m_new = jnp.maximum(m_old, jnp.max(st, axis=0, keepdims=True))
        p = jnp.exp2(st - m_new)
        alpha = jnp.exp2(m_old - m_new)
        l_ref[h] = alpha * l_ref[h] + jnp.sum(p, axis=0, keepdims=True)
        pv = _dot(vt_ref[j, h * V_HEAD:(h + 1) * V_HEAD, :], p.astype(BF16))
        acc_ref[h] = alpha * acc_ref[h] + pv
        m_ref[h] = m_new

    def full_tile(j, carry):
        for h in range(MLA_HEADS):
            scores(h, j)
        for h in range(MLA_HEADS):
            update(h, j, False)
        return carry

    lax.fori_loop(0, i, full_tile, 0)
    for h in range(MLA_HEADS):
        scores(h, i)
    for h in range(MLA_HEADS):
        update(h, i, True)
        o_ref[:, h * V_HEAD:(h + 1) * V_HEAD] = (acc_ref[h] / l_ref[h]).T.astype(BF16)

    x1 = x_ref[...] + _dot(o_ref[...], wo_ref[...])
    x1_ref[...] = x1
    xn_ref[...] = _rms(x1, g_ref[...]).astype(BF16)


def _attention(q, k, vt, x, w_o, g_ffn, batch, seq):
    h = MLA_HEADS
    nq = seq // TQ
    t, d = x.shape
    tile = lambda b, i: (b * nq + i, 0)
    const = lambda b, i: (0, 0)
    return pl.pallas_call(
        _attn_kernel, grid=(batch, nq),
        in_specs=[pl.BlockSpec((1, h * HEAD_PAD, TQ), lambda b, i: (b * nq + i, 0, 0)),
                  pl.BlockSpec((seq, h * HEAD_PAD), lambda b, i: (b, 0)),
                  pl.BlockSpec((nq, h * V_HEAD, TQ), lambda b, i: (b, 0, 0)),
                  pl.BlockSpec((TQ, d), tile), pl.BlockSpec(w_o.shape, const),
                  pl.BlockSpec((1, d), const)],
        out_specs=[pl.BlockSpec((TQ, d), tile), pl.BlockSpec((TQ, d), tile)],
        out_shape=[jax.ShapeDtypeStruct((t, d), F32), jax.ShapeDtypeStruct((t, d), BF16)],
        scratch_shapes=[pltpu.VMEM((h, TQ, TQ), F32),
                        pltpu.VMEM((h, 1, TQ), F32), pltpu.VMEM((h, 1, TQ), F32),
                        pltpu.VMEM((h, V_HEAD, TQ), F32),
                        pltpu.VMEM((TQ, h * V_HEAD), BF16)],
        compiler_params=_cparams(("parallel", "arbitrary")), name="attention",
    )(q, k, vt, x, w_o, g_ffn)


def _swiglu_chunk(x, w1, w3, w2):
    a = _dot(x, w1)
    b = _dot(x, w3)
    hid = (a * jax.nn.sigmoid(a)) * b
    return _dot(hid.astype(BF16), w2)


def _ffn_kernel(xn_ref, x1_ref, w1_ref, w3_ref, w2_ref, o_ref):
    @pl.when(pl.program_id(1) == 0)
    def _():
        o_ref[...] = x1_ref[...]

    o_ref[...] += _swiglu_chunk(xn_ref[...], w1_ref[...].astype(BF16), w3_ref[...].astype(BF16),
                                w2_ref[...].astype(BF16))


def _dense_ffn(xn, x1, w1, w3, w2, layer):
    t, d = x1.shape
    ff = w1.shape[2]
    tm, tf = TM_FFN, TF_FFN
    return pl.pallas_call(
        _ffn_kernel, grid=(t // tm, ff // tf),
        in_specs=[pl.BlockSpec((tm, d), lambda i, f: (i, 0)),
                  pl.BlockSpec((tm, d), lambda i, f: (i, 0)),
                  pl.BlockSpec((None, d, tf), lambda i, f: (layer, 0, f)),
                  pl.BlockSpec((None, d, tf), lambda i, f: (layer, 0, f)),
                  pl.BlockSpec((None, tf, d), lambda i, f: (layer, f, 0))],
        out_specs=pl.BlockSpec((tm, d), lambda i, f: (i, 0)),
        out_shape=jax.ShapeDtypeStruct((t, d), F32),
        compiler_params=_cparams(("parallel", "arbitrary")), name="dense_ffn",
    )(xn, x1, w1, w3, w2)


def _lru_kernel(x_ref, gmix_ref, win_ref, bin_ref, cw_ref, cb_ref, wga_ref, bga_ref,
                wgi_ref, bgi_ref, lam_ref, wout_ref, bout_ref, gffn_ref,
                x1_ref, xn_ref,
                xpad_ref, a_ref, u_ref, gate_ref, yg_ref, h_ref):
    nb, tt, d = x_ref.shape
    w = lam_ref.shape[1]
    m = nb * tt

    @pl.when(pl.program_id(0) == 0)
    def _():
        h_ref[...] = jnp.zeros_like(h_ref)
        xpad_ref[:, 0:SUBLANES, :] = jnp.zeros((nb, SUBLANES, w), F32)

    x = x_ref[...].reshape(m, d)
    xn = _rms(x, gmix_ref[...]).astype(BF16)
    hcat = _dot(xn, win_ref[...]) + bin_ref[...]
    gate_ref[...] = jax.nn.gelu(hcat[:, :w], approximate=True)
    xpad_ref[:, SUBLANES:, :] = hcat[:, w:].reshape(nb, tt, w)

    xc = cb_ref[...].reshape(1, 1, w) + jnp.zeros((nb, tt, w), F32)
    for j in range(CONV_WIDTH):
        lo = SUBLANES - (CONV_WIDTH - 1) + j
        xc = xc + xpad_ref[:, lo:lo + tt, :] * cw_ref[j:j + 1, :].reshape(1, 1, w)
    xpad_ref[:, 0:SUBLANES, :] = xpad_ref[:, tt:tt + SUBLANES, :]
    xc = xc.reshape(m, w)

    lam = lam_ref[...]
    log_sig = jnp.minimum(lam, 0.0) - jnp.log1p(jnp.exp(-jnp.abs(lam)))
    for hh in range(LRU_HEADS):
        sl = slice(hh * LANES, (hh + 1) * LANES)
        xh = xc[:, sl]
        xhb = xh.astype(BF16)
        r = jax.nn.sigmoid(_dot(xhb, wga_ref[hh]) + bga_ref[hh:hh + 1, :])
        ig = jax.nn.sigmoid(_dot(xhb, wgi_ref[hh]) + bgi_ref[hh:hh + 1, :])
        log_a = LRU_C * r * log_sig[:, sl]
        a = jnp.exp(log_a)
        z = 1.0 - a * a
        u = (z * lax.rsqrt(jnp.maximum(z, 1e-30))) * (ig * xh)
        for b in range(nb):
            a_ref[hh, b * LRU_PITCH:b * LRU_PITCH + tt, :] = a[b * tt:(b + 1) * tt, :]
            u_ref[hh, b * LRU_PITCH:b * LRU_PITCH + tt, :] = u[b * tt:(b + 1) * tt, :]

    def scan_step(t, hs):
        new = []
        for hh in range(LRU_HEADS):
            rows = pl.ds(t, nb, stride=LRU_PITCH)
            hv = a_ref[hh, rows, :] * hs[hh] + u_ref[hh, rows, :]
            u_ref[hh, rows, :] = hv
            new.append(hv)
        return tuple(new)

    hs = lax.fori_loop(0, tt, scan_step, tuple(h_ref[hh] for hh in range(LRU_HEADS)), unroll=4)
    for hh in range(LRU_HEADS):
        h_ref[hh] = hs[hh]

    for hh in range(LRU_HEADS):
        sl = slice(hh * LANES, (hh + 1) * LANES)
        for b in range(nb):
            y = u_ref[hh, b * LRU_PITCH:b * LRU_PITCH + tt, :]
            yg_ref[b * tt:(b + 1) * tt, sl] = (y * gate_ref[b * tt:(b + 1) * tt, sl]).astype(BF16)

    x1 = x + _dot(yg_ref[...], wout_ref[...]) + bout_ref[...]
    x1_ref[...] = x1.reshape(nb, tt, d)
    xn_ref[...] = _pack_pairs(_rms(x1, gffn_ref[...])).reshape(nb, tt, d // 2)


def _lru_mixer(x3, gmix, w_in, b_in, conv_w, conv_b, wga, bga, wgi, bgi, lam, w_out, b_out, g_ffn):
    nb, seq, d = x3.shape
    w = lam.shape[1]
    tt = T_LRU
    m = nb * tt
    consts = [gmix, w_in, b_in, conv_w, conv_b, wga, bga, wgi, bgi, lam, w_out, b_out, g_ffn]
    const_specs = [pl.BlockSpec(c.shape, (lambda i, n=c.ndim: (0,) * n)) for c in consts]
    blk = pl.BlockSpec((nb, tt, d), lambda i: (0, i, 0))
    return pl.pallas_call(
        _lru_kernel, grid=(seq // tt,),
        in_specs=[blk] + const_specs,
        out_specs=[blk, pl.BlockSpec((nb, tt, d // 2), lambda i: (0, i, 0))],
        out_shape=[jax.ShapeDtypeStruct((nb, seq, d), F32),
                   jax.ShapeDtypeStruct((nb, seq, d // 2), jnp.int32)],
        scratch_shapes=[pltpu.VMEM((nb, tt + SUBLANES, w), F32),
                        pltpu.VMEM((LRU_HEADS, nb * LRU_PITCH, LANES), F32),
                        pltpu.VMEM((LRU_HEADS, nb * LRU_PITCH, LANES), F32),
                        pltpu.VMEM((m, w), F32),
                        pltpu.VMEM((m, w), BF16),
                        pltpu.VMEM((LRU_HEADS, nb, LANES), F32)],
        compiler_params=_cparams(("arbitrary",)), name="rglru_mixer",
    )(x3, *consts)


def _router_kernel(x1_ref, g_ref, wh_ref, wl_ref, br_ref, slot_ref, gate_ref, te_ref, tv_ref,
                   lg_ref, tri_ref, cnt_ref, run_ref, start_ref, *, tile):
    phase = pl.program_id(0)
    j = pl.program_id(1)
    ne = N_EXPERTS
    tr = x1_ref.shape[0]

    @pl.when((phase == 0) & (j == 0))
    def _():
        cnt_ref[...] = jnp.zeros_like(cnt_ref)
        tri_ref[...] = jnp.where(lax.broadcasted_iota(jnp.int32, (tr, tr), 0)
                                 < lax.broadcasted_iota(jnp.int32, (tr, tr), 1),
                                 1.0, 0.0).astype(BF16)

    @pl.when(phase == 0)
    def _():
        x = _rms(x1_ref[...], g_ref[...])
        xh = x.astype(BF16)
        xl = (x - xh.astype(F32)).astype(BF16)
        wh = wh_ref[...]
        nat = _dot(xh, wh) + _dot(xl, wh) + _dot(xh, wl_ref[...])
        lg_ref[j] = nat.T[0:ne, :] + br_ref[...]

    logits = lg_ref[j]
    eidx = lax.broadcasted_iota(jnp.int32, (ne, tr), 0)
    m1 = jnp.max(logits, axis=0, keepdims=True)
    i1 = jnp.min(jnp.where(logits == m1, eidx, ne), axis=0, keepdims=True)
    oh1 = eidx == i1
    rest = jnp.where(oh1, -jnp.inf, logits)
    m2 = jnp.max(rest, axis=0, keepdims=True)
    i2 = jnp.min(jnp.where(rest == m2, eidx, ne), axis=0, keepdims=True)
    oh2 = eidx == i2
    oh = jnp.where(oh1 | oh2, 1.0, 0.0)
    tile_cnt = jnp.sum(oh, axis=1, keepdims=True)

    @pl.when(phase == 0)
    def _():
        cnt_ref[...] += tile_cnt

    @pl.when((phase == 1) & (j == 0))
    def _():
        cnt = cnt_ref[...]
        padded = jnp.ceil(cnt * (1.0 / tile)) * tile
        sub = lax.broadcasted_iota(jnp.int32, (ne, 1), 0)
        start = jnp.zeros((ne, 1), F32)
        for e in range(ne - 1):
            start = start + jnp.where(sub > e, padded[e:e + 1, :], 0.0)
        start_ref[...] = start
        run_ref[...] = jnp.zeros_like(run_ref)
        tile_start = lax.broadcasted_iota(jnp.int32, (ne, LANES), 1).astype(F32) * tile
        owner = jnp.sum(jnp.where(tile_start >= start + padded, 1, 0), axis=0, keepdims=True)
        owner = jnp.minimum(owner, ne - 1)
        esub = lax.broadcasted_iota(jnp.int32, (ne, LANES), 0)
        real = jnp.clip(cnt - (tile_start - start), 0.0, float(tile))
        te_ref[...] = owner
        tv_ref[...] = jnp.sum(jnp.where(esub == owner, real, 0.0), axis=0,
                              keepdims=True).astype(jnp.int32)

    @pl.when(phase == 1)
    def _():
        before = _dot(oh.astype(BF16), tri_ref[...])
        slot_e = start_ref[...] + run_ref[...] + before
        s1 = jnp.sum(jnp.where(oh1, slot_e, 0.0), axis=0, keepdims=True)
        s2 = jnp.sum(jnp.where(oh2, slot_e, 0.0), axis=0, keepdims=True)
        slot_ref[0:1, :] = s1.astype(jnp.int32)
        slot_ref[1:2, :] = s2.astype(jnp.int32)
        e21 = jnp.exp(m2 - m1)
        g1 = 1.0 / (1.0 + e21)
        g2 = e21 * g1
        rowi = lax.broadcasted_iota(jnp.int32, (LANES, tr), 0)
        gmat = jnp.where(rowi == 0, g1, jnp.where(rowi == 1, g2, 0.0))
        gate_ref[...] = gmat.T
        run_ref[...] += tile_cnt


def _router(x1, g_ffn, w_router, b_router, tile):
    t, d = x1.shape
    tr = TR
    nt = t // tr
    wp = jnp.pad(w_router, ((0, 0), (0, LANES - N_EXPERTS)))
    wh = wp.astype(BF16)
    wl = (wp - wh.astype(F32)).astype(BF16)
    const = lambda p, j: (0, 0)
    return pl.pallas_call(
        functools.partial(_router_kernel, tile=tile), grid=(2, nt),
        in_specs=[pl.BlockSpec((tr, d), lambda p, j: (j * (1 - p) + (nt - 1) * p, 0)),
                  pl.BlockSpec((1, d), const),
                  pl.BlockSpec(wh.shape, const), pl.BlockSpec(wl.shape, const),
                  pl.BlockSpec((N_EXPERTS, 1), const)],
        out_specs=[pl.BlockSpec((2, tr), lambda p, j: (0, j * p)),
                   pl.BlockSpec((tr, LANES), lambda p, j: (j * p, 0)),
                   pl.BlockSpec((1, LANES), const), pl.BlockSpec((1, LANES), const)],
        out_shape=[jax.ShapeDtypeStruct((2, t), jnp.int32),
                   jax.ShapeDtypeStruct((t, LANES), F32),
                   jax.ShapeDtypeStruct((1, LANES), jnp.int32),
                   jax.ShapeDtypeStruct((1, LANES), jnp.int32)],
        scratch_shapes=[pltpu.VMEM((nt, N_EXPERTS, tr), F32), pltpu.VMEM((tr, tr), BF16),
                        pltpu.VMEM((N_EXPERTS, 1), F32), pltpu.VMEM((N_EXPERTS, 1), F32),
                        pltpu.VMEM((N_EXPERTS, 1), F32)],
        compiler_params=_cparams(("arbitrary", "arbitrary")), name="moe_router",
    )(x1, g_ffn, wh, wl, b_router.reshape(N_EXPERTS, 1))


def _sc_mesh():
    return plsc.VectorSubcoreMesh(core_axis_name="c", subcore_axis_name="s")


def _sc_dispatch(xn, slots, n_slots):
    t, d = xn.shape
    win = SC_WINDOW
    nwin = t // win
    dp = d // SC_PLANES
    idx = slots.reshape(1, 2 * t)

    @functools.partial(pl.kernel,
                       out_type=jax.ShapeDtypeStruct((SC_PLANES, n_slots, dp), xn.dtype),
                       mesh=_sc_mesh(), scratch_types=[], name="moe_dispatch")
    def run(x_hbm, i_hbm, o_hbm):
        for c in range(SC_PLANES):
            def body(x_vmem, i0_vmem, i1_vmem, c=c):
                pltpu.sync_copy(x_vmem, o_hbm.at[c].at[i0_vmem.at[0]])
                pltpu.sync_copy(x_vmem, o_hbm.at[c].at[i1_vmem.at[0]])

            pltpu.emit_pipeline(
                body, grid=(nwin,),
                in_specs=[pl.BlockSpec((win, dp), lambda i, c=c: (i, c)),
                          pl.BlockSpec((1, win), lambda i: (0, i)),
                          pl.BlockSpec((1, win), lambda i: (0, i + nwin))],
                out_specs=[],
                core_axis_name=("c", "s"),
                dimension_semantics=(pltpu.PARALLEL,),
            )(x_hbm, i_hbm, i_hbm)

    return run(xn, idx)


def _sc_combine(y, slots):
    n2 = slots.size
    dp = y.shape[2]
    win = SC_WINDOW
    idx = slots.reshape(1, n2)

    @functools.partial(pl.kernel, out_type=jax.ShapeDtypeStruct((SC_PLANES, n2, dp), y.dtype),
                       mesh=_sc_mesh(), scratch_types=[], name="moe_combine")
    def run(y_hbm, i_hbm, o_hbm):
        for c in range(SC_PLANES):
            def body(i_vmem, o_vmem, c=c):
                pltpu.sync_copy(y_hbm.at[c].at[i_vmem.at[0]], o_vmem)

            pltpu.emit_pipeline(
                body, grid=(n2 // win,),
                in_specs=[pl.BlockSpec((1, win), lambda i: (0, i))],
                out_specs=[pl.BlockSpec((win, dp), lambda i: (i, 0))],
                core_axis_name=("c", "s"),
                dimension_semantics=(pltpu.PARALLEL,),
            )(i_hbm, o_hbm.at[c])

    return run(y, idx)


def _moe_ffn_kernel(te_ref, tv_ref, xs_ref, w1_ref, w3_ref, w2_ref, y_ref,
                    xb_ref, acc_ref, wb1_ref, wb3_ref, wb2_ref):
    i = pl.program_id(0)
    f = pl.program_id(1)
    valid = tv_ref[i]
    tm, dp = xs_ref.shape[1:]
    ts = TSUB_MOE

    @pl.when(f == 0)
    def _():
        rows = lax.broadcasted_iota(jnp.int32, (tm, 1), 0)
        for c, piece in enumerate(_unpack_planes(xs_ref)):
            xb_ref[:, c * dp:(c + 1) * dp] = jnp.where(rows < valid, piece, 0.0).astype(BF16)
        acc_ref[...] = jnp.zeros_like(acc_ref)

    def rows_block(lo, n, w1, w3, w2):
        acc_ref[lo:lo + n, :] += _swiglu_chunk(xb_ref[lo:lo + n, :], w1, w3, w2)

    @pl.when(valid == tm)
    def _():
        rows_block(0, tm, w1_ref[...].astype(BF16), w3_ref[...].astype(BF16),
                   w2_ref[...].astype(BF16))

    @pl.when((valid > 0) & (valid < tm))
    def _():
        wb1_ref[...] = w1_ref[...].astype(BF16)
        wb3_ref[...] = w3_ref[...].astype(BF16)
        wb2_ref[...] = w2_ref[...].astype(BF16)
        for s in range(tm // ts):
            @pl.when(s * ts < valid)
            def _():
                rows_block(s * ts, ts, wb1_ref[...], wb3_ref[...], wb2_ref[...])

    @pl.when(f == pl.num_programs(1) - 1)
    def _():
        packed = _pack_pairs(acc_ref[...])
        for c in range(SC_PLANES):
            y_ref[c] = packed[:, c * dp:(c + 1) * dp]


def _moe_ffn(xs, tile_expert, tile_valid, w1, w3, w2, layer):
    planes, ns, dp = xs.shape
    d = 2 * planes * dp
    ff = w1.shape[3]
    tm, tf = TM_MOE, TF_MOE

    def chunk(i, f, tv):
        return jnp.where(tv[i] > 0, f, 0)

    grid_spec = pltpu.PrefetchScalarGridSpec(
        num_scalar_prefetch=2, grid=(ns // tm, ff // tf),
        in_specs=[pl.BlockSpec((planes, tm, dp), lambda i, f, te, tv: (0, i, 0)),
                  pl.BlockSpec((None, None, d, tf),
                               lambda i, f, te, tv: (layer, te[i], 0, chunk(i, f, tv))),
                  pl.BlockSpec((None, None, d, tf),
                               lambda i, f, te, tv: (layer, te[i], 0, chunk(i, f, tv))),
                  pl.BlockSpec((None, None, tf, d),
                               lambda i, f, te, tv: (layer, te[i], chunk(i, f, tv), 0))],
        out_specs=pl.BlockSpec((planes, tm, dp), lambda i, f, te, tv: (0, i, 0)),
        scratch_shapes=[pltpu.VMEM((tm, d), BF16), pltpu.VMEM((tm, d), F32),
                        pltpu.VMEM((d, tf), BF16), pltpu.VMEM((d, tf), BF16),
                        pltpu.VMEM((tf, d), BF16)])
    return pl.pallas_call(
        _moe_ffn_kernel, grid_spec=grid_spec,
        out_shape=jax.ShapeDtypeStruct((planes, ns, dp), jnp.int32),
        compiler_params=_cparams(("arbitrary", "arbitrary")), name="moe_ffn",
    )(tile_expert, tile_valid, xs, w1, w3, w2)


def _combine_kernel(x1_ref, gt_ref, y0_ref, y1_ref, o_ref):
    o_ref[...] = _gated_sum(x1_ref, gt_ref, y0_ref, y1_ref)


def _combine(x1, gates_tok, yg):
    t, d = x1.shape
    tm = 512
    nt = t // tm
    row = lambda i: (i, 0)
    return pl.pallas_call(
        _combine_kernel, grid=(nt,),
        in_specs=[pl.BlockSpec((tm, d), row), pl.BlockSpec((tm, LANES), row),
                  pl.BlockSpec((SC_PLANES, tm, d // (2 * SC_PLANES)), lambda i: (0, i, 0)),
                  pl.BlockSpec((SC_PLANES, tm, d // (2 * SC_PLANES)), lambda i: (0, i + nt, 0))],
        out_specs=pl.BlockSpec((tm, d), row),
        out_shape=jax.ShapeDtypeStruct((t, d), F32),
        compiler_params=_cparams(("parallel",)), name="moe_combine_residual",
    )(x1, gates_tok, yg, yg)


def _mla_weights(w_in, w_qb, w_kvb, g_qn, g_kn):
    d = w_in.shape[0]
    h = MLA_HEADS
    lat = Q_LORA + KV_LORA
    win = jnp.zeros((d, lat + HEAD_PAD), F32)
    win = win.at[:, :lat].set(w_in[:, :lat])
    win = win.at[:, lat + LANES:lat + LANES + QK_ROPE].set(w_in[:, lat:])
    wq = w_qb.reshape(Q_LORA, h, QK_HEAD)
    wq = jnp.pad(wq, ((0, 0), (0, 0), (0, HEAD_PAD - QK_HEAD))).reshape(Q_LORA, h * HEAD_PAD)
    wkv = w_kvb.reshape(KV_LORA, h, QK_NOPE + V_HEAD)
    wk = wkv[:, :, :QK_NOPE].reshape(KV_LORA, h * QK_NOPE)
    wvt = wkv[:, :, QK_NOPE:].reshape(KV_LORA, h * V_HEAD).T
    scale = math.log2(math.e) / math.sqrt(QK_HEAD)
    gq = jnp.broadcast_to((g_qn * scale).reshape(QK_HEAD, 1), (QK_HEAD, TQ))
    gk = jnp.pad(g_kn, (0, HEAD_PAD - QK_HEAD)).reshape(1, HEAD_PAD)
    return (win.astype(BF16), wq.T.astype(BF16), wk.astype(BF16), wvt.astype(BF16), gq, gk)


def kernel(x, positions, norm_mix, norm_ffn, mla_w_in, mla_g_qa, mla_w_qb, mla_g_kva, mla_w_kvb, mla_g_qn, mla_g_kn, mla_w_o, lru_w_in, lru_b_in, lru_conv_w, lru_conv_b, lru_w_gate_a, lru_b_gate_a, lru_w_gate_i, lru_b_gate_i, lru_lambda, lru_w_out, lru_b_out, ffn_w1, ffn_w3, ffn_w2, moe_w_router, moe_b_router, moe_w1, moe_w3, moe_w2):
    batch, seq, d = x.shape
    t = batch * seq
    depth = norm_mix.shape[0]
    n_slots = 2 * t + N_EXPERTS * TM_MOE
    n_tiles = n_slots // TM_MOE
    row = lambda v: v.reshape(1, -1)

    cs, cst = _rope_tables(positions)
    parts = (x.reshape(t, d),)
    for i in range(depth):
        j = i // 2
        if i % 2 == 0:
            w_in, w_qbt, w_kb, w_vt, g_qn, g_kn = _mla_weights(
                mla_w_in[j], mla_w_qb[j], mla_w_kvb[j], mla_g_qn[j], mla_g_kn[j])
            xr, q, k, vt = _mla_qkv(parts, cs, cst, row(norm_mix[i]), w_in, row(mla_g_qa[j]),
                                    w_qbt, row(mla_g_kva[j]), w_kb, w_vt, g_qn, g_kn)
            x1, xn = _attention(q, k, vt, xr, mla_w_o[j].astype(BF16), row(norm_ffn[i]),
                                batch, seq)
            parts = (_dense_ffn(xn, x1, ffn_w1, ffn_w3, ffn_w2, j),)
        else:
            (xr,) = parts
            x1, xn = _lru_mixer(
                xr.reshape(batch, seq, d), row(norm_mix[i]), lru_w_in[j].astype(BF16),
                row(lru_b_in[j]), lru_conv_w[j], row(lru_conv_b[j]),
                lru_w_gate_a[j].astype(BF16), lru_b_gate_a[j], lru_w_gate_i[j].astype(BF16),
                lru_b_gate_i[j], row(lru_lambda[j]), lru_w_out[j].astype(BF16),
                row(lru_b_out[j]), row(norm_ffn[i]))
            x1 = x1.reshape(t, d)
            xn = xn.reshape(t, d // 2)
            slots, gates_tok, te, tv = _router(x1, row(norm_ffn[i]), moe_w_router[j],
                                               moe_b_router[j], TM_MOE)
            xs = _sc_dispatch(xn, slots, n_slots)
            y = _moe_ffn(xs, te[0, :n_tiles], tv[0, :n_tiles], moe_w1, moe_w3, moe_w2, j)
            yg = _sc_combine(y, slots)
            parts = (x1, gates_tok, yg)
    if len(parts) == 3:
        out = _combine(*parts)
    else:
        out = parts[0]
    return out.reshape(batch, seq, d)
```

```python
import functools
import math

import numpy as np
import jax
import jax.numpy as jnp
from jax import lax
from jax.experimental import pallas as pl
from jax.experimental.pallas import tpu as pltpu
from jax.experimental.pallas import tpu_sc as plsc

F32 = jnp.float32
BF16 = jnp.bfloat16

NORM_EPS = 1e-6
CHUNK = 64
MLA_HEADS = 8
QK_NOPE = 128
QK_ROPE = 64
QK_HEAD = QK_NOPE + QK_ROPE
V_HEAD = 128
Q_LORA = 384
KV_LORA = 256
ROPE_BASE = 10000.0
LRU_HEADS = 8
CONV_WIDTH = 4
LRU_C = 8.0
N_EXPERTS = 8

LANES = 128
SUBLANES = 8
HEAD_PAD = 2 * LANES
VMEM_LIMIT = 56 * 1024 * 1024

TQ = 256
TM_QKV = 2 * TQ
TM_FFN = 1024
TF_FFN = 256
T_LRU = 64
LRU_PITCH = T_LRU + SUBLANES
TR = 512
TM_MOE = 1024
TSUB_MOE = 256
TF_MOE = 512
SC_WINDOW = 128
SC_PLANES = 2


def _cparams(sem):
    return pltpu.CompilerParams(dimension_semantics=sem, vmem_limit_bytes=VMEM_LIMIT)


def _rms(x, g):
    return x * lax.rsqrt(jnp.mean(x * x, axis=-1, keepdims=True) + NORM_EPS) * g


def _dot(a, b):
    return jnp.dot(a, b, preferred_element_type=F32)


def _dot_nt(a, b):
    return lax.dot_general(a, b, (((1,), (1,)), ((), ())), preferred_element_type=F32)


def _pack_pairs(x):
    n = x.shape[1] // 2
    bits = lax.bitcast_convert_type(x.astype(BF16).astype(F32), jnp.int32)
    return bits[:, :n] | lax.shift_right_logical(bits[:, n:], 16)


def _unpack_pairs(word):
    hi = lax.bitcast_convert_type(word & jnp.int32(-65536), F32)
    lo = lax.bitcast_convert_type(lax.shift_left(word, 16), F32)
    return hi, lo


def _unpack_planes(ref):
    halves = [_unpack_pairs(ref[c]) for c in range(SC_PLANES)]
    return [h for h, _ in halves] + [l for _, l in halves]


def _rope_kernel(pos_ref, inv_ref, cs_ref, cst_ref):
    ang = inv_ref[...] * pos_ref[...].astype(F32)
    half = QK_ROPE // 2
    ct = jnp.cos(ang)
    st = jnp.sin(ang)
    cst_ref[0:half, :] = ct[0:half, :]
    cst_ref[half:QK_ROPE, :] = st[0:half, :]
    c = ct.T
    s = st.T
    lane = lax.broadcasted_iota(jnp.int32, c.shape, 1)
    cs_ref[:, 0:LANES] = jnp.where(lane < QK_ROPE, c, 0.0)
    cs_ref[:, LANES:2 * LANES] = jnp.where(lane < half, -s, 0.0)
    cs_ref[:, 2 * LANES:3 * LANES] = jnp.where((lane >= half) & (lane < QK_ROPE), s, 0.0)


def _rope_tables(positions):
    t = positions.size
    tm = 512
    inv = 1.0 / (ROPE_BASE ** (np.arange(0, QK_ROPE, 2, dtype=np.float32) / QK_ROPE))
    inv_col = np.zeros((LANES, 1), np.float32)
    inv_col[:QK_ROPE // 2, 0] = inv
    inv_col[QK_ROPE // 2:QK_ROPE, 0] = inv
    return pl.pallas_call(
        _rope_kernel,
        grid=(t // tm,),
        in_specs=[pl.BlockSpec((1, tm), lambda i: (0, i)),
                  pl.BlockSpec((LANES, 1), lambda i: (0, 0))],
        out_specs=[pl.BlockSpec((tm, 3 * LANES), lambda i: (i, 0)),
                   pl.BlockSpec((QK_ROPE, tm), lambda i: (0, i))],
        out_shape=[jax.ShapeDtypeStruct((t, 3 * LANES), F32),
                   jax.ShapeDtypeStruct((QK_ROPE, t), F32)],
        compiler_params=_cparams(("parallel",)),
        name="rope_tables",
    )(positions.reshape(1, t), jnp.asarray(inv_col))


def _rope_rot(x, cs):
    c = cs[:, 0:LANES]
    sa = cs[:, LANES:2 * LANES]
    sb = cs[:, 2 * LANES:3 * LANES]
    return (x * c + pltpu.roll(x, LANES - QK_ROPE // 2, 1) * sa
            + pltpu.roll(x, QK_ROPE // 2, 1) * sb)


def _mla_latent(x, gmix_ref, win_ref):
    return _dot(_rms(x, gmix_ref[...]).astype(BF16), win_ref[...])


def _mla_expand(hcat, gqa_ref, wqb_ref, gkva_ref, wkb_ref, wvt_ref):
    cq = hcat[:, :Q_LORA]
    ckv = hcat[:, Q_LORA:Q_LORA + KV_LORA]
    kr = hcat[:, Q_LORA + KV_LORA + LANES:Q_LORA + KV_LORA + 2 * LANES]
    qft = _dot_nt(wqb_ref[...], _rms(cq, gqa_ref[...]).astype(BF16))
    ckvn = _rms(ckv, gkva_ref[...]).astype(BF16)
    kf = _dot(ckvn, wkb_ref[...])
    vt = _dot_nt(wvt_ref[...], ckvn).astype(BF16)
    return qft, kf, kr, vt


def _mla_q_heads(qft, cst, gqn_ref, qt_ref, s):
    half = QK_ROPE // 2
    c, sn = cst[0:half, :], cst[half:QK_ROPE, :]
    g = gqn_ref[...]
    zero = jnp.zeros((HEAD_PAD - QK_HEAD, qft.shape[1]), BF16)
    for h in range(MLA_HEADS):
        blk = qft[h * HEAD_PAD:h * HEAD_PAD + QK_HEAD, :]
        ss = jnp.sum(blk * blk, axis=0, keepdims=True)
        inv = lax.rsqrt(ss * (1.0 / QK_HEAD) + NORM_EPS)
        y = blk * g * inv
        x1, x2 = y[QK_NOPE:QK_NOPE + half, :], y[QK_NOPE + half:QK_HEAD, :]
        base = h * HEAD_PAD
        qt_ref[s, base:base + QK_NOPE, :] = y[0:QK_NOPE, :].astype(BF16)
        qt_ref[s, base + QK_NOPE:base + QK_NOPE + half, :] = (x1 * c - x2 * sn).astype(BF16)
        qt_ref[s, base + QK_NOPE + half:base + QK_HEAD, :] = (x1 * sn + x2 * c).astype(BF16)
        qt_ref[s, base + QK_HEAD:base + HEAD_PAD, :] = zero


def _mla_k_heads(kf, kr, cs, gkn_ref, k_ref, rows):
    gkn_n, gkn_r = gkn_ref[:, 0:LANES], gkn_ref[:, LANES:2 * LANES]
    kr_ss = jnp.sum(kr * kr, axis=-1, keepdims=True)
    kr_rot = _rope_rot(kr * gkn_r, cs)
    for h in range(MLA_HEADS):
        kn = kf[:, h * LANES:(h + 1) * LANES]
        ssk = jnp.sum(kn * kn, axis=-1, keepdims=True) + kr_ss
        invk = lax.rsqrt(ssk * (1.0 / QK_HEAD) + NORM_EPS)
        k_ref[rows, h * HEAD_PAD:h * HEAD_PAD + LANES] = (kn * invk * gkn_n).astype(BF16)
        k_ref[rows, h * HEAD_PAD + LANES:(h + 1) * HEAD_PAD] = (kr_rot * invk).astype(BF16)


def _mla_qkv_body(x, cs_ref, cst_ref, gmix_ref, win_ref, gqa_ref, wqb_ref, gkva_ref, wkb_ref,
                  wvt_ref, gqn_ref, gkn_ref, qt_ref, k_ref, vt_ref):
    nsub = x.shape[0] // TQ
    rows = [slice(s * TQ, (s + 1) * TQ) for s in range(nsub)]
    hcats = [_mla_latent(x[r], gmix_ref, win_ref) for r in rows]
    mids = [_mla_expand(hc, gqa_ref, wqb_ref, gkva_ref, wkb_ref, wvt_ref) for hc in hcats]
    for s, (qft, kf, kr, vt) in enumerate(mids):
        vt_ref[s] = vt
    for s, (r, (qft, kf, kr, vt)) in enumerate(zip(rows, mids)):
        _mla_q_heads(qft, cst_ref[:, r], gqn_ref, qt_ref, s)
        _mla_k_heads(kf, kr, cs_ref[r, :], gkn_ref, k_ref, r)


def _gated_sum(x1_ref, gt_ref, y0_ref, y1_ref):
    gt = gt_ref[...]
    g0, g1 = gt[:, 0:1], gt[:, 1:2]
    y = jnp.concatenate([g0 * a + g1 * b
                         for a, b in zip(_unpack_planes(y0_ref), _unpack_planes(y1_ref))], axis=1)
    return x1_ref[...] + y


def _mla_qkv_kernel(x_ref, *rest):
    _mla_qkv_body(x_ref[...], *rest)


def _mla_qkv_combine_kernel(x1_ref, gt_ref, y0_ref, y1_ref, *rest):
    *mid, xo_ref, q_ref, k_ref, v_ref = rest
    x = _gated_sum(x1_ref, gt_ref, y0_ref, y1_ref)
    xo_ref[...] = x
    _mla_qkv_body(x, *mid, q_ref, k_ref, v_ref)


def _mla_qkv(x_parts, cs, cst, gmix, w_in, g_qa, w_qbt, g_kva, w_kb, w_vt, g_qn, g_kn):
    t, d = x_parts[0].shape
    tm = TM_QKV
    nsub = tm // TQ
    nt = t // tm
    row = lambda i: (i, 0)
    const = lambda i: (0, 0)
    h = MLA_HEADS
    weights = [gmix, w_in, g_qa, w_qbt, g_kva, w_kb, w_vt, g_qn, g_kn]
    w_specs = [pl.BlockSpec(w.shape, const) for w in weights]
    qkv_specs = [pl.BlockSpec((nsub, h * HEAD_PAD, TQ), lambda i: (i, 0, 0)),
                 pl.BlockSpec((tm, h * HEAD_PAD), row),
                 pl.BlockSpec((nsub, h * V_HEAD, TQ), lambda i: (i, 0, 0))]
    qkv_shapes = [jax.ShapeDtypeStruct((t // TQ, h * HEAD_PAD, TQ), BF16),
                  jax.ShapeDtypeStruct((t, h * HEAD_PAD), BF16),
                  jax.ShapeDtypeStruct((t // TQ, h * V_HEAD, TQ), BF16)]
    cs_specs = [pl.BlockSpec((tm, 3 * LANES), row), pl.BlockSpec((QK_ROPE, tm), lambda i: (0, i))]
    if len(x_parts) == 1:
        q, k, v = pl.pallas_call(
            _mla_qkv_kernel, grid=(nt,),
            in_specs=[pl.BlockSpec((tm, d), row)] + cs_specs + w_specs,
            out_specs=qkv_specs, out_shape=qkv_shapes,
            compiler_params=_cparams(("parallel",)), name="mla_qkv",
        )(x_parts[0], cs, cst, *weights)
        return x_parts[0], q, k, v
    x1, gates_tok, yg = x_parts
    x, q, k, v = pl.pallas_call(
        _mla_qkv_combine_kernel, grid=(nt,),
        in_specs=[pl.BlockSpec((tm, d), row), pl.BlockSpec((tm, LANES), row),
                  pl.BlockSpec((SC_PLANES, tm, d // (2 * SC_PLANES)), lambda i: (0, i, 0)),
                  pl.BlockSpec((SC_PLANES, tm, d // (2 * SC_PLANES)), lambda i: (0, i + nt, 0)),
                  ] + cs_specs + w_specs,
        out_specs=[pl.BlockSpec((tm, d), row)] + qkv_specs,
        out_shape=[jax.ShapeDtypeStruct((t, d), F32)] + qkv_shapes,
        compiler_params=_cparams(("parallel",)), name="mla_qkv_combine",
    )(x1, gates_tok, yg, yg, cs, cst, *weights)
    return x, q, k, v


def _attn_kernel(q_ref, k_ref, vt_ref, x_ref, wo_ref, g_ref, x1_ref, xn_ref,
                 s_ref, m_ref, l_ref, acc_ref, o_ref):
    i = pl.program_id(1)
    m_ref[...] = jnp.full(m_ref.shape, -jnp.inf, F32)
    l_ref[...] = jnp.zeros(l_ref.shape, F32)
    acc_ref[...] = jnp.zeros(acc_ref.shape, F32)

    def scores(h, j):
        off = pl.multiple_of(j * TQ, TQ)
        hs = slice(h * HEAD_PAD, (h + 1) * HEAD_PAD)
        s_ref[h] = _dot(k_ref[pl.ds(off, TQ), hs], q_ref[0, hs, :])

    def update(h, j, masked):
        st = s_ref[h]
        if masked:
            kc = lax.broadcasted_iota(jnp.int32, (TQ, TQ), 0) // CHUNK
            qc = lax.broadcasted_iota(jnp.int32, (TQ, TQ), 1) // CHUNK
            st = jnp.where(kc <= qc, st, -jnp.inf)
        m_old = m_ref[h]
        m_new = jnp.maximum(m_old, jnp.max(st, axis=0, keepdims=True))
        p = jnp.exp2(st - m_new)
        alpha = jnp.exp2(m_old - m_new)
        l_ref[h] = alpha * l_ref[h] + jnp.sum(p, axis=0, keepdims=True)
        pv = _dot(vt_ref[j, h * V_HEAD:(h + 1) * V_HEAD, :], p.astype(BF16))
        acc_ref[h] = alpha * acc_ref[h] + pv
        m_ref[h] = m_new

    def full_tile(j, carry):
        for h in range(MLA_HEADS):
            scores(h, j)
        for h in range(MLA_HEADS):
            update(h, j, False)
        return carry

    lax.fori_loop(0, i, full_tile, 0)
    for h in range(MLA_HEADS):
        scores(h, i)
    for h in range(MLA_HEADS):
        update(h, i, True)
        o_ref[:, h * V_HEAD:(h + 1) * V_HEAD] = (acc_ref[h] / l_ref[h]).T.astype(BF16)

    x1 = x_ref[...] + _dot(o_ref[...], wo_ref[...])
    x1_ref[...] = x1
    xn_ref[...] = _rms(x1, g_ref[...]).astype(BF16)


def _attention(q, k, vt, x, w_o, g_ffn, batch, seq):
    h = MLA_HEADS
    nq = seq // TQ
    t, d = x.shape
    tile = lambda b, i: (b * nq + i, 0)
    const = lambda b, i: (0, 0)
    return pl.pallas_call(
        _attn_kernel, grid=(batch, nq),
        in_specs=[pl.BlockSpec((1, h * HEAD_PAD, TQ), lambda b, i: (b * nq + i, 0, 0)),
                  pl.BlockSpec((seq, h * HEAD_PAD), lambda b, i: (b, 0)),
                  pl.BlockSpec((nq, h * V_HEAD, TQ), lambda b, i: (b, 0, 0)),
                  pl.BlockSpec((TQ, d), tile), pl.BlockSpec(w_o.shape, const),
                  pl.BlockSpec((1, d), const)],
        out_specs=[pl.BlockSpec((TQ, d), tile), pl.BlockSpec((TQ, d), tile)],
        out_shape=[jax.ShapeDtypeStruct((t, d), F32), jax.ShapeDtypeStruct((t, d), BF16)],
        scratch_shapes=[pltpu.VMEM((h, TQ, TQ), F32),
                        pltpu.VMEM((h, 1, TQ), F32), pltpu.VMEM((h, 1, TQ), F32),
                        pltpu.VMEM((h, V_HEAD, TQ), F32),
                        pltpu.VMEM((TQ, h * V_HEAD), BF16)],
        compiler_params=_cparams(("parallel", "arbitrary")), name="attention",
    )(q, k, vt, x, w_o, g_ffn)


def _swiglu_chunk(x, w1, w3, w2):
    a = _dot(x, w1)
    b = _dot(x, w3)
    hid = (a * jax.nn.sigmoid(a)) * b
    return _dot(hid.astype(BF16), w2)


def _weight_chunk_copies(w1_hbm, w3_hbm, w2_hbm, lead, c, tf, slot, w1b, w3b, w2b, sem):
    cols = pl.ds(pl.multiple_of(c * tf, tf), tf)
    return (pltpu.make_async_copy(w1_hbm.at[lead + (slice(None), cols)], w1b.at[slot], sem.at[0, slot]),
            pltpu.make_async_copy(w3_hbm.at[lead + (slice(None), cols)], w3b.at[slot], sem.at[1, slot]),
            pltpu.make_async_copy(w2_hbm.at[lead + (cols, slice(None))], w2b.at[slot], sem.at[2, slot]))


def _ffn_kernel(xn_ref, x1_ref, w1_hbm, w3_hbm, w2_hbm, o_ref, w1b, w3b, w2b, sem, *, layer, tf):
    i = pl.program_id(0)
    n_tiles = pl.num_programs(0)
    nc = w1_hbm.shape[2] // tf

    def copies(c, slot):
        return _weight_chunk_copies(w1_hbm, w3_hbm, w2_hbm, (layer,), c, tf, slot, w1b, w3b, w2b, sem)

    @pl.when(i == 0)
    def _():
        for cp in copies(0, 0):
            cp.start()

    o_ref[...] = x1_ref[...]

    def chunk(c, carry):
        slot = lax.rem(i * nc + c, 2)
        nxt = lax.rem(c + 1, nc)

        @pl.when((c + 1 < nc) | (i + 1 < n_tiles))
        def _():
            for cp in copies(nxt, 1 - slot):
                cp.start()

        for cp in copies(c, slot):
            cp.wait()
        o_ref[...] += _swiglu_chunk(xn_ref[...], w1b[slot].astype(BF16), w3b[slot].astype(BF16),
                                    w2b[slot].astype(BF16))
        return carry

    lax.fori_loop(0, nc, chunk, 0)


def _dense_ffn(xn, x1, w1, w3, w2, layer):
    t, d = x1.shape
    tm, tf = TM_FFN, TF_FFN
    row = lambda i: (i, 0)
    hbm = pl.BlockSpec(memory_space=pl.ANY)
    return pl.pallas_call(
        functools.partial(_ffn_kernel, layer=layer, tf=tf), grid=(t // tm,),
        in_specs=[pl.BlockSpec((tm, d), row), pl.BlockSpec((tm, d), row), hbm, hbm, hbm],
        out_specs=pl.BlockSpec((tm, d), row),
        out_shape=jax.ShapeDtypeStruct((t, d), F32),
        scratch_shapes=[pltpu.VMEM((2, d, tf), F32), pltpu.VMEM((2, d, tf), F32),
                        pltpu.VMEM((2, tf, d), F32), pltpu.SemaphoreType.DMA((3, 2))],
        compiler_params=_cparams(("arbitrary",)), name="dense_ffn",
    )(xn, x1, w1, w3, w2)


def _lru_kernel(x_ref, gmix_ref, win_ref, bin_ref, cw_ref, cb_ref, wga_ref, bga_ref,
                wgi_ref, bgi_ref, lam_ref, wout_ref, bout_ref, gffn_ref,
                x1_ref, xn_ref,
                xpad_ref, a_ref, u_ref, gate_ref, yg_ref, h_ref):
    nb, tt, d = x_ref.shape
    w = lam_ref.shape[1]
    m = nb * tt

    @pl.when(pl.program_id(0) == 0)
    def _():
        h_ref[...] = jnp.zeros_like(h_ref)
        xpad_ref[:, 0:SUBLANES, :] = jnp.zeros((nb, SUBLANES, w), F32)

    x = x_ref[...].reshape(m, d)
    xn = _rms(x, gmix_ref[...]).astype(BF16)
    hcat = _dot(xn, win_ref[...]) + bin_ref[...]
    gate_ref[...] = jax.nn.gelu(hcat[:, :w], approximate=True)
    xpad_ref[:, SUBLANES:, :] = hcat[:, w:].reshape(nb, tt, w)

    xc = cb_ref[...].reshape(1, 1, w) + jnp.zeros((nb, tt, w), F32)
    for j in range(CONV_WIDTH):
        lo = SUBLANES - (CONV_WIDTH - 1) + j
        xc = xc + xpad_ref[:, lo:lo + tt, :] * cw_ref[j:j + 1, :].reshape(1, 1, w)
    xpad_ref[:, 0:SUBLANES, :] = xpad_ref[:, tt:tt + SUBLANES, :]
    xc = xc.reshape(m, w)

    lam = lam_ref[...]
    log_sig = jnp.minimum(lam, 0.0) - jnp.log1p(jnp.exp(-jnp.abs(lam)))
    for hh in range(LRU_HEADS):
        sl = slice(hh * LANES, (hh + 1) * LANES)
        xh = xc[:, sl]
        xhb = xh.astype(BF16)
        r = jax.nn.sigmoid(_dot(xhb, wga_ref[hh]) + bga_ref[hh:hh + 1, :])
        ig = jax.nn.sigmoid(_dot(xhb, wgi_ref[hh]) + bgi_ref[hh:hh + 1, :])
        log_a = LRU_C * r * log_sig[:, sl]
        a = jnp.exp(log_a)
        z = 1.0 - a * a
        u = (z * lax.rsqrt(jnp.maximum(z, 1e-30))) * (ig * xh)
        for b in range(nb):
            a_ref[hh, b * LRU_PITCH:b * LRU_PITCH + tt, :] = a[b * tt:(b + 1) * tt, :]
            u_ref[hh, b * LRU_PITCH:b * LRU_PITCH + tt, :] = u[b * tt:(b + 1) * tt, :]

    def scan_step(t, hs):
        new = []
        for hh in range(LRU_HEADS):
            rows = pl.ds(t, nb, stride=LRU_PITCH)
            hv = a_ref[hh, rows, :] * hs[hh] + u_ref[hh, rows, :]
            u_ref[hh, rows, :] = hv
            new.append(hv)
        return tuple(new)

    hs = lax.fori_loop(0, tt, scan_step, tuple(h_ref[hh] for hh in range(LRU_HEADS)), unroll=4)
    for hh in range(LRU_HEADS):
        h_ref[hh] = hs[hh]

    for hh in range(LRU_HEADS):
        sl = slice(hh * LANES, (hh + 1) * LANES)
        for b in range(nb):
            y = u_ref[hh, b * LRU_PITCH:b * LRU_PITCH + tt, :]
            yg_ref[b * tt:(b + 1) * tt, sl] = (y * gate_ref[b * tt:(b + 1) * tt, sl]).astype(BF16)

    x1 = x + _dot(yg_ref[...], wout_ref[...]) + bout_ref[...]
    x1_ref[...] = x1.reshape(nb, tt, d)
    xn_ref[...] = _pack_pairs(_rms(x1, gffn_ref[...])).reshape(nb, tt, d // 2)


def _lru_mixer(x3, gmix, w_in, b_in, conv_w, conv_b, wga, bga, wgi, bgi, lam, w_out, b_out, g_ffn):
    nb, seq, d = x3.shape
    w = lam.shape[1]
    tt = T_LRU
    m = nb * tt
    consts = [gmix, w_in, b_in, conv_w, conv_b, wga, bga, wgi, bgi, lam, w_out, b_out, g_ffn]
    const_specs = [pl.BlockSpec(c.shape, (lambda i, n=c.ndim: (0,) * n)) for c in consts]
    blk = pl.BlockSpec((nb, tt, d), lambda i: (0, i, 0))
    return pl.pallas_call(
        _lru_kernel, grid=(seq // tt,),
        in_specs=[blk] + const_specs,
        out_specs=[blk, pl.BlockSpec((nb, tt, d // 2), lambda i: (0, i, 0))],
        out_shape=[jax.ShapeDtypeStruct((nb, seq, d), F32),
                   jax.ShapeDtypeStruct((nb, seq, d // 2), jnp.int32)],
        scratch_shapes=[pltpu.VMEM((nb, tt + SUBLANES, w), F32),
                        pltpu.VMEM((LRU_HEADS, nb * LRU_PITCH, LANES), F32),
                        pltpu.VMEM((LRU_HEADS, nb * LRU_PITCH, LANES), F32),
                        pltpu.VMEM((m, w), F32),
                        pltpu.VMEM((m, w), BF16),
                        pltpu.VMEM((LRU_HEADS, nb, LANES), F32)],
        compiler_params=_cparams(("arbitrary",)), name="rglru_mixer",
    )(x3, *consts)


def _router_kernel(x1_ref, g_ref, wh_ref, wl_ref, br_ref, slot_ref, gate_ref, te_ref, tv_ref,
                   lg_ref, tri_ref, cnt_ref, run_ref, start_ref, *, tile):
    phase = pl.program_id(0)
    j = pl.program_id(1)
    ne = N_EXPERTS
    tr = x1_ref.shape[0]

    @pl.when((phase == 0) & (j == 0))
    def _():
        cnt_ref[...] = jnp.zeros_like(cnt_ref)
        tri_ref[...] = jnp.where(lax.broadcasted_iota(jnp.int32, (tr, tr), 0)
                                 < lax.broadcasted_iota(jnp.int32, (tr, tr), 1),
                                 1.0, 0.0).astype(BF16)

    @pl.when(phase == 0)
    def _():
        x = _rms(x1_ref[...], g_ref[...])
        xh = x.astype(BF16)
        xl = (x - xh.astype(F32)).astype(BF16)
        wh = wh_ref[...]
        nat = _dot(xh, wh) + _dot(xl, wh) + _dot(xh, wl_ref[...])
        lg_ref[j] = nat.T[0:ne, :] + br_ref[...]

    logits = lg_ref[j]
    eidx = lax.broadcasted_iota(jnp.int32, (ne, tr), 0)
    m1 = jnp.max(logits, axis=0, keepdims=True)
    i1 = jnp.min(jnp.where(logits == m1, eidx, ne), axis=0, keepdims=True)
    oh1 = eidx == i1
    rest = jnp.where(oh1, -jnp.inf, logits)
    m2 = jnp.max(rest, axis=0, keepdims=True)
    i2 = jnp.min(jnp.where(rest == m2, eidx, ne), axis=0, keepdims=True)
    oh2 = eidx == i2
    oh = jnp.where(oh1 | oh2, 1.0, 0.0)
    tile_cnt = jnp.sum(oh, axis=1, keepdims=True)

    @pl.when(phase == 0)
    def _():
        cnt_ref[...] += tile_cnt

    @pl.when((phase == 1) & (j == 0))
    def _():
        cnt = cnt_ref[...]
        padded = jnp.ceil(cnt * (1.0 / tile)) * tile
        sub = lax.broadcasted_iota(jnp.int32, (ne, 1), 0)
        start = jnp.zeros((ne, 1), F32)
        for e in range(ne - 1):
            start = start + jnp.where(sub > e, padded[e:e + 1, :], 0.0)
        start_ref[...] = start
        run_ref[...] = jnp.zeros_like(run_ref)
        tile_start = lax.broadcasted_iota(jnp.int32, (ne, LANES), 1).astype(F32) * tile
        owner = jnp.sum(jnp.where(tile_start >= start + padded, 1, 0), axis=0, keepdims=True)
        owner = jnp.minimum(owner, ne - 1)
        esub = lax.broadcasted_iota(jnp.int32, (ne, LANES), 0)
        real = jnp.clip(cnt - (tile_start - start), 0.0, float(tile))
        te_ref[...] = owner
        tv_ref[...] = jnp.sum(jnp.where(esub == owner, real, 0.0), axis=0,
                              keepdims=True).astype(jnp.int32)

    @pl.when(phase == 1)
    def _():
        before = _dot(oh.astype(BF16), tri_ref[...])
        slot_e = start_ref[...] + run_ref[...] + before
        s1 = jnp.sum(jnp.where(oh1, slot_e, 0.0), axis=0, keepdims=True)
        s2 = jnp.sum(jnp.where(oh2, slot_e, 0.0), axis=0, keepdims=True)
        slot_ref[0:1, :] = s1.astype(jnp.int32)
        slot_ref[1:2, :] = s2.astype(jnp.int32)
        e21 = jnp.exp(m2 - m1)
        g1 = 1.0 / (1.0 + e21)
        g2 = e21 * g1
        rowi = lax.broadcasted_iota(jnp.int32, (LANES, tr), 0)
        gmat = jnp.where(rowi == 0, g1, jnp.where(rowi == 1, g2, 0.0))
        gate_ref[...] = gmat.T
        run_ref[...] += tile_cnt


def _router(x1, g_ffn, w_router, b_router, tile):
    t, d = x1.shape
    tr = TR
    nt = t // tr
    wp = jnp.pad(w_router, ((0, 0), (0, LANES - N_EXPERTS)))
    wh = wp.astype(BF16)
    wl = (wp - wh.astype(F32)).astype(BF16)
    const = lambda p, j: (0, 0)
    return pl.pallas_call(
        functools.partial(_router_kernel, tile=tile), grid=(2, nt),
        in_specs=[pl.BlockSpec((tr, d), lambda p, j: (j * (1 - p) + (nt - 1) * p, 0)),
                  pl.BlockSpec((1, d), const),
                  pl.BlockSpec(wh.shape, const), pl.BlockSpec(wl.shape, const),
                  pl.BlockSpec((N_EXPERTS, 1), const)],
        out_specs=[pl.BlockSpec((2, tr), lambda p, j: (0, j * p)),
                   pl.BlockSpec((tr, LANES), lambda p, j: (j * p, 0)),
                   pl.BlockSpec((1, LANES), const), pl.BlockSpec((1, LANES), const)],
        out_shape=[jax.ShapeDtypeStruct((2, t), jnp.int32),
                   jax.ShapeDtypeStruct((t, LANES), F32),
                   jax.ShapeDtypeStruct((1, LANES), jnp.int32),
                   jax.ShapeDtypeStruct((1, LANES), jnp.int32)],
        scratch_shapes=[pltpu.VMEM((nt, N_EXPERTS, tr), F32), pltpu.VMEM((tr, tr), BF16),
                        pltpu.VMEM((N_EXPERTS, 1), F32), pltpu.VMEM((N_EXPERTS, 1), F32),
                        pltpu.VMEM((N_EXPERTS, 1), F32)],
        compiler_params=_cparams(("arbitrary", "arbitrary")), name="moe_router",
    )(x1, g_ffn, wh, wl, b_router.reshape(N_EXPERTS, 1))


def _sc_mesh():
    return plsc.VectorSubcoreMesh(core_axis_name="c", subcore_axis_name="s")


def _sc_dispatch(xn, slots, n_slots):
    t, d = xn.shape
    win = SC_WINDOW
    nwin = t // win
    dp = d // SC_PLANES
    idx = slots.reshape(1, 2 * t)

    @functools.partial(pl.kernel,
                       out_type=jax.ShapeDtypeStruct((SC_PLANES, n_slots, dp), xn.dtype),
                       mesh=_sc_mesh(), scratch_types=[], name="moe_dispatch")
    def run(x_hbm, i_hbm, o_hbm):
        for c in range(SC_PLANES):
            def body(x_vmem, i0_vmem, i1_vmem, c=c):
                pltpu.sync_copy(x_vmem, o_hbm.at[c].at[i0_vmem.at[0]])
                pltpu.sync_copy(x_vmem, o_hbm.at[c].at[i1_vmem.at[0]])

            pltpu.emit_pipeline(
                body, grid=(nwin,),
                in_specs=[pl.BlockSpec((win, dp), lambda i, c=c: (i, c)),
                          pl.BlockSpec((1, win), lambda i: (0, i)),
                          pl.BlockSpec((1, win), lambda i: (0, i + nwin))],
                out_specs=[],
                core_axis_name=("c", "s"),
                dimension_semantics=(pltpu.PARALLEL,),
            )(x_hbm, i_hbm, i_hbm)

    return run(xn, idx)


def _sc_combine(y, slots):
    n2 = slots.size
    dp = y.shape[2]
    win = SC_WINDOW
    idx = slots.reshape(1, n2)

    @functools.partial(pl.kernel, out_type=jax.ShapeDtypeStruct((SC_PLANES, n2, dp), y.dtype),
                       mesh=_sc_mesh(), scratch_types=[], name="moe_combine")
    def run(y_hbm, i_hbm, o_hbm):
        for c in range(SC_PLANES):
            def body(i_vmem, o_vmem, c=c):
                pltpu.sync_copy(y_hbm.at[c].at[i_vmem.at[0]], o_vmem)

            pltpu.emit_pipeline(
                body, grid=(n2 // win,),
                in_specs=[pl.BlockSpec((1, win), lambda i: (0, i))],
                out_specs=[pl.BlockSpec((win, dp), lambda i: (i, 0))],
                core_axis_name=("c", "s"),
                dimension_semantics=(pltpu.PARALLEL,),
            )(i_hbm, o_hbm.at[c])

    return run(y, idx)


def _moe_ffn_kernel(te_ref, tv_ref, xs_ref, w1_hbm, w3_hbm, w2_hbm, y_ref,
                    xb_ref, acc_ref, w1b, w3b, w2b, wb1_ref, wb3_ref, wb2_ref, sem, *, layer, tf):
    i = pl.program_id(0)
    n_tiles = pl.num_programs(0)
    valid = tv_ref[i]
    tm, dp = xs_ref.shape[1:]
    ts = TSUB_MOE
    nc = w1_hbm.shape[3] // tf

    def copies(tile, c, slot):
        return _weight_chunk_copies(w1_hbm, w3_hbm, w2_hbm, (layer, te_ref[tile]), c, tf, slot,
                                    w1b, w3b, w2b, sem)

    @pl.when((i == 0) & (valid > 0))
    def _():
        for cp in copies(0, 0, 0):
            cp.start()

    rows = lax.broadcasted_iota(jnp.int32, (tm, 1), 0)
    for c, piece in enumerate(_unpack_planes(xs_ref)):
        xb_ref[:, c * dp:(c + 1) * dp] = jnp.where(rows < valid, piece, 0.0).astype(BF16)
    acc_ref[...] = jnp.zeros_like(acc_ref)

    def rows_block(lo, n, w1, w3, w2):
        acc_ref[lo:lo + n, :] += _swiglu_chunk(xb_ref[lo:lo + n, :], w1, w3, w2)

    @pl.when(valid > 0)
    def _():
        nxt_tile = jnp.minimum(i + 1, n_tiles - 1)
        next_tile_live = (i + 1 < n_tiles) & (tv_ref[nxt_tile] > 0)

        def chunk(c, carry):
            slot = lax.rem(i * nc + c, 2)
            last = c + 1 == nc

            @pl.when(jnp.logical_not(last) | next_tile_live)
            def _():
                for cp in copies(jnp.where(last, nxt_tile, i), jnp.where(last, 0, c + 1), 1 - slot):
                    cp.start()

            for cp in copies(i, c, slot):
                cp.wait()

            @pl.when(valid == tm)
            def _():
                rows_block(0, tm, w1b[slot].astype(BF16), w3b[slot].astype(BF16),
                           w2b[slot].astype(BF16))

            @pl.when(valid < tm)
            def _():
                wb1_ref[...] = w1b[slot].astype(BF16)
                wb3_ref[...] = w3b[slot].astype(BF16)
                wb2_ref[...] = w2b[slot].astype(BF16)
                for s in range(tm // ts):
                    @pl.when(s * ts < valid)
                    def _():
                        rows_block(s * ts, ts, wb1_ref[...], wb3_ref[...], wb2_ref[...])

            return carry

        lax.fori_loop(0, nc, chunk, 0)

    packed = _pack_pairs(acc_ref[...])
    for c in range(SC_PLANES):
        y_ref[c] = packed[:, c * dp:(c + 1) * dp]


def _moe_ffn(xs, tile_expert, tile_valid, w1, w3, w2, layer):
    planes, ns, dp = xs.shape
    d = 2 * planes * dp
    tm, tf = TM_MOE, TF_MOE
    hbm = pl.BlockSpec(memory_space=pl.ANY)
    grid_spec = pltpu.PrefetchScalarGridSpec(
        num_scalar_prefetch=2, grid=(ns // tm,),
        in_specs=[pl.BlockSpec((planes, tm, dp), lambda i, te, tv: (0, i, 0)), hbm, hbm, hbm],
        out_specs=pl.BlockSpec((planes, tm, dp), lambda i, te, tv: (0, i, 0)),
        scratch_shapes=[pltpu.VMEM((tm, d), BF16), pltpu.VMEM((tm, d), F32),
                        pltpu.VMEM((2, d, tf), F32), pltpu.VMEM((2, d, tf), F32),
                        pltpu.VMEM((2, tf, d), F32),
                        pltpu.VMEM((d, tf), BF16), pltpu.VMEM((d, tf), BF16),
                        pltpu.VMEM((tf, d), BF16), pltpu.SemaphoreType.DMA((3, 2))])
    return pl.pallas_call(
        functools.partial(_moe_ffn_kernel, layer=layer, tf=tf), grid_spec=grid_spec,
        out_shape=jax.ShapeDtypeStruct((planes, ns, dp), jnp.int32),
        compiler_params=_cparams(("arbitrary",)), name="moe_ffn",
    )(tile_expert, tile_valid, xs, w1, w3, w2)


def _combine_kernel(x1_ref, gt_ref, y0_ref, y1_ref, o_ref):
    o_ref[...] = _gated_sum(x1_ref, gt_ref, y0_ref, y1_ref)


def _combine(x1, gates_tok, yg):
    t, d = x1.shape
    tm = 512
    nt = t // tm
    row = lambda i: (i, 0)
    return pl.pallas_call(
        _combine_kernel, grid=(nt,),
        in_specs=[pl.BlockSpec((tm, d), row), pl.BlockSpec((tm, LANES), row),
                  pl.BlockSpec((SC_PLANES, tm, d // (2 * SC_PLANES)), lambda i: (0, i, 0)),
                  pl.BlockSpec((SC_PLANES, tm, d // (2 * SC_PLANES)), lambda i: (0, i + nt, 0))],
        out_specs=pl.BlockSpec((tm, d), row),
        out_shape=jax.ShapeDtypeStruct((t, d), F32),
        compiler_params=_cparams(("parallel",)), name="moe_combine_residual",
    )(x1, gates_tok, yg, yg)


def _mla_weights(w_in, w_qb, w_kvb, g_qn, g_kn):
    d = w_in.shape[0]
    h = MLA_HEADS
    lat = Q_LORA + KV_LORA
    win = jnp.zeros((d, lat + HEAD_PAD), F32)
    win = win.at[:, :lat].set(w_in[:, :lat])
    win = win.at[:, lat + LANES:lat + LANES + QK_ROPE].set(w_in[:, lat:])
    wq = w_qb.reshape(Q_LORA, h, QK_HEAD)
    wq = jnp.pad(wq, ((0, 0), (0, 0), (0, HEAD_PAD - QK_HEAD))).reshape(Q_LORA, h * HEAD_PAD)
    wkv = w_kvb.reshape(KV_LORA, h, QK_NOPE + V_HEAD)
    wk = wkv[:, :, :QK_NOPE].reshape(KV_LORA, h * QK_NOPE)
    wvt = wkv[:, :, QK_NOPE:].reshape(KV_LORA, h * V_HEAD).T
    scale = math.log2(math.e) / math.sqrt(QK_HEAD)
    gq = jnp.broadcast_to((g_qn * scale).reshape(QK_HEAD, 1), (QK_HEAD, TQ))
    gk = jnp.pad(g_kn, (0, HEAD_PAD - QK_HEAD)).reshape(1, HEAD_PAD)
    return (win.astype(BF16), wq.T.astype(BF16), wk.astype(BF16), wvt.astype(BF16), gq, gk)


def kernel(x, positions, norm_mix, norm_ffn, mla_w_in, mla_g_qa, mla_w_qb, mla_g_kva, mla_w_kvb, mla_g_qn, mla_g_kn, mla_w_o, lru_w_in, lru_b_in, lru_conv_w, lru_conv_b, lru_w_gate_a, lru_b_gate_a, lru_w_gate_i, lru_b_gate_i, lru_lambda, lru_w_out, lru_b_out, ffn_w1, ffn_w3, ffn_w2, moe_w_router, moe_b_router, moe_w1, moe_w3, moe_w2):
    batch, seq, d = x.shape
    t = batch * seq
    depth = norm_mix.shape[0]
    n_slots = 2 * t + N_EXPERTS * TM_MOE
    n_tiles = n_slots // TM_MOE
    row = lambda v: v.reshape(1, -1)

    cs, cst = _rope_tables(positions)
    parts = (x.reshape(t, d),)
    for i in range(depth):
        j = i // 2
        if i % 2 == 0:
            w_in, w_qbt, w_kb, w_vt, g_qn, g_kn = _mla_weights(
                mla_w_in[j], mla_w_qb[j], mla_w_kvb[j], mla_g_qn[j], mla_g_kn[j])
            xr, q, k, vt = _mla_qkv(parts, cs, cst, row(norm_mix[i]), w_in, row(mla_g_qa[j]),
                                    w_qbt, row(mla_g_kva[j]), w_kb, w_vt, g_qn, g_kn)
            x1, xn = _attention(q, k, vt, xr, mla_w_o[j].astype(BF16), row(norm_ffn[i]),
                                batch, seq)
            parts = (_dense_ffn(xn, x1, ffn_w1, ffn_w3, ffn_w2, j),)
        else:
            (xr,) = parts
            x1, xn = _lru_mixer(
                xr.reshape(batch, seq, d), row(norm_mix[i]), lru_w_in[j].astype(BF16),
                row(lru_b_in[j]), lru_conv_w[j], row(lru_conv_b[j]),
                lru_w_gate_a[j].astype(BF16), lru_b_gate_a[j], lru_w_gate_i[j].astype(BF16),
                lru_b_gate_i[j], row(lru_lambda[j]), lru_w_out[j].astype(BF16),
                row(lru_b_out[j]), row(norm_ffn[i]))
            x1 = x1.reshape(t, d)
            xn = xn.reshape(t, d // 2)
            slots, gates_tok, te, tv = _router(x1, row(norm_ffn[i]), moe_w_router[j],
                                               moe_b_router[j], TM_MOE)
            xs = _sc_dispatch(xn, slots, n_slots)
            y = _moe_ffn(xs, te[0, :n_tiles], tv[0, :n_tiles], moe_w1, moe_w3, moe_w2, j)
            yg = _sc_combine(y, slots)
            parts = (x1, gates_tok, yg)
    if len(parts) == 3:
        out = _combine(*parts)
    else:
        out = parts[0]
    return out.reshape(batch, seq, d)
```

```python
import functools
import math

import numpy as np
import jax
import jax.numpy as jnp
from jax import lax
from jax.experimental import pallas as pl
from jax.experimental.pallas import tpu as pltpu
from jax.experimental.pallas import tpu_sc as plsc

F32 = jnp.float32
BF16 = jnp.bfloat16

NORM_EPS = 1e-6
CHUNK = 64
MLA_HEADS = 8
QK_NOPE = 128
QK_ROPE = 64
QK_HEAD = QK_NOPE + QK_ROPE
V_HEAD = 128
Q_LORA = 384
KV_LORA = 256
ROPE_BASE = 10000.0
LRU_HEADS = 8
CONV_WIDTH = 4
LRU_C = 8.0
N_EXPERTS = 8

LANES = 128
SUBLANES = 8
HEAD_PAD = 2 * LANES
VMEM_LIMIT = 56 * 1024 * 1024

TQ = 256
TM_QKV = 2 * TQ
TM_FFN = 1024
TF_FFN = 512
T_LRU = 64
LRU_PITCH = T_LRU + SUBLANES
TR = 512
TM_MOE = 1024
TSUB_MOE = 256
TF_MOE = 512
SC_WINDOW = 128
SC_PLANES = 2


def _cparams(sem):
    return pltpu.CompilerParams(dimension_semantics=sem, vmem_limit_bytes=VMEM_LIMIT)


def _rms(x, g):
    return x * lax.rsqrt(jnp.mean(x * x, axis=-1, keepdims=True) + NORM_EPS) * g


def _dot(a, b):
    return jnp.dot(a, b, preferred_element_type=F32)


def _dot_nt(a, b):
    return lax.dot_general(a, b, (((1,), (1,)), ((), ())), preferred_element_type=F32)


def _pack_pairs(x):
    n = x.shape[1] // 2
    bits = lax.bitcast_convert_type(x.astype(BF16).astype(F32), jnp.int32)
    return bits[:, :n] | lax.shift_right_logical(bits[:, n:], 16)


def _unpack_pairs(word):
    hi = lax.bitcast_convert_type(word & jnp.int32(-65536), F32)
    lo = lax.bitcast_convert_type(lax.shift_left(word, 16), F32)
    return hi, lo


def _unpack_planes(ref):
    halves = [_unpack_pairs(ref[c]) for c in range(SC_PLANES)]
    return [h for h, _ in halves] + [l for _, l in halves]


def _rope_kernel(pos_ref, inv_ref, cs_ref, cst_ref):
    ang = inv_ref[...] * pos_ref[...].astype(F32)
    half = QK_ROPE // 2
    ct = jnp.cos(ang)
    st = jnp.sin(ang)
    cst_ref[0:half, :] = ct[0:half, :]
    cst_ref[half:QK_ROPE, :] = st[0:half, :]
    c = ct.T
    s = st.T
    lane = lax.broadcasted_iota(jnp.int32, c.shape, 1)
    cs_ref[:, 0:LANES] = jnp.where(lane < QK_ROPE, c, 0.0)
    cs_ref[:, LANES:2 * LANES] = jnp.where(lane < half, -s, 0.0)
    cs_ref[:, 2 * LANES:3 * LANES] = jnp.where((lane >= half) & (lane < QK_ROPE), s, 0.0)


def _rope_tables(positions):
    t = positions.size
    tm = 512
    inv = 1.0 / (ROPE_BASE ** (np.arange(0, QK_ROPE, 2, dtype=np.float32) / QK_ROPE))
    inv_col = np.zeros((LANES, 1), np.float32)
    inv_col[:QK_ROPE // 2, 0] = inv
    inv_col[QK_ROPE // 2:QK_ROPE, 0] = inv
    return pl.pallas_call(
        _rope_kernel,
        grid=(t // tm,),
        in_specs=[pl.BlockSpec((1, tm), lambda i: (0, i)),
                  pl.BlockSpec((LANES, 1), lambda i: (0, 0))],
        out_specs=[pl.BlockSpec((tm, 3 * LANES), lambda i: (i, 0)),
                   pl.BlockSpec((QK_ROPE, tm), lambda i: (0, i))],
        out_shape=[jax.ShapeDtypeStruct((t, 3 * LANES), F32),
                   jax.ShapeDtypeStruct((QK_ROPE, t), F32)],
        compiler_params=_cparams(("parallel",)),
        name="rope_tables",
    )(positions.reshape(1, t), jnp.asarray(inv_col))


def _rope_rot(x, cs):
    c = cs[:, 0:LANES]
    sa = cs[:, LANES:2 * LANES]
    sb = cs[:, 2 * LANES:3 * LANES]
    return (x * c + pltpu.roll(x, LANES - QK_ROPE // 2, 1) * sa
            + pltpu.roll(x, QK_ROPE // 2, 1) * sb)


def _mla_latent(x, gmix_ref, win_ref):
    return _dot(_rms(x, gmix_ref[...]).astype(BF16), win_ref[...])


def _mla_expand(hcat, gqa_ref, wqb_ref, gkva_ref, wkb_ref, wvt_ref):
    cq = hcat[:, :Q_LORA]
    ckv = hcat[:, Q_LORA:Q_LORA + KV_LORA]
    kr = hcat[:, Q_LORA + KV_LORA + LANES:Q_LORA + KV_LORA + 2 * LANES]
    qft = _dot_nt(wqb_ref[...], _rms(cq, gqa_ref[...]).astype(BF16))
    ckvn = _rms(ckv, gkva_ref[...]).astype(BF16)
    kf = _dot(ckvn, wkb_ref[...])
    vt = _dot_nt(wvt_ref[...], ckvn).astype(BF16)
    return qft, kf, kr, vt


def _mla_q_heads(qft, cst, gqn_ref, qt_ref, s):
    half = QK_ROPE // 2
    c, sn = cst[0:half, :], cst[half:QK_ROPE, :]
    g = gqn_ref[...]
    zero = jnp.zeros((HEAD_PAD - QK_HEAD, qft.shape[1]), BF16)
    for h in range(MLA_HEADS):
        blk = qft[h * HEAD_PAD:h * HEAD_PAD + QK_HEAD, :]
        ss = jnp.sum(blk * blk, axis=0, keepdims=True)
        inv = lax.rsqrt(ss * (1.0 / QK_HEAD) + NORM_EPS)
        y = blk * g * inv
        x1, x2 = y[QK_NOPE:QK_NOPE + half, :], y[QK_NOPE + half:QK_HEAD, :]
        base = h * HEAD_PAD
        qt_ref[s, base:base + QK_NOPE, :] = y[0:QK_NOPE, :].astype(BF16)
        qt_ref[s, base + QK_NOPE:base + QK_NOPE + half, :] = (x1 * c - x2 * sn).astype(BF16)
        qt_ref[s, base + QK_NOPE + half:base + QK_HEAD, :] = (x1 * sn + x2 * c).astype(BF16)
        qt_ref[s, base + QK_HEAD:base + HEAD_PAD, :] = zero


def _mla_k_heads(kf, kr, cs, gkn_ref, k_ref, rows):
    gkn_n, gkn_r = gkn_ref[:, 0:LANES], gkn_ref[:, LANES:2 * LANES]
    kr_ss = jnp.sum(kr * kr, axis=-1, keepdims=True)
    kr_rot = _rope_rot(kr * gkn_r, cs)
    for h in range(MLA_HEADS):
        kn = kf[:, h * LANES:(h + 1) * LANES]
        ssk = jnp.sum(kn * kn, axis=-1, keepdims=True) + kr_ss
        invk = lax.rsqrt(ssk * (1.0 / QK_HEAD) + NORM_EPS)
        k_ref[rows, h * HEAD_PAD:h * HEAD_PAD + LANES] = (kn * invk * gkn_n).astype(BF16)
        k_ref[rows, h * HEAD_PAD + LANES:(h + 1) * HEAD_PAD] = (kr_rot * invk).astype(BF16)


def _mla_qkv_body(x, cs_ref, cst_ref, gmix_ref, win_ref, gqa_ref, wqb_ref, gkva_ref, wkb_ref,
                  wvt_ref, gqn_ref, gkn_ref, qt_ref, k_ref, vt_ref):
    nsub = x.shape[0] // TQ
    rows = [slice(s * TQ, (s + 1) * TQ) for s in range(nsub)]
    hcats = [_mla_latent(x[r], gmix_ref, win_ref) for r in rows]
    mids = [_mla_expand(hc, gqa_ref, wqb_ref, gkva_ref, wkb_ref, wvt_ref) for hc in hcats]
    for s, (qft, kf, kr, vt) in enumerate(mids):
        vt_ref[s] = vt
    for s, (r, (qft, kf, kr, vt)) in enumerate(zip(rows, mids)):
        _mla_q_heads(qft, cst_ref[:, r], gqn_ref, qt_ref, s)
        _mla_k_heads(kf, kr, cs_ref[r, :], gkn_ref, k_ref, r)


def _gated_sum(x1_ref, gt_ref, y0_ref, y1_ref):
    gt = gt_ref[...]
    g0, g1 = gt[:, 0:1], gt[:, 1:2]
    y = jnp.concatenate([g0 * a + g1 * b
                         for a, b in zip(_unpack_planes(y0_ref), _unpack_planes(y1_ref))], axis=1)
    return x1_ref[...] + y


def _mla_qkv_kernel(x_ref, *rest):
    _mla_qkv_body(x_ref[...], *rest)


def _mla_qkv_combine_kernel(x1_ref, gt_ref, y0_ref, y1_ref, *rest):
    *mid, xo_ref, q_ref, k_ref, v_ref = rest
    x = _gated_sum(x1_ref, gt_ref, y0_ref, y1_ref)
    xo_ref[...] = x
    _mla_qkv_body(x, *mid, q_ref, k_ref, v_ref)


def _mla_qkv(x_parts, cs, cst, gmix, w_in, g_qa, w_qbt, g_kva, w_kb, w_vt, g_qn, g_kn):
    t, d = x_parts[0].shape
    tm = TM_QKV
    nsub = tm // TQ
    nt = t // tm
    row = lambda i: (i, 0)
    const = lambda i: (0, 0)
    h = MLA_HEADS
    weights = [gmix, w_in, g_qa, w_qbt, g_kva, w_kb, w_vt, g_qn, g_kn]
    w_specs = [pl.BlockSpec(w.shape, const) for w in weights]
    qkv_specs = [pl.BlockSpec((nsub, h * HEAD_PAD, TQ), lambda i: (i, 0, 0)),
                 pl.BlockSpec((tm, h * HEAD_PAD), row),
                 pl.BlockSpec((nsub, h * V_HEAD, TQ), lambda i: (i, 0, 0))]
    qkv_shapes = [jax.ShapeDtypeStruct((t // TQ, h * HEAD_PAD, TQ), BF16),
                  jax.ShapeDtypeStruct((t, h * HEAD_PAD), BF16),
                  jax.ShapeDtypeStruct((t // TQ, h * V_HEAD, TQ), BF16)]
    cs_specs = [pl.BlockSpec((tm, 3 * LANES), row), pl.BlockSpec((QK_ROPE, tm), lambda i: (0, i))]
    if len(x_parts) == 1:
        q, k, v = pl.pallas_call(
            _mla_qkv_kernel, grid=(nt,),
            in_specs=[pl.BlockSpec((tm, d), row)] + cs_specs + w_specs,
            out_specs=qkv_specs, out_shape=qkv_shapes,
            compiler_params=_cparams(("parallel",)), name="mla_qkv",
        )(x_parts[0], cs, cst, *weights)
        return x_parts[0], q, k, v
    x1, gates_tok, yg = x_parts
    x, q, k, v = pl.pallas_call(
        _mla_qkv_combine_kernel, grid=(nt,),
        in_specs=[pl.BlockSpec((tm, d), row), pl.BlockSpec((tm, LANES), row),
                  pl.BlockSpec((SC_PLANES, tm, d // (2 * SC_PLANES)), lambda i: (0, i, 0)),
                  pl.BlockSpec((SC_PLANES, tm, d // (2 * SC_PLANES)), lambda i: (0, i + nt, 0)),
                  ] + cs_specs + w_specs,
        out_specs=[pl.BlockSpec((tm, d), row)] + qkv_specs,
        out_shape=[jax.ShapeDtypeStruct((t, d), F32)] + qkv_shapes,
        compiler_params=_cparams(("parallel",)), name="mla_qkv_combine",
    )(x1, gates_tok, yg, yg, cs, cst, *weights)
    return x, q, k, v


def _attn_kernel(q_ref, k_ref, vt_ref, x_ref, wo_ref, g_ref, x1_ref, xn_ref,
                 s_ref, m_ref, l_ref, acc_ref, o_ref):
    i = pl.program_id(1)
    m_ref[...] = jnp.full(m_ref.shape, -jnp.inf, F32)
    l_ref[...] = jnp.zeros(l_ref.shape, F32)
    acc_ref[...] = jnp.zeros(acc_ref.shape, F32)

    def scores(h, j):
        off = pl.multiple_of(j * TQ, TQ)
        hs = slice(h * HEAD_PAD, (h + 1) * HEAD_PAD)
        s_ref[h] = _dot(k_ref[pl.ds(off, TQ), hs], q_ref[0, hs, :])

    def update(h, j, masked):
        st = s_ref[h]
        if masked:
            kc = lax.broadcasted_iota(jnp.int32, (TQ, TQ), 0) // CHUNK
            qc = lax.broadcasted_iota(jnp.int32, (TQ, TQ), 1) // CHUNK
            st = jnp.where(kc <= qc, st, -jnp.inf)
        m_old = m_ref[h]
        m_new = jnp.maximum(m_old, jnp.max(st, axis=0, keepdims=True))
        p = jnp.exp2(st - m_new)
        alpha = jnp.exp2(m_old - m_new)
        l_ref[h] = alpha * l_ref[h] + jnp.sum(p, axis=0, keepdims=True)
        pv = _dot(vt_ref[j, h * V_HEAD:(h + 1) * V_HEAD, :], p.astype(BF16))
        acc_ref[h] = alpha * acc_ref[h] + pv
        m_ref[h] = m_new

    def full_tile(j, carry):
        for h in range(MLA_HEADS):
            scores(h, j)
        for h in range(MLA_HEADS):
            update(h, j, False)
        return carry

    lax.fori_loop(0, i, full_tile, 0)
    for h in range(MLA_HEADS):
        scores(h, i)
    for h in range(MLA_HEADS):
        update(h, i, True)
        o_ref[:, h * V_HEAD:(h + 1) * V_HEAD] = (acc_ref[h] / l_ref[h]).T.astype(BF16)

    x1 = x_ref[...] + _dot(o_ref[...], wo_ref[...])
    x1_ref[...] = x1
    xn_ref[...] = _rms(x1, g_ref[...]).astype(BF16)


def _attention(q, k, vt, x, w_o, g_ffn, batch, seq):
    h = MLA_HEADS
    nq = seq // TQ
    t, d = x.shape
    tile = lambda b, i: (b * nq + i, 0)
    const = lambda b, i: (0, 0)
    return pl.pallas_call(
        _attn_kernel, grid=(batch, nq),
        in_specs=[pl.BlockSpec((1, h * HEAD_PAD, TQ), lambda b, i: (b * nq + i, 0, 0)),
                  pl.BlockSpec((seq, h * HEAD_PAD), lambda b, i: (b, 0)),
                  pl.BlockSpec((nq, h * V_HEAD, TQ), lambda b, i: (b, 0, 0)),
                  pl.BlockSpec((TQ, d), tile), pl.BlockSpec(w_o.shape, const),
                  pl.BlockSpec((1, d), const)],
        out_specs=[pl.BlockSpec((TQ, d), tile), pl.BlockSpec((TQ, d), tile)],
        out_shape=[jax.ShapeDtypeStruct((t, d), F32), jax.ShapeDtypeStruct((t, d), BF16)],
        scratch_shapes=[pltpu.VMEM((h, TQ, TQ), F32),
                        pltpu.VMEM((h, 1, TQ), F32), pltpu.VMEM((h, 1, TQ), F32),
                        pltpu.VMEM((h, V_HEAD, TQ), F32),
                        pltpu.VMEM((TQ, h * V_HEAD), BF16)],
        compiler_params=_cparams(("parallel", "arbitrary")), name="attention",
    )(q, k, vt, x, w_o, g_ffn)


def _swiglu_chunk(x, w1, w3, w2):
    a = _dot(x, w1)
    b = _dot(x, w3)
    hid = (a * jax.nn.sigmoid(a)) * b
    return _dot(hid.astype(BF16), w2)


def _weight_chunk_copies(w1_hbm, w3_hbm, w2_hbm, lead, c, tf, slot, w1b, w3b, w2b, sem):
    cols = pl.ds(pl.multiple_of(c * tf, tf), tf)
    return (pltpu.make_async_copy(w1_hbm.at[lead + (slice(None), cols)], w1b.at[slot], sem.at[0, slot]),
            pltpu.make_async_copy(w3_hbm.at[lead + (slice(None), cols)], w3b.at[slot], sem.at[1, slot]),
            pltpu.make_async_copy(w2_hbm.at[lead + (cols, slice(None))], w2b.at[slot], sem.at[2, slot]))


def _ffn_kernel(xn_ref, x1_ref, w1_hbm, w3_hbm, w2_hbm, o_ref, w1b, w3b, w2b, sem, *, layer, tf):
    i = pl.program_id(0)
    n_tiles = pl.num_programs(0)
    ff = w1_hbm.shape[2]
    chunks = [(lo, min(tf, ff - lo)) for lo in range(0, ff, tf)]
    assert len(chunks) % 2 == 0

    def copies(c):
        lo, width = chunks[c]
        slot = c % 2
        cols = pl.ds(lo, width)
        return (pltpu.make_async_copy(w1_hbm.at[layer, :, cols], w1b.at[slot, :, 0:width],
                                      sem.at[0, slot]),
                pltpu.make_async_copy(w3_hbm.at[layer, :, cols], w3b.at[slot, :, 0:width],
                                      sem.at[1, slot]),
                pltpu.make_async_copy(w2_hbm.at[layer, cols, :], w2b.at[slot, 0:width, :],
                                      sem.at[2, slot]))

    @pl.when(i == 0)
    def _():
        for cp in copies(0):
            cp.start()

    o_ref[...] = x1_ref[...]
    for c, (lo, width) in enumerate(chunks):
        if c + 1 < len(chunks):
            for cp in copies(c + 1):
                cp.start()
        else:
            @pl.when(i + 1 < n_tiles)
            def _():
                for cp in copies(0):
                    cp.start()
        for cp in copies(c):
            cp.wait()
        slot = c % 2
        o_ref[...] += _swiglu_chunk(xn_ref[...], w1b[slot, :, 0:width].astype(BF16),
                                    w3b[slot, :, 0:width].astype(BF16),
                                    w2b[slot, 0:width, :].astype(BF16))


def _dense_ffn(xn, x1, w1, w3, w2, layer):
    t, d = x1.shape
    tm, tf = TM_FFN, TF_FFN
    row = lambda i: (i, 0)
    hbm = pl.BlockSpec(memory_space=pl.ANY)
    return pl.pallas_call(
        functools.partial(_ffn_kernel, layer=layer, tf=tf), grid=(t // tm,),
        in_specs=[pl.BlockSpec((tm, d), row), pl.BlockSpec((tm, d), row), hbm, hbm, hbm],
        out_specs=pl.BlockSpec((tm, d), row),
        out_shape=jax.ShapeDtypeStruct((t, d), F32),
        scratch_shapes=[pltpu.VMEM((2, d, tf), F32), pltpu.VMEM((2, d, tf), F32),
                        pltpu.VMEM((2, tf, d), F32), pltpu.SemaphoreType.DMA((3, 2))],
        compiler_params=_cparams(("arbitrary",)), name="dense_ffn",
    )(xn, x1, w1, w3, w2)


def _lru_kernel(x_ref, gmix_ref, win_ref, bin_ref, cw_ref, cb_ref, wga_ref, bga_ref,
                wgi_ref, bgi_ref, lam_ref, wout_ref, bout_ref, gffn_ref,
                x1_ref, xn_ref,
                xpad_ref, a_ref, u_ref, gate_ref, yg_ref, h_ref):
    nb, tt, d = x_ref.shape
    w = lam_ref.shape[1]
    m = nb * tt

    @pl.when(pl.program_id(0) == 0)
    def _():
        h_ref[...] = jnp.zeros_like(h_ref)
        xpad_ref[:, 0:SUBLANES, :] = jnp.zeros((nb, SUBLANES, w), F32)

    x = x_ref[...].reshape(m, d)
    xn = _rms(x, gmix_ref[...]).astype(BF16)
    hcat = _dot(xn, win_ref[...]) + bin_ref[...]
    gate_ref[...] = jax.nn.gelu(hcat[:, :w], approximate=True)
    xpad_ref[:, SUBLANES:, :] = hcat[:, w:].reshape(nb, tt, w)

    xc = cb_ref[...].reshape(1, 1, w) + jnp.zeros((nb, tt, w), F32)
    for j in range(CONV_WIDTH):
        lo = SUBLANES - (CONV_WIDTH - 1) + j
        xc = xc + xpad_ref[:, lo:lo + tt, :] * cw_ref[j:j + 1, :].reshape(1, 1, w)
    xpad_ref[:, 0:SUBLANES, :] = xpad_ref[:, tt:tt + SUBLANES, :]
    xc = xc.reshape(m, w)

    lam = lam_ref[...]
    log_sig = jnp.minimum(lam, 0.0) - jnp.log1p(jnp.exp(-jnp.abs(lam)))
    for hh in range(LRU_HEADS):
        sl = slice(hh * LANES, (hh + 1) * LANES)
        xh = xc[:, sl]
        xhb = xh.astype(BF16)
        r = jax.nn.sigmoid(_dot(xhb, wga_ref[hh]) + bga_ref[hh:hh + 1, :])
        ig = jax.nn.sigmoid(_dot(xhb, wgi_ref[hh]) + bgi_ref[hh:hh + 1, :])
        log_a = LRU_C * r * log_sig[:, sl]
        a = jnp.exp(log_a)
        z = 1.0 - a * a
        u = (z * lax.rsqrt(jnp.maximum(z, 1e-30))) * (ig * xh)
        for b in range(nb):
            a_ref[hh, b * LRU_PITCH:b * LRU_PITCH + tt, :] = a[b * tt:(b + 1) * tt, :]
            u_ref[hh, b * LRU_PITCH:b * LRU_PITCH + tt, :] = u[b * tt:(b + 1) * tt, :]

    def scan_step(t, hs):
        new = []
        for hh in range(LRU_HEADS):
            rows = pl.ds(t, nb, stride=LRU_PITCH)
            hv = a_ref[hh, rows, :] * hs[hh] + u_ref[hh, rows, :]
            u_ref[hh, rows, :] = hv
            new.append(hv)
        return tuple(new)

    hs = lax.fori_loop(0, tt, scan_step, tuple(h_ref[hh] for hh in range(LRU_HEADS)), unroll=4)
    for hh in range(LRU_HEADS):
        h_ref[hh] = hs[hh]

    for hh in range(LRU_HEADS):
        sl = slice(hh * LANES, (hh + 1) * LANES)
        for b in range(nb):
            y = u_ref[hh, b * LRU_PITCH:b * LRU_PITCH + tt, :]
            yg_ref[b * tt:(b + 1) * tt, sl] = (y * gate_ref[b * tt:(b + 1) * tt, sl]).astype(BF16)

    x1 = x + _dot(yg_ref[...], wout_ref[...]) + bout_ref[...]
    x1_ref[...] = x1.reshape(nb, tt, d)
    xn_ref[...] = _pack_pairs(_rms(x1, gffn_ref[...])).reshape(nb, tt, d // 2)


def _lru_mixer(x3, gmix, w_in, b_in, conv_w, conv_b, wga, bga, wgi, bgi, lam, w_out, b_out, g_ffn):
    nb, seq, d = x3.shape
    w = lam.shape[1]
    tt = T_LRU
    m = nb * tt
    consts = [gmix, w_in, b_in, conv_w, conv_b, wga, bga, wgi, bgi, lam, w_out, b_out, g_ffn]
    const_specs = [pl.BlockSpec(c.shape, (lambda i, n=c.ndim: (0,) * n)) for c in consts]
    blk = pl.BlockSpec((nb, tt, d), lambda i: (0, i, 0))
    return pl.pallas_call(
        _lru_kernel, grid=(seq // tt,),
        in_specs=[blk] + const_specs,
        out_specs=[blk, pl.BlockSpec((nb, tt, d // 2), lambda i: (0, i, 0))],
        out_shape=[jax.ShapeDtypeStruct((nb, seq, d), F32),
                   jax.ShapeDtypeStruct((nb, seq, d // 2), jnp.int32)],
        scratch_shapes=[pltpu.VMEM((nb, tt + SUBLANES, w), F32),
                        pltpu.VMEM((LRU_HEADS, nb * LRU_PITCH, LANES), F32),
                        pltpu.VMEM((LRU_HEADS, nb * LRU_PITCH, LANES), F32),
                        pltpu.VMEM((m, w), F32),
                        pltpu.VMEM((m, w), BF16),
                        pltpu.VMEM((LRU_HEADS, nb, LANES), F32)],
        compiler_params=_cparams(("arbitrary",)), name="rglru_mixer",
    )(x3, *consts)


def _router_kernel(x1_ref, g_ref, wh_ref, wl_ref, br_ref, slot_ref, gate_ref, te_ref, tv_ref,
                   lg_ref, tri_ref, cnt_ref, run_ref, start_ref, *, tile):
    phase = pl.program_id(0)
    j = pl.program_id(1)
    ne = N_EXPERTS
    tr = x1_ref.shape[0]

    @pl.when((phase == 0) & (j == 0))
    def _():
        cnt_ref[...] = jnp.zeros_like(cnt_ref)
        tri_ref[...] = jnp.where(lax.broadcasted_iota(jnp.int32, (tr, tr), 0)
                                 < lax.broadcasted_iota(jnp.int32, (tr, tr), 1),
                                 1.0, 0.0).astype(BF16)

    @pl.when(phase == 0)
    def _():
        x = _rms(x1_ref[...], g_ref[...])
        xh = x.astype(BF16)
        xl = (x - xh.astype(F32)).astype(BF16)
        wh = wh_ref[...]
        nat = _dot(xh, wh) + _dot(xl, wh) + _dot(xh, wl_ref[...])
        lg_ref[j] = nat.T[0:ne, :] + br_ref[...]

    logits = lg_ref[j]
    eidx = lax.broadcasted_iota(jnp.int32, (ne, tr), 0)
    m1 = jnp.max(logits, axis=0, keepdims=True)
    i1 = jnp.min(jnp.where(logits == m1, eidx, ne), axis=0, keepdims=True)
    oh1 = eidx == i1
    rest = jnp.where(oh1, -jnp.inf, logits)
    m2 = jnp.max(rest, axis=0, keepdims=True)
    i2 = jnp.min(jnp.where(rest == m2, eidx, ne), axis=0, keepdims=True)
    oh2 = eidx == i2
    oh = jnp.where(oh1 | oh2, 1.0, 0.0)
    tile_cnt = jnp.sum(oh, axis=1, keepdims=True)

    @pl.when(phase == 0)
    def _():
        cnt_ref[...] += tile_cnt

    @pl.when((phase == 1) & (j == 0))
    def _():
        cnt = cnt_ref[...]
        padded = jnp.ceil(cnt * (1.0 / tile)) * tile
        sub = lax.broadcasted_iota(jnp.int32, (ne, 1), 0)
        start = jnp.zeros((ne, 1), F32)
        for e in range(ne - 1):
            start = start + jnp.where(sub > e, padded[e:e + 1, :], 0.0)
        start_ref[...] = start
        run_ref[...] = jnp.zeros_like(run_ref)
        tile_start = lax.broadcasted_iota(jnp.int32, (ne, LANES), 1).astype(F32) * tile
        owner = jnp.sum(jnp.where(tile_start >= start + padded, 1, 0), axis=0, keepdims=True)
        owner = jnp.minimum(owner, ne - 1)
        esub = lax.broadcasted_iota(jnp.int32, (ne, LANES), 0)
        real = jnp.clip(cnt - (tile_start - start), 0.0, float(tile))
        te_ref[...] = owner
        tv_ref[...] = jnp.sum(jnp.where(esub == owner, real, 0.0), axis=0,
                              keepdims=True).astype(jnp.int32)

    @pl.when(phase == 1)
    def _():
        before = _dot(oh.astype(BF16), tri_ref[...])
        slot_e = start_ref[...] + run_ref[...] + before
        s1 = jnp.sum(jnp.where(oh1, slot_e, 0.0), axis=0, keepdims=True)
        s2 = jnp.sum(jnp.where(oh2, slot_e, 0.0), axis=0, keepdims=True)
        slot_ref[0:1, :] = s1.astype(jnp.int32)
        slot_ref[1:2, :] = s2.astype(jnp.int32)
        e21 = jnp.exp(m2 - m1)
        g1 = 1.0 / (1.0 + e21)
        g2 = e21 * g1
        rowi = lax.broadcasted_iota(jnp.int32, (LANES, tr), 0)
        gmat = jnp.where(rowi == 0, g1, jnp.where(rowi == 1, g2, 0.0))
        gate_ref[...] = gmat.T
        run_ref[...] += tile_cnt


def _router(x1, g_ffn, w_router, b_router, tile):
    t, d = x1.shape
    tr = TR
    nt = t // tr
    wp = jnp.pad(w_router, ((0, 0), (0, LANES - N_EXPERTS)))
    wh = wp.astype(BF16)
    wl = (wp - wh.astype(F32)).astype(BF16)
    const = lambda p, j: (0, 0)
    return pl.pallas_call(
        functools.partial(_router_kernel, tile=tile), grid=(2, nt),
        in_specs=[pl.BlockSpec((tr, d), lambda p, j: (j * (1 - p) + (nt - 1) * p, 0)),
                  pl.BlockSpec((1, d), const),
                  pl.BlockSpec(wh.shape, const), pl.BlockSpec(wl.shape, const),
                  pl.BlockSpec((N_EXPERTS, 1), const)],
        out_specs=[pl.BlockSpec((2, tr), lambda p, j: (0, j * p)),
                   pl.BlockSpec((tr, LANES), lambda p, j: (j * p, 0)),
                   pl.BlockSpec((1, LANES), const), pl.BlockSpec((1, LANES), const)],
        out_shape=[jax.ShapeDtypeStruct((2, t), jnp.int32),
                   jax.ShapeDtypeStruct((t, LANES), F32),
                   jax.ShapeDtypeStruct((1, LANES), jnp.int32),
                   jax.ShapeDtypeStruct((1, LANES), jnp.int32)],
        scratch_shapes=[pltpu.VMEM((nt, N_EXPERTS, tr), F32), pltpu.VMEM((tr, tr), BF16),
                        pltpu.VMEM((N_EXPERTS, 1), F32), pltpu.VMEM((N_EXPERTS, 1), F32),
                        pltpu.VMEM((N_EXPERTS, 1), F32)],
        compiler_params=_cparams(("arbitrary", "arbitrary")), name="moe_router",
    )(x1, g_ffn, wh, wl, b_router.reshape(N_EXPERTS, 1))


def _sc_mesh():
    return plsc.VectorSubcoreMesh(core_axis_name="c", subcore_axis_name="s")


def _sc_dispatch(xn, slots, n_slots):
    t, d = xn.shape
    win = SC_WINDOW
    nwin = t // win
    dp = d // SC_PLANES
    idx = slots.reshape(1, 2 * t)

    @functools.partial(pl.kernel,
                       out_type=jax.ShapeDtypeStruct((SC_PLANES, n_slots, dp), xn.dtype),
                       mesh=_sc_mesh(), scratch_types=[], name="moe_dispatch")
    def run(x_hbm, i_hbm, o_hbm):
        for c in range(SC_PLANES):
            def body(x_vmem, i0_vmem, i1_vmem, c=c):
                pltpu.sync_copy(x_vmem, o_hbm.at[c].at[i0_vmem.at[0]])
                pltpu.sync_copy(x_vmem, o_hbm.at[c].at[i1_vmem.at[0]])

            pltpu.emit_pipeline(
                body, grid=(nwin,),
                in_specs=[pl.BlockSpec((win, dp), lambda i, c=c: (i, c)),
                          pl.BlockSpec((1, win), lambda i: (0, i)),
                          pl.BlockSpec((1, win), lambda i: (0, i + nwin))],
                out_specs=[],
                core_axis_name=("c", "s"),
                dimension_semantics=(pltpu.PARALLEL,),
            )(x_hbm, i_hbm, i_hbm)

    return run(xn, idx)


def _sc_combine(y, slots):
    n2 = slots.size
    dp = y.shape[2]
    win = SC_WINDOW
    idx = slots.reshape(1, n2)

    @functools.partial(pl.kernel, out_type=jax.ShapeDtypeStruct((SC_PLANES, n2, dp), y.dtype),
                       mesh=_sc_mesh(), scratch_types=[], name="moe_combine")
    def run(y_hbm, i_hbm, o_hbm):
        for c in range(SC_PLANES):
            def body(i_vmem, o_vmem, c=c):
                pltpu.sync_copy(y_hbm.at[c].at[i_vmem.at[0]], o_vmem)

            pltpu.emit_pipeline(
                body, grid=(n2 // win,),
                in_specs=[pl.BlockSpec((1, win), lambda i: (0, i))],
                out_specs=[pl.BlockSpec((win, dp), lambda i: (i, 0))],
                core_axis_name=("c", "s"),
                dimension_semantics=(pltpu.PARALLEL,),
            )(i_hbm, o_hbm.at[c])

    return run(y, idx)


def _moe_ffn_kernel(te_ref, tv_ref, xs_ref, w1_hbm, w3_hbm, w2_hbm, y_ref,
                    xb_ref, acc_ref, w1b, w3b, w2b, wb1_ref, wb3_ref, wb2_ref, sem, *, layer, tf):
    i = pl.program_id(0)
    n_tiles = pl.num_programs(0)
    valid = tv_ref[i]
    tm, dp = xs_ref.shape[1:]
    ts = TSUB_MOE
    nc = w1_hbm.shape[3] // tf

    def copies(tile, c, slot):
        return _weight_chunk_copies(w1_hbm, w3_hbm, w2_hbm, (layer, te_ref[tile]), c, tf, slot,
                                    w1b, w3b, w2b, sem)

    @pl.when((i == 0) & (valid > 0))
    def _():
        for cp in copies(0, 0, 0):
            cp.start()

    rows = lax.broadcasted_iota(jnp.int32, (tm, 1), 0)
    for c, piece in enumerate(_unpack_planes(xs_ref)):
        xb_ref[:, c * dp:(c + 1) * dp] = jnp.where(rows < valid, piece, 0.0).astype(BF16)
    acc_ref[...] = jnp.zeros_like(acc_ref)

    def rows_block(lo, n, w1, w3, w2):
        acc_ref[lo:lo + n, :] += _swiglu_chunk(xb_ref[lo:lo + n, :], w1, w3, w2)

    @pl.when(valid > 0)
    def _():
        nxt_tile = jnp.minimum(i + 1, n_tiles - 1)
        next_tile_live = (i + 1 < n_tiles) & (tv_ref[nxt_tile] > 0)

        def chunk(c, carry):
            slot = lax.rem(i * nc + c, 2)
            last = c + 1 == nc

            @pl.when(jnp.logical_not(last) | next_tile_live)
            def _():
                for cp in copies(jnp.where(last, nxt_tile, i), jnp.where(last, 0, c + 1), 1 - slot):
                    cp.start()

            for cp in copies(i, c, slot):
                cp.wait()

            @pl.when(valid == tm)
            def _():
                rows_block(0, tm, w1b[slot].astype(BF16), w3b[slot].astype(BF16),
                           w2b[slot].astype(BF16))

            @pl.when(valid < tm)
            def _():
                wb1_ref[...] = w1b[slot].astype(BF16)
                wb3_ref[...] = w3b[slot].astype(BF16)
                wb2_ref[...] = w2b[slot].astype(BF16)
                for s in range(tm // ts):
                    @pl.when(s * ts < valid)
                    def _():
                        rows_block(s * ts, ts, wb1_ref[...], wb3_ref[...], wb2_ref[...])

            return carry

        lax.fori_loop(0, nc, chunk, 0)

    packed = _pack_pairs(acc_ref[...])
    for c in range(SC_PLANES):
        y_ref[c] = packed[:, c * dp:(c + 1) * dp]


def _moe_ffn(xs, tile_expert, tile_valid, w1, w3, w2, layer):
    planes, ns, dp = xs.shape
    d = 2 * planes * dp
    tm, tf = TM_MOE, TF_MOE
    hbm = pl.BlockSpec(memory_space=pl.ANY)
    grid_spec = pltpu.PrefetchScalarGridSpec(
        num_scalar_prefetch=2, grid=(ns // tm,),
        in_specs=[pl.BlockSpec((planes, tm, dp), lambda i, te, tv: (0, i, 0)), hbm, hbm, hbm],
        out_specs=pl.BlockSpec((planes, tm, dp), lambda i, te, tv: (0, i, 0)),
        scratch_shapes=[pltpu.VMEM((tm, d), BF16), pltpu.VMEM((tm, d), F32),
                        pltpu.VMEM((2, d, tf), F32), pltpu.VMEM((2, d, tf), F32),
                        pltpu.VMEM((2, tf, d), F32),
                        pltpu.VMEM((d, tf), BF16), pltpu.VMEM((d, tf), BF16),
                        pltpu.VMEM((tf, d), BF16), pltpu.SemaphoreType.DMA((3, 2))])
    return pl.pallas_call(
        functools.partial(_moe_ffn_kernel, layer=layer, tf=tf), grid_spec=grid_spec,
        out_shape=jax.ShapeDtypeStruct((planes, ns, dp), jnp.int32),
        compiler_params=_cparams(("arbitrary",)), name="moe_ffn",
    )(tile_expert, tile_valid, xs, w1, w3, w2)


def _combine_kernel(x1_ref, gt_ref, y0_ref, y1_ref, o_ref):
    o_ref[...] = _gated_sum(x1_ref, gt_ref, y0_ref, y1_ref)


def _combine(x1, gates_tok, yg):
    t, d = x1.shape
    tm = 512
    nt = t // tm
    row = lambda i: (i, 0)
    return pl.pallas_call(
        _combine_kernel, grid=(nt,),
        in_specs=[pl.BlockSpec((tm, d), row), pl.BlockSpec((tm, LANES), row),
                  pl.BlockSpec((SC_PLANES, tm, d // (2 * SC_PLANES)), lambda i: (0, i, 0)),
                  pl.BlockSpec((SC_PLANES, tm, d // (2 * SC_PLANES)), lambda i: (0, i + nt, 0))],
        out_specs=pl.BlockSpec((tm, d), row),
        out_shape=jax.ShapeDtypeStruct((t, d), F32),
        compiler_params=_cparams(("parallel",)), name="moe_combine_residual",
    )(x1, gates_tok, yg, yg)


def _mla_weights(w_in, w_qb, w_kvb, g_qn, g_kn):
    d = w_in.shape[0]
    h = MLA_HEADS
    lat = Q_LORA + KV_LORA
    win = jnp.zeros((d, lat + HEAD_PAD), F32)
    win = win.at[:, :lat].set(w_in[:, :lat])
    win = win.at[:, lat + LANES:lat + LANES + QK_ROPE].set(w_in[:, lat:])
    wq = w_qb.reshape(Q_LORA, h, QK_HEAD)
    wq = jnp.pad(wq, ((0, 0), (0, 0), (0, HEAD_PAD - QK_HEAD))).reshape(Q_LORA, h * HEAD_PAD)
    wkv = w_kvb.reshape(KV_LORA, h, QK_NOPE + V_HEAD)
    wk = wkv[:, :, :QK_NOPE].reshape(KV_LORA, h * QK_NOPE)
    wvt = wkv[:, :, QK_NOPE:].reshape(KV_LORA, h * V_HEAD).T
    scale = math.log2(math.e) / math.sqrt(QK_HEAD)
    gq = jnp.broadcast_to((g_qn * scale).reshape(QK_HEAD, 1), (QK_HEAD, TQ))
    gk = jnp.pad(g_kn, (0, HEAD_PAD - QK_HEAD)).reshape(1, HEAD_PAD)
    return (win.astype(BF16), wq.T.astype(BF16), wk.astype(BF16), wvt.astype(BF16), gq, gk)


def kernel(x, positions, norm_mix, norm_ffn, mla_w_in, mla_g_qa, mla_w_qb, mla_g_kva, mla_w_kvb, mla_g_qn, mla_g_kn, mla_w_o, lru_w_in, lru_b_in, lru_conv_w, lru_conv_b, lru_w_gate_a, lru_b_gate_a, lru_w_gate_i, lru_b_gate_i, lru_lambda, lru_w_out, lru_b_out, ffn_w1, ffn_w3, ffn_w2, moe_w_router, moe_b_router, moe_w1, moe_w3, moe_w2):
    batch, seq, d = x.shape
    t = batch * seq
    depth = norm_mix.shape[0]
    n_slots = 2 * t + N_EXPERTS * TM_MOE
    n_tiles = n_slots // TM_MOE
    row = lambda v: v.reshape(1, -1)

    cs, cst = _rope_tables(positions)
    parts = (x.reshape(t, d),)
    for i in range(depth):
        j = i // 2
        if i % 2 == 0:
            w_in, w_qbt, w_kb, w_vt, g_qn, g_kn = _mla_weights(
                mla_w_in[j], mla_w_qb[j], mla_w_kvb[j], mla_g_qn[j], mla_g_kn[j])
            xr, q, k, vt = _mla_qkv(parts, cs, cst, row(norm_mix[i]), w_in, row(mla_g_qa[j]),
                                    w_qbt, row(mla_g_kva[j]), w_kb, w_vt, g_qn, g_kn)
            x1, xn = _attention(q, k, vt, xr, mla_w_o[j].astype(BF16), row(norm_ffn[i]),
                                batch, seq)
            parts = (_dense_ffn(xn, x1, ffn_w1, ffn_w3, ffn_w2, j),)
        else:
            (xr,) = parts
            x1, xn = _lru_mixer(
                xr.reshape(batch, seq, d), row(norm_mix[i]), lru_w_in[j].astype(BF16),
                row(lru_b_in[j]), lru_conv_w[j], row(lru_conv_b[j]),
                lru_w_gate_a[j].astype(BF16), lru_b_gate_a[j], lru_w_gate_i[j].astype(BF16),
                lru_b_gate_i[j], row(lru_lambda[j]), lru_w_out[j].astype(BF16),
                row(lru_b_out[j]), row(norm_ffn[i]))
            x1 = x1.reshape(t, d)
            xn = xn.reshape(t, d // 2)
            slots, gates_tok, te, tv = _router(x1, row(norm_ffn[i]), moe_w_router[j],
                                               moe_b_router[j], TM_MOE)
            xs = _sc_dispatch(xn, slots, n_slots)
            y = _moe_ffn(xs, te[0, :n_tiles], tv[0, :n_tiles], moe_w1, moe_w3, moe_w2, j)
            yg = _sc_combine(y, slots)
            parts = (x1, gates_tok, yg)
    if len(parts) == 3:
        out = _combine(*parts)
    else:
        out = parts[0]
    return out.reshape(batch, seq, d)
```

```python
import functools
import math

import numpy as np
import jax
import jax.numpy as jnp
from jax import lax
from jax.experimental import pallas as pl
from jax.experimental.pallas import tpu as pltpu
from jax.experimental.pallas import tpu_sc as plsc

F32 = jnp.float32
BF16 = jnp.bfloat16

NORM_EPS = 1e-6
CHUNK = 64
MLA_HEADS = 8
QK_NOPE = 128
QK_ROPE = 64
QK_HEAD = QK_NOPE + QK_ROPE
V_HEAD = 128
Q_LORA = 384
KV_LORA = 256
ROPE_BASE = 10000.0
LRU_HEADS = 8
CONV_WIDTH = 4
LRU_C = 8.0
N_EXPERTS = 8

LANES = 128
SUBLANES = 8
HEAD_PAD = 2 * LANES
VMEM_LIMIT = 56 * 1024 * 1024

TQ = 256
TM_QKV = 2 * TQ
TM_FFN = 2048
TSUB_FFN = 1024
TF_FFN = 512
T_LRU = 64
LRU_PITCH = T_LRU + SUBLANES
TR = 512
TM_MOE = 1024
TSUB_MOE = 256
TF_MOE = 512
SC_WINDOW = 128
SC_PLANES = 2


def _cparams(sem):
    return pltpu.CompilerParams(dimension_semantics=sem, vmem_limit_bytes=VMEM_LIMIT)


def _rms(x, g):
    return x * lax.rsqrt(jnp.mean(x * x, axis=-1, keepdims=True) + NORM_EPS) * g


def _dot(a, b):
    return jnp.dot(a, b, preferred_element_type=F32)


def _dot_nt(a, b):
    return lax.dot_general(a, b, (((1,), (1,)), ((), ())), preferred_element_type=F32)


def _pack_pairs(x):
    n = x.shape[1] // 2
    bits = lax.bitcast_convert_type(x.astype(BF16).astype(F32), jnp.int32)
    return bits[:, :n] | lax.shift_right_logical(bits[:, n:], 16)


def _unpack_pairs(word):
    hi = lax.bitcast_convert_type(word & jnp.int32(-65536), F32)
    lo = lax.bitcast_convert_type(lax.shift_left(word, 16), F32)
    return hi, lo


def _unpack_planes(ref):
    halves = [_unpack_pairs(ref[c]) for c in range(SC_PLANES)]
    return [h for h, _ in halves] + [l for _, l in halves]


def _rope_kernel(pos_ref, inv_ref, cs_ref, cst_ref):
    ang = inv_ref[...] * pos_ref[...].astype(F32)
    half = QK_ROPE // 2
    ct = jnp.cos(ang)
    st = jnp.sin(ang)
    cst_ref[0:half, :] = ct[0:half, :]
    cst_ref[half:QK_ROPE, :] = st[0:half, :]
    c = ct.T
    s = st.T
    lane = lax.broadcasted_iota(jnp.int32, c.shape, 1)
    cs_ref[:, 0:LANES] = jnp.where(lane < QK_ROPE, c, 0.0)
    cs_ref[:, LANES:2 * LANES] = jnp.where(lane < half, -s, 0.0)
    cs_ref[:, 2 * LANES:3 * LANES] = jnp.where((lane >= half) & (lane < QK_ROPE), s, 0.0)


def _rope_tables(positions):
    t = positions.size
    tm = 512
    inv = 1.0 / (ROPE_BASE ** (np.arange(0, QK_ROPE, 2, dtype=np.float32) / QK_ROPE))
    inv_col = np.zeros((LANES, 1), np.float32)
    inv_col[:QK_ROPE // 2, 0] = inv
    inv_col[QK_ROPE // 2:QK_ROPE, 0] = inv
    return pl.pallas_call(
        _rope_kernel,
        grid=(t // tm,),
        in_specs=[pl.BlockSpec((1, tm), lambda i: (0, i)),
                  pl.BlockSpec((LANES, 1), lambda i: (0, 0))],
        out_specs=[pl.BlockSpec((tm, 3 * LANES), lambda i: (i, 0)),
                   pl.BlockSpec((QK_ROPE, tm), lambda i: (0, i))],
        out_shape=[jax.ShapeDtypeStruct((t, 3 * LANES), F32),
                   jax.ShapeDtypeStruct((QK_ROPE, t), F32)],
        compiler_params=_cparams(("parallel",)),
        name="rope_tables",
    )(positions.reshape(1, t), jnp.asarray(inv_col))


def _rope_rot(x, cs):
    c = cs[:, 0:LANES]
    sa = cs[:, LANES:2 * LANES]
    sb = cs[:, 2 * LANES:3 * LANES]
    return (x * c + pltpu.roll(x, LANES - QK_ROPE // 2, 1) * sa
            + pltpu.roll(x, QK_ROPE // 2, 1) * sb)


def _mla_latent(x, gmix_ref, win_ref):
    return _dot(_rms(x, gmix_ref[...]).astype(BF16), win_ref[...])


def _mla_expand(hcat, gqa_ref, wqb_ref, gkva_ref, wkb_ref, wvt_ref):
    cq = hcat[:, :Q_LORA]
    ckv = hcat[:, Q_LORA:Q_LORA + KV_LORA]
    kr = hcat[:, Q_LORA + KV_LORA + LANES:Q_LORA + KV_LORA + 2 * LANES]
    qft = _dot_nt(wqb_ref[...], _rms(cq, gqa_ref[...]).astype(BF16))
    ckvn = _rms(ckv, gkva_ref[...]).astype(BF16)
    kf = _dot(ckvn, wkb_ref[...])
    vt = _dot_nt(wvt_ref[...], ckvn).astype(BF16)
    return qft, kf, kr, vt


def _mla_q_heads(qft, cst, gqn_ref, qt_ref, s):
    half = QK_ROPE // 2
    c, sn = cst[0:half, :], cst[half:QK_ROPE, :]
    g = gqn_ref[...]
    zero = jnp.zeros((HEAD_PAD - QK_HEAD, qft.shape[1]), BF16)
    for h in range(MLA_HEADS):
        blk = qft[h * HEAD_PAD:h * HEAD_PAD + QK_HEAD, :]
        ss = jnp.sum(blk * blk, axis=0, keepdims=True)
        inv = lax.rsqrt(ss * (1.0 / QK_HEAD) + NORM_EPS)
        y = blk * g * inv
        x1, x2 = y[QK_NOPE:QK_NOPE + half, :], y[QK_NOPE + half:QK_HEAD, :]
        base = h * HEAD_PAD
        qt_ref[s, base:base + QK_NOPE, :] = y[0:QK_NOPE, :].astype(BF16)
        qt_ref[s, base + QK_NOPE:base + QK_NOPE + half, :] = (x1 * c - x2 * sn).astype(BF16)
        qt_ref[s, base + QK_NOPE + half:base + QK_HEAD, :] = (x1 * sn + x2 * c).astype(BF16)
        qt_ref[s, base + QK_HEAD:base + HEAD_PAD, :] = zero


def _mla_k_heads(kf, kr, cs, gkn_ref, k_ref, rows):
    gkn_n, gkn_r = gkn_ref[:, 0:LANES], gkn_ref[:, LANES:2 * LANES]
    kr_ss = jnp.sum(kr * kr, axis=-1, keepdims=True)
    kr_rot = _rope_rot(kr * gkn_r, cs)
    for h in range(MLA_HEADS):
        kn = kf[:, h * LANES:(h + 1) * LANES]
        ssk = jnp.sum(kn * kn, axis=-1, keepdims=True) + kr_ss
        invk = lax.rsqrt(ssk * (1.0 / QK_HEAD) + NORM_EPS)
        k_ref[rows, h * HEAD_PAD:h * HEAD_PAD + LANES] = (kn * invk * gkn_n).astype(BF16)
        k_ref[rows, h * HEAD_PAD + LANES:(h + 1) * HEAD_PAD] = (kr_rot * invk).astype(BF16)


def _mla_qkv_body(x, cs_ref, cst_ref, gmix_ref, win_ref, gqa_ref, wqb_ref, gkva_ref, wkb_ref,
                  wvt_ref, gqn_ref, gkn_ref, qt_ref, k_ref, vt_ref):
    nsub = x.shape[0] // TQ
    rows = [slice(s * TQ, (s + 1) * TQ) for s in range(nsub)]
    hcats = [_mla_latent(x[r], gmix_ref, win_ref) for r in rows]
    mids = [_mla_expand(hc, gqa_ref, wqb_ref, gkva_ref, wkb_ref, wvt_ref) for hc in hcats]
    for s, (qft, kf, kr, vt) in enumerate(mids):
        vt_ref[s] = vt
    for s, (r, (qft, kf, kr, vt)) in enumerate(zip(rows, mids)):
        _mla_q_heads(qft, cst_ref[:, r], gqn_ref, qt_ref, s)
        _mla_k_heads(kf, kr, cs_ref[r, :], gkn_ref, k_ref, r)


def _gated_sum(x1_ref, gt_ref, y0_ref, y1_ref):
    gt = gt_ref[...]
    g0, g1 = gt[:, 0:1], gt[:, 1:2]
    y = jnp.concatenate([g0 * a + g1 * b
                         for a, b in zip(_unpack_planes(y0_ref), _unpack_planes(y1_ref))], axis=1)
    return x1_ref[...] + y


def _mla_qkv_kernel(x_ref, *rest):
    _mla_qkv_body(x_ref[...], *rest)


def _mla_qkv_combine_kernel(x1_ref, gt_ref, y0_ref, y1_ref, *rest):
    *mid, xo_ref, q_ref, k_ref, v_ref = rest
    x = _gated_sum(x1_ref, gt_ref, y0_ref, y1_ref)
    xo_ref[...] = x
    _mla_qkv_body(x, *mid, q_ref, k_ref, v_ref)


def _mla_qkv(x_parts, cs, cst, gmix, w_in, g_qa, w_qbt, g_kva, w_kb, w_vt, g_qn, g_kn):
    t, d = x_parts[0].shape
    tm = TM_QKV
    nsub = tm // TQ
    nt = t // tm
    row = lambda i: (i, 0)
    const = lambda i: (0, 0)
    h = MLA_HEADS
    weights = [gmix, w_in, g_qa, w_qbt, g_kva, w_kb, w_vt, g_qn, g_kn]
    w_specs = [pl.BlockSpec(w.shape, const) for w in weights]
    qkv_specs = [pl.BlockSpec((nsub, h * HEAD_PAD, TQ), lambda i: (i, 0, 0)),
                 pl.BlockSpec((tm, h * HEAD_PAD), row),
                 pl.BlockSpec((nsub, h * V_HEAD, TQ), lambda i: (i, 0, 0))]
    qkv_shapes = [jax.ShapeDtypeStruct((t // TQ, h * HEAD_PAD, TQ), BF16),
                  jax.ShapeDtypeStruct((t, h * HEAD_PAD), BF16),
                  jax.ShapeDtypeStruct((t // TQ, h * V_HEAD, TQ), BF16)]
    cs_specs = [pl.BlockSpec((tm, 3 * LANES), row), pl.BlockSpec((QK_ROPE, tm), lambda i: (0, i))]
    if len(x_parts) == 1:
        q, k, v = pl.pallas_call(
            _mla_qkv_kernel, grid=(nt,),
            in_specs=[pl.BlockSpec((tm, d), row)] + cs_specs + w_specs,
            out_specs=qkv_specs, out_shape=qkv_shapes,
            compiler_params=_cparams(("parallel",)), name="mla_qkv",
        )(x_parts[0], cs, cst, *weights)
        return x_parts[0], q, k, v
    x1, gates_tok, yg = x_parts
    x, q, k, v = pl.pallas_call(
        _mla_qkv_combine_kernel, grid=(nt,),
        in_specs=[pl.BlockSpec((tm, d), row), pl.BlockSpec((tm, LANES), row),
                  pl.BlockSpec((SC_PLANES, tm, d // (2 * SC_PLANES)), lambda i: (0, i, 0)),
                  pl.BlockSpec((SC_PLANES, tm, d // (2 * SC_PLANES)), lambda i: (0, i + nt, 0)),
                  ] + cs_specs + w_specs,
        out_specs=[pl.BlockSpec((tm, d), row)] + qkv_specs,
        out_shape=[jax.ShapeDtypeStruct((t, d), F32)] + qkv_shapes,
        compiler_params=_cparams(("parallel",)), name="mla_qkv_combine",
    )(x1, gates_tok, yg, yg, cs, cst, *weights)
    return x, q, k, v


def _attn_kernel(q_ref, k_ref, vt_ref, x_ref, wo_ref, g_ref, x1_ref, xn_ref,
                 s_ref, m_ref, l_ref, acc_ref, o_ref):
    i = pl.program_id(1)
    m_ref[...] = jnp.full(m_ref.shape, -jnp.inf, F32)
    l_ref[...] = jnp.zeros(l_ref.shape, F32)
    acc_ref[...] = jnp.zeros(acc_ref.shape, F32)

    def scores(h, j):
        off = pl.multiple_of(j * TQ, TQ)
        hs = slice(h * HEAD_PAD, (h + 1) * HEAD_PAD)
        s_ref[h] = _dot(k_ref[pl.ds(off, TQ), hs], q_ref[0, hs, :])

    def update(h, j, masked):
        st = s_ref[h]
        if masked:
            kc = lax.broadcasted_iota(jnp.int32, (TQ, TQ), 0) // CHUNK
            qc = lax.broadcasted_iota(jnp.int32, (TQ, TQ), 1) // CHUNK
            st = jnp.where(kc <= qc, st, -jnp.inf)
        m_old = m_ref[h]
        m_new = jnp.maximum(m_old, jnp.max(st, axis=0, keepdims=True))
        p = jnp.exp2(st - m_new)
        alpha = jnp.exp2(m_old - m_new)
        l_ref[h] = alpha * l_ref[h] + jnp.sum(p, axis=0, keepdims=True)
        pv = _dot(vt_ref[j, h * V_HEAD:(h + 1) * V_HEAD, :], p.astype(BF16))
        acc_ref[h] = alpha * acc_ref[h] + pv
        m_ref[h] = m_new

    def full_tile(j, carry):
        for h in range(MLA_HEADS):
            scores(h, j)
        for h in range(MLA_HEADS):
            update(h, j, False)
        return carry

    lax.fori_loop(0, i, full_tile, 0)
    for h in range(MLA_HEADS):
        scores(h, i)
    for h in range(MLA_HEADS):
        update(h, i, True)
        o_ref[:, h * V_HEAD:(h + 1) * V_HEAD] = (acc_ref[h] / l_ref[h]).T.astype(BF16)

    x1 = x_ref[...] + _dot(o_ref[...], wo_ref[...])
    x1_ref[...] = x1
    xn_ref[...] = _rms(x1, g_ref[...]).astype(BF16)


def _attention(q, k, vt, x, w_o, g_ffn, batch, seq):
    h = MLA_HEADS
    nq = seq // TQ
    t, d = x.shape
    tile = lambda b, i: (b * nq + i, 0)
    const = lambda b, i: (0, 0)
    return pl.pallas_call(
        _attn_kernel, grid=(batch, nq),
        in_specs=[pl.BlockSpec((1, h * HEAD_PAD, TQ), lambda b, i: (b * nq + i, 0, 0)),
                  pl.BlockSpec((seq, h * HEAD_PAD), lambda b, i: (b, 0)),
                  pl.BlockSpec((nq, h * V_HEAD, TQ), lambda b, i: (b, 0, 0)),
                  pl.BlockSpec((TQ, d), tile), pl.BlockSpec(w_o.shape, const),
                  pl.BlockSpec((1, d), const)],
        out_specs=[pl.BlockSpec((TQ, d), tile), pl.BlockSpec((TQ, d), tile)],
        out_shape=[jax.ShapeDtypeStruct((t, d), F32), jax.ShapeDtypeStruct((t, d), BF16)],
        scratch_shapes=[pltpu.VMEM((h, TQ, TQ), F32),
                        pltpu.VMEM((h, 1, TQ), F32), pltpu.VMEM((h, 1, TQ), F32),
                        pltpu.VMEM((h, V_HEAD, TQ), F32),
                        pltpu.VMEM((TQ, h * V_HEAD), BF16)],
        compiler_params=_cparams(("parallel", "arbitrary")), name="attention",
    )(q, k, vt, x, w_o, g_ffn)


def _swiglu_chunk(x, w1, w3, w2):
    a = _dot(x, w1)
    b = _dot(x, w3)
    hid = (a * jax.nn.sigmoid(a)) * b
    return _dot(hid.astype(BF16), w2)


def _weight_chunk_copies(w1_hbm, w3_hbm, w2_hbm, lead, c, tf, slot, w1b, w3b, w2b, sem):
    cols = pl.ds(pl.multiple_of(c * tf, tf), tf)
    return (pltpu.make_async_copy(w1_hbm.at[lead + (slice(None), cols)], w1b.at[slot], sem.at[0, slot]),
            pltpu.make_async_copy(w3_hbm.at[lead + (slice(None), cols)], w3b.at[slot], sem.at[1, slot]),
            pltpu.make_async_copy(w2_hbm.at[lead + (cols, slice(None))], w2b.at[slot], sem.at[2, slot]))


def _ffn_kernel(xn_ref, x1_hbm, w1_hbm, w3_hbm, w2_hbm, o_ref, w1b, w3b, w2b, wb1, wb3, wb2,
                sem, xsem, *, layer, tf, rows):
    i = pl.program_id(0)
    n_tiles = pl.num_programs(0)
    tm = o_ref.shape[0]
    ff = w1_hbm.shape[2]
    chunks = [(lo, min(tf, ff - lo)) for lo in range(0, ff, tf)]
    assert len(chunks) % 2 == 0
    x1_copy = pltpu.make_async_copy(x1_hbm.at[pl.ds(pl.multiple_of(i * tm, tm), tm), :], o_ref, xsem)
    x1_copy.start()

    def copies(c):
        lo, width = chunks[c]
        slot = c % 2
        cols = pl.ds(lo, width)
        return (pltpu.make_async_copy(w1_hbm.at[layer, :, cols], w1b.at[slot, :, 0:width],
                                      sem.at[0, slot]),
                pltpu.make_async_copy(w3_hbm.at[layer, :, cols], w3b.at[slot, :, 0:width],
                                      sem.at[1, slot]),
                pltpu.make_async_copy(w2_hbm.at[layer, cols, :], w2b.at[slot, 0:width, :],
                                      sem.at[2, slot]))

    @pl.when(i == 0)
    def _():
        for cp in copies(0):
            cp.start()

    for c, (lo, width) in enumerate(chunks):
        if c + 1 < len(chunks):
            for cp in copies(c + 1):
                cp.start()
        else:
            @pl.when(i + 1 < n_tiles)
            def _():
                for cp in copies(0):
                    cp.start()
        for cp in copies(c):
            cp.wait()
        slot = c % 2
        wb1[:, 0:width] = w1b[slot, :, 0:width].astype(BF16)
        wb3[:, 0:width] = w3b[slot, :, 0:width].astype(BF16)
        wb2[0:width, :] = w2b[slot, 0:width, :].astype(BF16)
        for r in range(0, tm, rows):
            y = _swiglu_chunk(xn_ref[r:r + rows, :], wb1[:, 0:width], wb3[:, 0:width],
                              wb2[0:width, :])
            if c == 0 and r == 0:
                x1_copy.wait()
            o_ref[r:r + rows, :] += y


def _dense_ffn(xn, x1, w1, w3, w2, layer):
    t, d = x1.shape
    tm, tf = TM_FFN, TF_FFN
    row = lambda i: (i, 0)
    hbm = pl.BlockSpec(memory_space=pl.ANY)
    return pl.pallas_call(
        functools.partial(_ffn_kernel, layer=layer, tf=tf, rows=TSUB_FFN), grid=(t // tm,),
        in_specs=[pl.BlockSpec((tm, d), row), hbm, hbm, hbm, hbm],
        out_specs=pl.BlockSpec((tm, d), row),
        out_shape=jax.ShapeDtypeStruct((t, d), F32),
        scratch_shapes=[pltpu.VMEM((2, d, tf), F32), pltpu.VMEM((2, d, tf), F32),
                        pltpu.VMEM((2, tf, d), F32),
                        pltpu.VMEM((d, tf), BF16), pltpu.VMEM((d, tf), BF16),
                        pltpu.VMEM((tf, d), BF16),
                        pltpu.SemaphoreType.DMA((3, 2)), pltpu.SemaphoreType.DMA],
        compiler_params=_cparams(("arbitrary",)), name="dense_ffn",
    )(xn, x1, w1, w3, w2)


def _lru_kernel(x_ref, gmix_ref, win_ref, bin_ref, cw_ref, cb_ref, wga_ref, bga_ref,
                wgi_ref, bgi_ref, lam_ref, wout_ref, bout_ref, gffn_ref,
                x1_ref, xn_ref,
                xpad_ref, a_ref, u_ref, gate_ref, yg_ref, h_ref):
    nb, tt, d = x_ref.shape
    w = lam_ref.shape[1]
    m = nb * tt

    @pl.when(pl.program_id(0) == 0)
    def _():
        h_ref[...] = jnp.zeros_like(h_ref)
        xpad_ref[:, 0:SUBLANES, :] = jnp.zeros((nb, SUBLANES, w), F32)

    x = x_ref[...].reshape(m, d)
    xn = _rms(x, gmix_ref[...]).astype(BF16)
    hcat = _dot(xn, win_ref[...]) + bin_ref[...]
    gate_ref[...] = jax.nn.gelu(hcat[:, :w], approximate=True)
    xpad_ref[:, SUBLANES:, :] = hcat[:, w:].reshape(nb, tt, w)

    xc = cb_ref[...].reshape(1, 1, w) + jnp.zeros((nb, tt, w), F32)
    for j in range(CONV_WIDTH):
        lo = SUBLANES - (CONV_WIDTH - 1) + j
        xc = xc + xpad_ref[:, lo:lo + tt, :] * cw_ref[j:j + 1, :].reshape(1, 1, w)
    xpad_ref[:, 0:SUBLANES, :] = xpad_ref[:, tt:tt + SUBLANES, :]
    xc = xc.reshape(m, w)

    lam = lam_ref[...]
    log_sig = jnp.minimum(lam, 0.0) - jnp.log1p(jnp.exp(-jnp.abs(lam)))
    for hh in range(LRU_HEADS):
        sl = slice(hh * LANES, (hh + 1) * LANES)
        xh = xc[:, sl]
        xhb = xh.astype(BF16)
        r = jax.nn.sigmoid(_dot(xhb, wga_ref[hh]) + bga_ref[hh:hh + 1, :])
        ig = jax.nn.sigmoid(_dot(xhb, wgi_ref[hh]) + bgi_ref[hh:hh + 1, :])
        log_a = LRU_C * r * log_sig[:, sl]
        a = jnp.exp(log_a)
        z = 1.0 - a * a
        u = (z * lax.rsqrt(jnp.maximum(z, 1e-30))) * (ig * xh)
        for b in range(nb):
            a_ref[hh, b * LRU_PITCH:b * LRU_PITCH + tt, :] = a[b * tt:(b + 1) * tt, :]
            u_ref[hh, b * LRU_PITCH:b * LRU_PITCH + tt, :] = u[b * tt:(b + 1) * tt, :]

    def scan_step(t, hs):
        new = []
        for hh in range(LRU_HEADS):
            rows = pl.ds(t, nb, stride=LRU_PITCH)
            hv = a_ref[hh, rows, :] * hs[hh] + u_ref[hh, rows, :]
            u_ref[hh, rows, :] = hv
            new.append(hv)
        return tuple(new)

    hs = lax.fori_loop(0, tt, scan_step, tuple(h_ref[hh] for hh in range(LRU_HEADS)), unroll=4)
    for hh in range(LRU_HEADS):
        h_ref[hh] = hs[hh]

    for hh in range(LRU_HEADS):
        sl = slice(hh * LANES, (hh + 1) * LANES)
        for b in range(nb):
            y = u_ref[hh, b * LRU_PITCH:b * LRU_PITCH + tt, :]
            yg_ref[b * tt:(b + 1) * tt, sl] = (y * gate_ref[b * tt:(b + 1) * tt, sl]).astype(BF16)

    x1 = x + _dot(yg_ref[...], wout_ref[...]) + bout_ref[...]
    x1_ref[...] = x1.reshape(nb, tt, d)
    xn_ref[...] = _pack_pairs(_rms(x1, gffn_ref[...])).reshape(nb, tt, d // 2)


def _lru_mixer(x3, gmix, w_in, b_in, conv_w, conv_b, wga, bga, wgi, bgi, lam, w_out, b_out, g_ffn):
    nb, seq, d = x3.shape
    w = lam.shape[1]
    tt = T_LRU
    m = nb * tt
    consts = [gmix, w_in, b_in, conv_w, conv_b, wga, bga, wgi, bgi, lam, w_out, b_out, g_ffn]
    const_specs = [pl.BlockSpec(c.shape, (lambda i, n=c.ndim: (0,) * n)) for c in consts]
    blk = pl.BlockSpec((nb, tt, d), lambda i: (0, i, 0))
    return pl.pallas_call(
        _lru_kernel, grid=(seq // tt,),
        in_specs=[blk] + const_specs,
        out_specs=[blk, pl.BlockSpec((nb, tt, d // 2), lambda i: (0, i, 0))],
        out_shape=[jax.ShapeDtypeStruct((nb, seq, d), F32),
                   jax.ShapeDtypeStruct((nb, seq, d // 2), jnp.int32)],
        scratch_shapes=[pltpu.VMEM((nb, tt + SUBLANES, w), F32),
                        pltpu.VMEM((LRU_HEADS, nb * LRU_PITCH, LANES), F32),
                        pltpu.VMEM((LRU_HEADS, nb * LRU_PITCH, LANES), F32),
                        pltpu.VMEM((m, w), F32),
                        pltpu.VMEM((m, w), BF16),
                        pltpu.VMEM((LRU_HEADS, nb, LANES), F32)],
        compiler_params=_cparams(("arbitrary",)), name="rglru_mixer",
    )(x3, *consts)


def _router_kernel(x1_ref, g_ref, wh_ref, wl_ref, br_ref, slot_ref, gate_ref, te_ref, tv_ref,
                   lg_ref, tri_ref, cnt_ref, run_ref, start_ref, *, tile):
    phase = pl.program_id(0)
    j = pl.program_id(1)
    ne = N_EXPERTS
    tr = x1_ref.shape[0]

    @pl.when((phase == 0) & (j == 0))
    def _():
        cnt_ref[...] = jnp.zeros_like(cnt_ref)
        tri_ref[...] = jnp.where(lax.broadcasted_iota(jnp.int32, (tr, tr), 0)
                                 < lax.broadcasted_iota(jnp.int32, (tr, tr), 1),
                                 1.0, 0.0).astype(BF16)

    @pl.when(phase == 0)
    def _():
        x = _rms(x1_ref[...], g_ref[...])
        xh = x.astype(BF16)
        xl = (x - xh.astype(F32)).astype(BF16)
        wh = wh_ref[...]
        nat = _dot(xh, wh) + _dot(xl, wh) + _dot(xh, wl_ref[...])
        lg_ref[j] = nat.T[0:ne, :] + br_ref[...]

    logits = lg_ref[j]
    eidx = lax.broadcasted_iota(jnp.int32, (ne, tr), 0)
    m1 = jnp.max(logits, axis=0, keepdims=True)
    i1 = jnp.min(jnp.where(logits == m1, eidx, ne), axis=0, keepdims=True)
    oh1 = eidx == i1
    rest = jnp.where(oh1, -jnp.inf, logits)
    m2 = jnp.max(rest, axis=0, keepdims=True)
    i2 = jnp.min(jnp.where(rest == m2, eidx, ne), axis=0, keepdims=True)
    oh2 = eidx == i2
    oh = jnp.where(oh1 | oh2, 1.0, 0.0)
    tile_cnt = jnp.sum(oh, axis=1, keepdims=True)

    @pl.when(phase == 0)
    def _():
        cnt_ref[...] += tile_cnt

    @pl.when((phase == 1) & (j == 0))
    def _():
        cnt = cnt_ref[...]
        padded = jnp.ceil(cnt * (1.0 / tile)) * tile
        sub = lax.broadcasted_iota(jnp.int32, (ne, 1), 0)
        start = jnp.zeros((ne, 1), F32)
        for e in range(ne - 1):
            start = start + jnp.where(sub > e, padded[e:e + 1, :], 0.0)
        start_ref[...] = start
        run_ref[...] = jnp.zeros_like(run_ref)
        tile_start = lax.broadcasted_iota(jnp.int32, (ne, LANES), 1).astype(F32) * tile
        owner = jnp.sum(jnp.where(tile_start >= start + padded, 1, 0), axis=0, keepdims=True)
        owner = jnp.minimum(owner, ne - 1)
        esub = lax.broadcasted_iota(jnp.int32, (ne, LANES), 0)
        real = jnp.clip(cnt - (tile_start - start), 0.0, float(tile))
        te_ref[...] = owner
        tv_ref[...] = jnp.sum(jnp.where(esub == owner, real, 0.0), axis=0,
                              keepdims=True).astype(jnp.int32)

    @pl.when(phase == 1)
    def _():
        before = _dot(oh.astype(BF16), tri_ref[...])
        slot_e = start_ref[...] + run_ref[...] + before
        s1 = jnp.sum(jnp.where(oh1, slot_e, 0.0), axis=0, keepdims=True)
        s2 = jnp.sum(jnp.where(oh2, slot_e, 0.0), axis=0, keepdims=True)
        slot_ref[0:1, :] = s1.astype(jnp.int32)
        slot_ref[1:2, :] = s2.astype(jnp.int32)
        e21 = jnp.exp(m2 - m1)
        g1 = 1.0 / (1.0 + e21)
        g2 = e21 * g1
        rowi = lax.broadcasted_iota(jnp.int32, (LANES, tr), 0)
        gmat = jnp.where(rowi == 0, g1, jnp.where(rowi == 1, g2, 0.0))
        gate_ref[...] = gmat.T
        run_ref[...] += tile_cnt


def _router(x1, g_ffn, w_router, b_router, tile):
    t, d = x1.shape
    tr = TR
    nt = t // tr
    wp = jnp.pad(w_router, ((0, 0), (0, LANES - N_EXPERTS)))
    wh = wp.astype(BF16)
    wl = (wp - wh.astype(F32)).astype(BF16)
    const = lambda p, j: (0, 0)
    return pl.pallas_call(
        functools.partial(_router_kernel, tile=tile), grid=(2, nt),
        in_specs=[pl.BlockSpec((tr, d), lambda p, j: (j * (1 - p) + (nt - 1) * p, 0)),
                  pl.BlockSpec((1, d), const),
                  pl.BlockSpec(wh.shape, const), pl.BlockSpec(wl.shape, const),
                  pl.BlockSpec((N_EXPERTS, 1), const)],
        out_specs=[pl.BlockSpec((2, tr), lambda p, j: (0, j * p)),
                   pl.BlockSpec((tr, LANES), lambda p, j: (j * p, 0)),
                   pl.BlockSpec((1, LANES), const), pl.BlockSpec((1, LANES), const)],
        out_shape=[jax.ShapeDtypeStruct((2, t), jnp.int32),
                   jax.ShapeDtypeStruct((t, LANES), F32),
                   jax.ShapeDtypeStruct((1, LANES), jnp.int32),
                   jax.ShapeDtypeStruct((1, LANES), jnp.int32)],
        scratch_shapes=[pltpu.VMEM((nt, N_EXPERTS, tr), F32), pltpu.VMEM((tr, tr), BF16),
                        pltpu.VMEM((N_EXPERTS, 1), F32), pltpu.VMEM((N_EXPERTS, 1), F32),
                        pltpu.VMEM((N_EXPERTS, 1), F32)],
        compiler_params=_cparams(("arbitrary", "arbitrary")), name="moe_router",
    )(x1, g_ffn, wh, wl, b_router.reshape(N_EXPERTS, 1))


def _sc_mesh():
    return plsc.VectorSubcoreMesh(core_axis_name="c", subcore_axis_name="s")


def _sc_dispatch(xn, slots, n_slots):
    t, d = xn.shape
    win = SC_WINDOW
    nwin = t // win
    dp = d // SC_PLANES
    idx = slots.reshape(1, 2 * t)

    @functools.partial(pl.kernel,
                       out_type=jax.ShapeDtypeStruct((SC_PLANES, n_slots, dp), xn.dtype),
                       mesh=_sc_mesh(), scratch_types=[], name="moe_dispatch")
    def run(x_hbm, i_hbm, o_hbm):
        for c in range(SC_PLANES):
            def body(x_vmem, i0_vmem, i1_vmem, c=c):
                pltpu.sync_copy(x_vmem, o_hbm.at[c].at[i0_vmem.at[0]])
                pltpu.sync_copy(x_vmem, o_hbm.at[c].at[i1_vmem.at[0]])

            pltpu.emit_pipeline(
                body, grid=(nwin,),
                in_specs=[pl.BlockSpec((win, dp), lambda i, c=c: (i, c)),
                          pl.BlockSpec((1, win), lambda i: (0, i)),
                          pl.BlockSpec((1, win), lambda i: (0, i + nwin))],
                out_specs=[],
                core_axis_name=("c", "s"),
                dimension_semantics=(pltpu.PARALLEL,),
            )(x_hbm, i_hbm, i_hbm)

    return run(xn, idx)


def _sc_combine(y, slots):
    n2 = slots.size
    dp = y.shape[2]
    win = SC_WINDOW
    idx = slots.reshape(1, n2)

    @functools.partial(pl.kernel, out_type=jax.ShapeDtypeStruct((SC_PLANES, n2, dp), y.dtype),
                       mesh=_sc_mesh(), scratch_types=[], name="moe_combine")
    def run(y_hbm, i_hbm, o_hbm):
        for c in range(SC_PLANES):
            def body(i_vmem, o_vmem, c=c):
                pltpu.sync_copy(y_hbm.at[c].at[i_vmem.at[0]], o_vmem)

            pltpu.emit_pipeline(
                body, grid=(n2 // win,),
                in_specs=[pl.BlockSpec((1, win), lambda i: (0, i))],
                out_specs=[pl.BlockSpec((win, dp), lambda i: (i, 0))],
                core_axis_name=("c", "s"),
                dimension_semantics=(pltpu.PARALLEL,),
            )(i_hbm, o_hbm.at[c])

    return run(y, idx)


def _moe_ffn_kernel(te_ref, tv_ref, xs_ref, w1_hbm, w3_hbm, w2_hbm, y_ref,
                    xb_ref, acc_ref, w1b, w3b, w2b, wb1_ref, wb3_ref, wb2_ref, sem, *, layer, tf):
    i = pl.program_id(0)
    n_tiles = pl.num_programs(0)
    valid = tv_ref[i]
    tm, dp = xs_ref.shape[1:]
    ts = TSUB_MOE
    nc = w1_hbm.shape[3] // tf

    def copies(tile, c, slot):
        return _weight_chunk_copies(w1_hbm, w3_hbm, w2_hbm, (layer, te_ref[tile]), c, tf, slot,
                                    w1b, w3b, w2b, sem)

    @pl.when((i == 0) & (valid > 0))
    def _():
        for cp in copies(0, 0, 0):
            cp.start()

    rows = lax.broadcasted_iota(jnp.int32, (tm, 1), 0)
    for c, piece in enumerate(_unpack_planes(xs_ref)):
        xb_ref[:, c * dp:(c + 1) * dp] = jnp.where(rows < valid, piece, 0.0).astype(BF16)
    acc_ref[...] = jnp.zeros_like(acc_ref)

    def rows_block(lo, n, w1, w3, w2):
        acc_ref[lo:lo + n, :] += _swiglu_chunk(xb_ref[lo:lo + n, :], w1, w3, w2)

    @pl.when(valid > 0)
    def _():
        nxt_tile = jnp.minimum(i + 1, n_tiles - 1)
        next_tile_live = (i + 1 < n_tiles) & (tv_ref[nxt_tile] > 0)

        def chunk(c, carry):
            slot = lax.rem(i * nc + c, 2)
            last = c + 1 == nc

            @pl.when(jnp.logical_not(last) | next_tile_live)
            def _():
                for cp in copies(jnp.where(last, nxt_tile, i), jnp.where(last, 0, c + 1), 1 - slot):
                    cp.start()

            for cp in copies(i, c, slot):
                cp.wait()

            @pl.when(valid == tm)
            def _():
                rows_block(0, tm, w1b[slot].astype(BF16), w3b[slot].astype(BF16),
                           w2b[slot].astype(BF16))

            @pl.when(valid < tm)
            def _():
                wb1_ref[...] = w1b[slot].astype(BF16)
                wb3_ref[...] = w3b[slot].astype(BF16)
                wb2_ref[...] = w2b[slot].astype(BF16)
                for s in range(tm // ts):
                    @pl.when(s * ts < valid)
                    def _():
                        rows_block(s * ts, ts, wb1_ref[...], wb3_ref[...], wb2_ref[...])

            return carry

        lax.fori_loop(0, nc, chunk, 0)

    packed = _pack_pairs(acc_ref[...])
    for c in range(SC_PLANES):
        y_ref[c] = packed[:, c * dp:(c + 1) * dp]


def _moe_ffn(xs, tile_expert, tile_valid, w1, w3, w2, layer):
    planes, ns, dp = xs.shape
    d = 2 * planes * dp
    tm, tf = TM_MOE, TF_MOE
    hbm = pl.BlockSpec(memory_space=pl.ANY)
    grid_spec = pltpu.PrefetchScalarGridSpec(
        num_scalar_prefetch=2, grid=(ns // tm,),
        in_specs=[pl.BlockSpec((planes, tm, dp), lambda i, te, tv: (0, i, 0)), hbm, hbm, hbm],
        out_specs=pl.BlockSpec((planes, tm, dp), lambda i, te, tv: (0, i, 0)),
        scratch_shapes=[pltpu.VMEM((tm, d), BF16), pltpu.VMEM((tm, d), F32),
                        pltpu.VMEM((2, d, tf), F32), pltpu.VMEM((2, d, tf), F32),
                        pltpu.VMEM((2, tf, d), F32),
                        pltpu.VMEM((d, tf), BF16), pltpu.VMEM((d, tf), BF16),
                        pltpu.VMEM((tf, d), BF16), pltpu.SemaphoreType.DMA((3, 2))])
    return pl.pallas_call(
        functools.partial(_moe_ffn_kernel, layer=layer, tf=tf), grid_spec=grid_spec,
        out_shape=jax.ShapeDtypeStruct((planes, ns, dp), jnp.int32),
        compiler_params=_cparams(("arbitrary",)), name="moe_ffn",
    )(tile_expert, tile_valid, xs, w1, w3, w2)


def _combine_kernel(x1_ref, gt_ref, y0_ref, y1_ref, o_ref):
    o_ref[...] = _gated_sum(x1_ref, gt_ref, y0_ref, y1_ref)


def _combine(x1, gates_tok, yg):
    t, d = x1.shape
    tm = 512
    nt = t // tm
    row = lambda i: (i, 0)
    return pl.pallas_call(
        _combine_kernel, grid=(nt,),
        in_specs=[pl.BlockSpec((tm, d), row), pl.BlockSpec((tm, LANES), row),
                  pl.BlockSpec((SC_PLANES, tm, d // (2 * SC_PLANES)), lambda i: (0, i, 0)),
                  pl.BlockSpec((SC_PLANES, tm, d // (2 * SC_PLANES)), lambda i: (0, i + nt, 0))],
        out_specs=pl.BlockSpec((tm, d), row),
        out_shape=jax.ShapeDtypeStruct((t, d), F32),
        compiler_params=_cparams(("parallel",)), name="moe_combine_residual",
    )(x1, gates_tok, yg, yg)


def _mla_weights(w_in, w_qb, w_kvb, g_qn, g_kn):
    d = w_in.shape[0]
    h = MLA_HEADS
    lat = Q_LORA + KV_LORA
    win = jnp.zeros((d, lat + HEAD_PAD), F32)
    win = win.at[:, :lat].set(w_in[:, :lat])
    win = win.at[:, lat + LANES:lat + LANES + QK_ROPE].set(w_in[:, lat:])
    wq = w_qb.reshape(Q_LORA, h, QK_HEAD)
    wq = jnp.pad(wq, ((0, 0), (0, 0), (0, HEAD_PAD - QK_HEAD))).reshape(Q_LORA, h * HEAD_PAD)
    wkv = w_kvb.reshape(KV_LORA, h, QK_NOPE + V_HEAD)
    wk = wkv[:, :, :QK_NOPE].reshape(KV_LORA, h * QK_NOPE)
    wvt = wkv[:, :, QK_NOPE:].reshape(KV_LORA, h * V_HEAD).T
    scale = math.log2(math.e) / math.sqrt(QK_HEAD)
    gq = jnp.broadcast_to((g_qn * scale).reshape(QK_HEAD, 1), (QK_HEAD, TQ))
    gk = jnp.pad(g_kn, (0, HEAD_PAD - QK_HEAD)).reshape(1, HEAD_PAD)
    return (win.astype(BF16), wq.T.astype(BF16), wk.astype(BF16), wvt.astype(BF16), gq, gk)


def kernel(x, positions, norm_mix, norm_ffn, mla_w_in, mla_g_qa, mla_w_qb, mla_g_kva, mla_w_kvb, mla_g_qn, mla_g_kn, mla_w_o, lru_w_in, lru_b_in, lru_conv_w, lru_conv_b, lru_w_gate_a, lru_b_gate_a, lru_w_gate_i, lru_b_gate_i, lru_lambda, lru_w_out, lru_b_out, ffn_w1, ffn_w3, ffn_w2, moe_w_router, moe_b_router, moe_w1, moe_w3, moe_w2):
    batch, seq, d = x.shape
    t = batch * seq
    depth = norm_mix.shape[0]
    n_slots = 2 * t + N_EXPERTS * TM_MOE
    n_tiles = n_slots // TM_MOE
    row = lambda v: v.reshape(1, -1)

    cs, cst = _rope_tables(positions)
    parts = (x.reshape(t, d),)
    for i in range(depth):
        j = i // 2
        if i % 2 == 0:
            w_in, w_qbt, w_kb, w_vt, g_qn, g_kn = _mla_weights(
                mla_w_in[j], mla_w_qb[j], mla_w_kvb[j], mla_g_qn[j], mla_g_kn[j])
            xr, q, k, vt = _mla_qkv(parts, cs, cst, row(norm_mix[i]), w_in, row(mla_g_qa[j]),
                                    w_qbt, row(mla_g_kva[j]), w_kb, w_vt, g_qn, g_kn)
            x1, xn = _attention(q, k, vt, xr, mla_w_o[j].astype(BF16), row(norm_ffn[i]),
                                batch, seq)
            parts = (_dense_ffn(xn, x1, ffn_w1, ffn_w3, ffn_w2, j),)
        else:
            (xr,) = parts
            x1, xn = _lru_mixer(
                xr.reshape(batch, seq, d), row(norm_mix[i]), lru_w_in[j].astype(BF16),
                row(lru_b_in[j]), lru_conv_w[j], row(lru_conv_b[j]),
                lru_w_gate_a[j].astype(BF16), lru_b_gate_a[j], lru_w_gate_i[j].astype(BF16),
                lru_b_gate_i[j], row(lru_lambda[j]), lru_w_out[j].astype(BF16),
                row(lru_b_out[j]), row(norm_ffn[i]))
            x1 = x1.reshape(t, d)
            xn = xn.reshape(t, d // 2)
            slots, gates_tok, te, tv = _router(x1, row(norm_ffn[i]), moe_w_router[j],
                                               moe_b_router[j], TM_MOE)
            xs = _sc_dispatch(xn, slots, n_slots)
            y = _moe_ffn(xs, te[0, :n_tiles], tv[0, :n_tiles], moe_w1, moe_w3, moe_w2, j)
            yg = _sc_combine(y, slots)
            parts = (x1, gates_tok, yg)
    if len(parts) == 3:
        out = _combine(*parts)
    else:
        out = parts[0]
    return out.reshape(batch, seq, d)
```

```python
import functools
import math

import numpy as np
import jax
import jax.numpy as jnp
from jax import lax
from jax.experimental import pallas as pl
from jax.experimental.pallas import tpu as pltpu
from jax.experimental.pallas import tpu_sc as plsc

F32 = jnp.float32
BF16 = jnp.bfloat16

NORM_EPS = 1e-6
CHUNK = 64
MLA_HEADS = 8
QK_NOPE = 128
QK_ROPE = 64
QK_HEAD = QK_NOPE + QK_ROPE
V_HEAD = 128
Q_LORA = 384
KV_LORA = 256
ROPE_BASE = 10000.0
LRU_HEADS = 8
CONV_WIDTH = 4
LRU_C = 8.0
N_EXPERTS = 8

LANES = 128
SUBLANES = 8
HEAD_PAD = 2 * LANES
VMEM_LIMIT = 56 * 1024 * 1024

TQ = 256
TM_QKV = 2 * TQ
ATTN_SUM_ROWS = 16
TM_FFN = 1024
TF_FFN = 512
T_LRU = 64
LRU_PITCH = T_LRU + SUBLANES
TR = 512
TM_MOE = 1024
TSUB_MOE = 256
TF_MOE = 512
SC_WINDOW = 128
SC_PLANES = 2


def _cparams(sem):
    return pltpu.CompilerParams(dimension_semantics=sem, vmem_limit_bytes=VMEM_LIMIT)


def _rms(x, g):
    return x * lax.rsqrt(jnp.mean(x * x, axis=-1, keepdims=True) + NORM_EPS) * g


def _dot(a, b):
    return jnp.dot(a, b, preferred_element_type=F32)


def _dot_nt(a, b):
    return lax.dot_general(a, b, (((1,), (1,)), ((), ())), preferred_element_type=F32)


def _pack_pairs(x):
    n = x.shape[1] // 2
    bits = lax.bitcast_convert_type(x.astype(BF16).astype(F32), jnp.int32)
    return bits[:, :n] | lax.shift_right_logical(bits[:, n:], 16)


def _unpack_pairs(word):
    hi = lax.bitcast_convert_type(word & jnp.int32(-65536), F32)
    lo = lax.bitcast_convert_type(lax.shift_left(word, 16), F32)
    return hi, lo


def _unpack_planes(ref):
    halves = [_unpack_pairs(ref[c]) for c in range(SC_PLANES)]
    return [h for h, _ in halves] + [l for _, l in halves]


def _rope_kernel(pos_ref, inv_ref, cs_ref, cst_ref):
    ang = inv_ref[...] * pos_ref[...].astype(F32)
    half = QK_ROPE // 2
    ct = jnp.cos(ang)
    st = jnp.sin(ang)
    cst_ref[0:half, :] = ct[0:half, :]
    cst_ref[half:QK_ROPE, :] = st[0:half, :]
    pad = jnp.zeros((LANES - QK_ROPE, ang.shape[1]), F32)
    c = jnp.concatenate([ct, pad], axis=0).T
    s = jnp.concatenate([st, pad], axis=0).T
    lane = lax.broadcasted_iota(jnp.int32, c.shape, 1)
    cs_ref[:, 0:LANES] = jnp.where(lane < QK_ROPE, c, 0.0)
    cs_ref[:, LANES:2 * LANES] = jnp.where(lane < half, -s, 0.0)
    cs_ref[:, 2 * LANES:3 * LANES] = jnp.where((lane >= half) & (lane < QK_ROPE), s, 0.0)


def _rope_tables(positions):
    t = positions.size
    tm = 512
    inv = 1.0 / (ROPE_BASE ** (np.arange(0, QK_ROPE, 2, dtype=np.float32) / QK_ROPE))
    inv_col = np.concatenate([inv, inv]).reshape(QK_ROPE, 1).astype(np.float32)
    return pl.pallas_call(
        _rope_kernel,
        grid=(t // tm,),
        in_specs=[pl.BlockSpec((1, tm), lambda i: (0, i)),
                  pl.BlockSpec((QK_ROPE, 1), lambda i: (0, 0))],
        out_specs=[pl.BlockSpec((tm, 3 * LANES), lambda i: (i, 0)),
                   pl.BlockSpec((QK_ROPE, tm), lambda i: (0, i))],
        out_shape=[jax.ShapeDtypeStruct((t, 3 * LANES), F32),
                   jax.ShapeDtypeStruct((QK_ROPE, t), F32)],
        compiler_params=_cparams(("parallel",)),
        name="rope_tables",
    )(positions.reshape(1, t), jnp.asarray(inv_col))


def _rope_rot(x, cs):
    c = cs[:, 0:LANES]
    sa = cs[:, LANES:2 * LANES]
    sb = cs[:, 2 * LANES:3 * LANES]
    return (x * c + pltpu.roll(x, LANES - QK_ROPE // 2, 1) * sa
            + pltpu.roll(x, QK_ROPE // 2, 1) * sb)


def _mla_latent(x, gmix_ref, win_ref):
    return _dot(_rms(x, gmix_ref[...]).astype(BF16), win_ref[...])


def _mla_expand(hcat, gqa_ref, wqb_ref, gkva_ref, wkb_ref, wvt_ref):
    cq = hcat[:, :Q_LORA]
    ckv = hcat[:, Q_LORA:Q_LORA + KV_LORA]
    kr = hcat[:, Q_LORA + KV_LORA + LANES:Q_LORA + KV_LORA + 2 * LANES]
    qft = _dot_nt(wqb_ref[...], _rms(cq, gqa_ref[...]).astype(BF16))
    ckvn = _rms(ckv, gkva_ref[...]).astype(BF16)
    kf = _dot(ckvn, wkb_ref[...])
    vt = _dot_nt(wvt_ref[...], ckvn).astype(BF16)
    return qft, kf, kr, vt


def _mla_q_heads(qft, cst, gqn_ref, qt_ref, s):
    half = QK_ROPE // 2
    c, sn = cst[0:half, :], cst[half:QK_ROPE, :]
    g = gqn_ref[...]
    zero = jnp.zeros((HEAD_PAD - QK_HEAD, qft.shape[1]), BF16)
    for h in range(MLA_HEADS):
        blk = qft[h * HEAD_PAD:h * HEAD_PAD + QK_HEAD, :]
        ss = jnp.sum(blk * blk, axis=0, keepdims=True)
        inv = lax.rsqrt(ss * (1.0 / QK_HEAD) + NORM_EPS)
        y = blk * g * inv
        x1, x2 = y[QK_NOPE:QK_NOPE + half, :], y[QK_NOPE + half:QK_HEAD, :]
        base = h * HEAD_PAD
        qt_ref[s, base:base + QK_NOPE, :] = y[0:QK_NOPE, :].astype(BF16)
        qt_ref[s, base + QK_NOPE:base + QK_NOPE + half, :] = (x1 * c - x2 * sn).astype(BF16)
        qt_ref[s, base + QK_NOPE + half:base + QK_HEAD, :] = (x1 * sn + x2 * c).astype(BF16)
        qt_ref[s, base + QK_HEAD:base + HEAD_PAD, :] = zero


def _mla_k_heads(kf, kr, cs, gkn_ref, k_ref, rows):
    gkn_n, gkn_r = gkn_ref[:, 0:LANES], gkn_ref[:, LANES:2 * LANES]
    kr_ss = jnp.sum(kr * kr, axis=-1, keepdims=True)
    kr_rot = _rope_rot(kr * gkn_r, cs)
    for h in range(MLA_HEADS):
        kn = kf[:, h * LANES:(h + 1) * LANES]
        ssk = jnp.sum(kn * kn, axis=-1, keepdims=True) + kr_ss
        invk = lax.rsqrt(ssk * (1.0 / QK_HEAD) + NORM_EPS)
        k_ref[rows, h * HEAD_PAD:h * HEAD_PAD + LANES] = (kn * invk * gkn_n).astype(BF16)
        k_ref[rows, h * HEAD_PAD + LANES:(h + 1) * HEAD_PAD] = (kr_rot * invk).astype(BF16)


def _mla_qkv_body(x, cs_ref, cst_ref, gmix_ref, win_ref, gqa_ref, wqb_ref, gkva_ref, wkb_ref,
                  wvt_ref, gqn_ref, gkn_ref, qt_ref, k_ref, vt_ref):
    nsub = x.shape[0] // TQ
    rows = [slice(s * TQ, (s + 1) * TQ) for s in range(nsub)]
    hcats = [_mla_latent(x[r], gmix_ref, win_ref) for r in rows]
    mids = [_mla_expand(hc, gqa_ref, wqb_ref, gkva_ref, wkb_ref, wvt_ref) for hc in hcats]
    for s, (qft, kf, kr, vt) in enumerate(mids):
        vt_ref[s] = vt
    for s, (r, (qft, kf, kr, vt)) in enumerate(zip(rows, mids)):
        _mla_q_heads(qft, cst_ref[:, r], gqn_ref, qt_ref, s)
        _mla_k_heads(kf, kr, cs_ref[r, :], gkn_ref, k_ref, r)


def _gated_sum(x1_ref, gt_ref, y0_ref, y1_ref):
    gt = gt_ref[...]
    g0, g1 = gt[:, 0:1], gt[:, 1:2]
    y = jnp.concatenate([g0 * a + g1 * b
                         for a, b in zip(_unpack_planes(y0_ref), _unpack_planes(y1_ref))], axis=1)
    return x1_ref[...] + y


def _mla_qkv_kernel(x_ref, *rest):
    _mla_qkv_body(x_ref[...], *rest)


def _mla_qkv_combine_kernel(x1_ref, gt_ref, y0_ref, y1_ref, *rest):
    *mid, xo_ref, q_ref, k_ref, v_ref = rest
    x = _gated_sum(x1_ref, gt_ref, y0_ref, y1_ref)
    xo_ref[...] = x
    _mla_qkv_body(x, *mid, q_ref, k_ref, v_ref)


def _mla_qkv(x_parts, cs, cst, gmix, w_in, g_qa, w_qbt, g_kva, w_kb, w_vt, g_qn, g_kn):
    t, d = x_parts[0].shape
    tm = TM_QKV
    nsub = tm // TQ
    nt = t // tm
    row = lambda i: (i, 0)
    const = lambda i: (0, 0)
    h = MLA_HEADS
    weights = [gmix, w_in, g_qa, w_qbt, g_kva, w_kb, w_vt, g_qn, g_kn]
    w_specs = [pl.BlockSpec(w.shape, const) for w in weights]
    qkv_specs = [pl.BlockSpec((nsub, h * HEAD_PAD, TQ), lambda i: (i, 0, 0)),
                 pl.BlockSpec((tm, h * HEAD_PAD), row),
                 pl.BlockSpec((nsub, h * V_HEAD, TQ), lambda i: (i, 0, 0))]
    qkv_shapes = [jax.ShapeDtypeStruct((t // TQ, h * HEAD_PAD, TQ), BF16),
                  jax.ShapeDtypeStruct((t, h * HEAD_PAD), BF16),
                  jax.ShapeDtypeStruct((t // TQ, h * V_HEAD, TQ), BF16)]
    cs_specs = [pl.BlockSpec((tm, 3 * LANES), row), pl.BlockSpec((QK_ROPE, tm), lambda i: (0, i))]
    if len(x_parts) == 1:
        q, k, v = pl.pallas_call(
            _mla_qkv_kernel, grid=(nt,),
            in_specs=[pl.BlockSpec((tm, d), row)] + cs_specs + w_specs,
            out_specs=qkv_specs, out_shape=qkv_shapes,
            compiler_params=_cparams(("parallel",)), name="mla_qkv",
        )(x_parts[0], cs, cst, *weights)
        return x_parts[0], q, k, v
    x1, gates_tok, yg = x_parts
    x, q, k, v = pl.pallas_call(
        _mla_qkv_combine_kernel, grid=(nt,),
        in_specs=[pl.BlockSpec((tm, d), row), pl.BlockSpec((tm, LANES), row),
                  pl.BlockSpec((SC_PLANES, tm, d // (2 * SC_PLANES)), lambda i: (0, i, 0)),
                  pl.BlockSpec((SC_PLANES, tm, d // (2 * SC_PLANES)), lambda i: (0, i + nt, 0)),
                  ] + cs_specs + w_specs,
        out_specs=[pl.BlockSpec((tm, d), row)] + qkv_specs,
        out_shape=[jax.ShapeDtypeStruct((t, d), F32)] + qkv_shapes,
        compiler_params=_cparams(("parallel",)), name="mla_qkv_combine",
    )(x1, gates_tok, yg, yg, cs, cst, *weights)
    return x, q, k, v


def _attn_kernel(q_ref, k_ref, vt_ref, x_ref, wo_ref, g_ref, x1_ref, xn_ref,
                 s_ref, m_ref, acc_ref, o_ref):
    i = pl.program_id(1)
    ones = jnp.ones((ATTN_SUM_ROWS, TQ), BF16)

    def scores(h, j):
        off = pl.multiple_of(j * TQ, TQ)
        hs = slice(h * HEAD_PAD, (h + 1) * HEAD_PAD)
        s_ref[h] = _dot(k_ref[pl.ds(off, TQ), hs], q_ref[0, hs, :])

    def weighted(h, j, p):
        lhs = jnp.concatenate([vt_ref[j, h * V_HEAD:(h + 1) * V_HEAD, :], ones], axis=0)
        return _dot(lhs, p.astype(BF16))

    for h in range(MLA_HEADS):
        scores(h, i)
    kc = lax.broadcasted_iota(jnp.int32, (TQ, TQ), 0) // CHUNK
    qc = lax.broadcasted_iota(jnp.int32, (TQ, TQ), 1) // CHUNK
    for h in range(MLA_HEADS):
        st = jnp.where(kc <= qc, s_ref[h], -jnp.inf)
        m = jnp.max(st, axis=0, keepdims=True)
        acc_ref[h] = weighted(h, i, jnp.exp2(st - m))
        m_ref[h] = m

    def update(h, j):
        st = s_ref[h]
        m_old = m_ref[h]
        m_new = jnp.maximum(m_old, jnp.max(st, axis=0, keepdims=True))
        acc_ref[h] = jnp.exp2(m_old - m_new) * acc_ref[h] + weighted(h, j, jnp.exp2(st - m_new))
        m_ref[h] = m_new

    def full_tile(j, carry):
        for h in range(MLA_HEADS):
            scores(h, j)
        for h in range(MLA_HEADS):
            update(h, j)
        return carry

    lax.fori_loop(0, i, full_tile, 0)
    for h in range(MLA_HEADS):
        acc = acc_ref[h]
        o_ref[:, h * V_HEAD:(h + 1) * V_HEAD] = (
            acc[0:V_HEAD, :] / acc[V_HEAD:V_HEAD + 1, :]).T.astype(BF16)

    x1 = x_ref[...] + _dot(o_ref[...], wo_ref[...])
    x1_ref[...] = x1
    xn_ref[...] = _rms(x1, g_ref[...]).astype(BF16)


def _attention(q, k, vt, x, w_o, g_ffn, batch, seq):
    h = MLA_HEADS
    nq = seq // TQ
    t, d = x.shape
    tile = lambda b, i: (b * nq + i, 0)
    const = lambda b, i: (0, 0)
    return pl.pallas_call(
        _attn_kernel, grid=(batch, nq),
        in_specs=[pl.BlockSpec((1, h * HEAD_PAD, TQ), lambda b, i: (b * nq + i, 0, 0)),
                  pl.BlockSpec((seq, h * HEAD_PAD), lambda b, i: (b, 0)),
                  pl.BlockSpec((nq, h * V_HEAD, TQ), lambda b, i: (b, 0, 0)),
                  pl.BlockSpec((TQ, d), tile), pl.BlockSpec(w_o.shape, const),
                  pl.BlockSpec((1, d), const)],
        out_specs=[pl.BlockSpec((TQ, d), tile), pl.BlockSpec((TQ, d), tile)],
        out_shape=[jax.ShapeDtypeStruct((t, d), F32), jax.ShapeDtypeStruct((t, d), BF16)],
        scratch_shapes=[pltpu.VMEM((h, TQ, TQ), F32),
                        pltpu.VMEM((h, 1, TQ), F32),
                        pltpu.VMEM((h, V_HEAD + ATTN_SUM_ROWS, TQ), F32),
                        pltpu.VMEM((TQ, h * V_HEAD), BF16)],
        compiler_params=_cparams(("parallel", "arbitrary")), name="attention",
    )(q, k, vt, x, w_o, g_ffn)


def _swiglu_chunk(x, w1, w3, w2):
    a = _dot(x, w1)
    b = _dot(x, w3)
    hid = (a * jax.nn.sigmoid(a)) * b
    return _dot(hid.astype(BF16), w2)


def _weight_chunk_copies(w1_hbm, w3_hbm, w2_hbm, lead, c, tf, slot, w1b, w3b, w2b, sem):
    cols = pl.ds(pl.multiple_of(c * tf, tf), tf)
    return (pltpu.make_async_copy(w1_hbm.at[lead + (slice(None), cols)], w1b.at[slot], sem.at[0, slot]),
            pltpu.make_async_copy(w3_hbm.at[lead + (slice(None), cols)], w3b.at[slot], sem.at[1, slot]),
            pltpu.make_async_copy(w2_hbm.at[lead + (cols, slice(None))], w2b.at[slot], sem.at[2, slot]))


def _ffn_kernel(xn_ref, x1_ref, w1_hbm, w3_hbm, w2_hbm, o_ref, w1b, w3b, w2b, sem, *, layer, tf):
    i = pl.program_id(0)
    n_tiles = pl.num_programs(0)
    ff = w1_hbm.shape[2]
    chunks = [(lo, min(tf, ff - lo)) for lo in range(0, ff, tf)]
    assert len(chunks) % 2 == 0

    def copies(c):
        lo, width = chunks[c]
        slot = c % 2
        cols = pl.ds(lo, width)
        return (pltpu.make_async_copy(w1_hbm.at[layer, :, cols], w1b.at[slot, :, 0:width],
                                      sem.at[0, slot]),
                pltpu.make_async_copy(w3_hbm.at[layer, :, cols], w3b.at[slot, :, 0:width],
                                      sem.at[1, slot]),
                pltpu.make_async_copy(w2_hbm.at[layer, cols, :], w2b.at[slot, 0:width, :],
                                      sem.at[2, slot]))

    @pl.when(i == 0)
    def _():
        for cp in copies(0):
            cp.start()

    o_ref[...] = x1_ref[...]
    for c, (lo, width) in enumerate(chunks):
        if c + 1 < len(chunks):
            for cp in copies(c + 1):
                cp.start()
        else:
            @pl.when(i + 1 < n_tiles)
            def _():
                for cp in copies(0):
                    cp.start()
        for cp in copies(c):
            cp.wait()
        slot = c % 2
        o_ref[...] += _swiglu_chunk(xn_ref[...], w1b[slot, :, 0:width].astype(BF16),
                                    w3b[slot, :, 0:width].astype(BF16),
                                    w2b[slot, 0:width, :].astype(BF16))


def _dense_ffn(xn, x1, w1, w3, w2, layer):
    t, d = x1.shape
    tm, tf = TM_FFN, TF_FFN
    row = lambda i: (i, 0)
    hbm = pl.BlockSpec(memory_space=pl.ANY)
    return pl.pallas_call(
        functools.partial(_ffn_kernel, layer=layer, tf=tf), grid=(t // tm,),
        in_specs=[pl.BlockSpec((tm, d), row), pl.BlockSpec((tm, d), row), hbm, hbm, hbm],
        out_specs=pl.BlockSpec((tm, d), row),
        out_shape=jax.ShapeDtypeStruct((t, d), F32),
        scratch_shapes=[pltpu.VMEM((2, d, tf), F32), pltpu.VMEM((2, d, tf), F32),
                        pltpu.VMEM((2, tf, d), F32), pltpu.SemaphoreType.DMA((3, 2))],
        compiler_params=_cparams(("arbitrary",)), name="dense_ffn",
    )(xn, x1, w1, w3, w2)


def _lru_kernel(x_ref, gmix_ref, win_ref, bin_ref, cw_ref, cb_ref, wga_ref, bga_ref,
                wgi_ref, bgi_ref, lam_ref, wout_ref, bout_ref, gffn_ref,
                x1_ref, xn_ref,
                xpad_ref, a_ref, u_ref, gate_ref, yg_ref, h_ref):
    nb, tt, d = x_ref.shape
    w = lam_ref.shape[1]
    m = nb * tt

    @pl.when(pl.program_id(0) == 0)
    def _():
        h_ref[...] = jnp.zeros_like(h_ref)
        xpad_ref[:, 0:SUBLANES, :] = jnp.zeros((nb, SUBLANES, w), F32)

    x = x_ref[...].reshape(m, d)
    xn = _rms(x, gmix_ref[...]).astype(BF16)
    hcat = _dot(xn, win_ref[...]) + bin_ref[...]
    gate_ref[...] = jax.nn.gelu(hcat[:, :w], approximate=True)
    xpad_ref[:, SUBLANES:, :] = hcat[:, w:].reshape(nb, tt, w)

    xc = cb_ref[...].reshape(1, 1, w) + jnp.zeros((nb, tt, w), F32)
    for j in range(CONV_WIDTH):
        lo = SUBLANES - (CONV_WIDTH - 1) + j
        xc = xc + xpad_ref[:, lo:lo + tt, :] * cw_ref[j:j + 1, :].reshape(1, 1, w)
    xpad_ref[:, 0:SUBLANES, :] = xpad_ref[:, tt:tt + SUBLANES, :]
    xc = xc.reshape(m, w)

    lam = lam_ref[...]
    log_sig = jnp.minimum(lam, 0.0) - jnp.log1p(jnp.exp(-jnp.abs(lam)))
    for hh in range(LRU_HEADS):
        sl = slice(hh * LANES, (hh + 1) * LANES)
        xh = xc[:, sl]
        xhb = xh.astype(BF16)
        r = jax.nn.sigmoid(_dot(xhb, wga_ref[hh]) + bga_ref[hh:hh + 1, :])
        ig = jax.nn.sigmoid(_dot(xhb, wgi_ref[hh]) + bgi_ref[hh:hh + 1, :])
        log_a = LRU_C * r * log_sig[:, sl]
        a = jnp.exp(log_a)
        z = 1.0 - a * a
        u = (z * lax.rsqrt(jnp.maximum(z, 1e-30))) * (ig * xh)
        for b in range(nb):
            a_ref[hh, b * LRU_PITCH:b * LRU_PITCH + tt, :] = a[b * tt:(b + 1) * tt, :]
            u_ref[hh, b * LRU_PITCH:b * LRU_PITCH + tt, :] = u[b * tt:(b + 1) * tt, :]

    def scan_step(t, hs):
        new = []
        for hh in range(LRU_HEADS):
            rows = pl.ds(t, nb, stride=LRU_PITCH)
            hv = a_ref[hh, rows, :] * hs[hh] + u_ref[hh, rows, :]
            u_ref[hh, rows, :] = hv
            new.append(hv)
        return tuple(new)

    hs = lax.fori_loop(0, tt, scan_step, tuple(h_ref[hh] for hh in range(LRU_HEADS)), unroll=4)
    for hh in range(LRU_HEADS):
        h_ref[hh] = hs[hh]

    for hh in range(LRU_HEADS):
        sl = slice(hh * LANES, (hh + 1) * LANES)
        for b in range(nb):
            y = u_ref[hh, b * LRU_PITCH:b * LRU_PITCH + tt, :]
            yg_ref[b * tt:(b + 1) * tt, sl] = (y * gate_ref[b * tt:(b + 1) * tt, sl]).astype(BF16)

    x1 = x + _dot(yg_ref[...], wout_ref[...]) + bout_ref[...]
    x1_ref[...] = x1.reshape(nb, tt, d)
    xn_ref[...] = _pack_pairs(_rms(x1, gffn_ref[...])).reshape(nb, tt, d // 2)


def _lru_mixer(x3, gmix, w_in, b_in, conv_w, conv_b, wga, bga, wgi, bgi, lam, w_out, b_out, g_ffn):
    nb, seq, d = x3.shape
    w = lam.shape[1]
    tt = T_LRU
    m = nb * tt
    consts = [gmix, w_in, b_in, conv_w, conv_b, wga, bga, wgi, bgi, lam, w_out, b_out, g_ffn]
    const_specs = [pl.BlockSpec(c.shape, (lambda i, n=c.ndim: (0,) * n)) for c in consts]
    blk = pl.BlockSpec((nb, tt, d), lambda i: (0, i, 0))
    return pl.pallas_call(
        _lru_kernel, grid=(seq // tt,),
        in_specs=[blk] + const_specs,
        out_specs=[blk, pl.BlockSpec((nb, tt, d // 2), lambda i: (0, i, 0))],
        out_shape=[jax.ShapeDtypeStruct((nb, seq, d), F32),
                   jax.ShapeDtypeStruct((nb, seq, d // 2), jnp.int32)],
        scratch_shapes=[pltpu.VMEM((nb, tt + SUBLANES, w), F32),
                        pltpu.VMEM((LRU_HEADS, nb * LRU_PITCH, LANES), F32),
                        pltpu.VMEM((LRU_HEADS, nb * LRU_PITCH, LANES), F32),
                        pltpu.VMEM((m, w), F32),
                        pltpu.VMEM((m, w), BF16),
                        pltpu.VMEM((LRU_HEADS, nb, LANES), F32)],
        compiler_params=_cparams(("arbitrary",)), name="rglru_mixer",
    )(x3, *consts)


def _router_kernel(x1_ref, g_ref, wh_ref, wl_ref, br_ref, slot_ref, gate_ref, te_ref, tv_ref,
                   lg_ref, tri_ref, cnt_ref, run_ref, start_ref, *, tile):
    phase = pl.program_id(0)
    j = pl.program_id(1)
    ne = N_EXPERTS
    tr = x1_ref.shape[0]

    @pl.when((phase == 0) & (j == 0))
    def _():
        cnt_ref[...] = jnp.zeros_like(cnt_ref)
        tri_ref[...] = jnp.where(lax.broadcasted_iota(jnp.int32, (tr, tr), 0)
                                 < lax.broadcasted_iota(jnp.int32, (tr, tr), 1),
                                 1.0, 0.0).astype(BF16)

    @pl.when(phase == 0)
    def _():
        x = _rms(x1_ref[...], g_ref[...])
        xh = x.astype(BF16)
        xl = (x - xh.astype(F32)).astype(BF16)
        wh = wh_ref[...]
        nat = _dot(xh, wh) + _dot(xl, wh) + _dot(xh, wl_ref[...])
        lg_ref[j] = nat.T[0:ne, :] + br_ref[...]

    logits = lg_ref[j]
    eidx = lax.broadcasted_iota(jnp.int32, (ne, tr), 0)
    m1 = jnp.max(logits, axis=0, keepdims=True)
    i1 = jnp.min(jnp.where(logits == m1, eidx, ne), axis=0, keepdims=True)
    oh1 = eidx == i1
    rest = jnp.where(oh1, -jnp.inf, logits)
    m2 = jnp.max(rest, axis=0, keepdims=True)
    i2 = jnp.min(jnp.where(rest == m2, eidx, ne), axis=0, keepdims=True)
    oh2 = eidx == i2
    oh = jnp.where(oh1 | oh2, 1.0, 0.0)
    tile_cnt = jnp.sum(oh, axis=1, keepdims=True)

    @pl.when(phase == 0)
    def _():
        cnt_ref[...] += tile_cnt

    @pl.when((phase == 1) & (j == 0))
    def _():
        cnt = cnt_ref[...]
        padded = jnp.ceil(cnt * (1.0 / tile)) * tile
        sub = lax.broadcasted_iota(jnp.int32, (ne, 1), 0)
        start = jnp.zeros((ne, 1), F32)
        for e in range(ne - 1):
            start = start + jnp.where(sub > e, padded[e:e + 1, :], 0.0)
        start_ref[...] = start
        run_ref[...] = jnp.zeros_like(run_ref)
        tile_start = lax.broadcasted_iota(jnp.int32, (ne, LANES), 1).astype(F32) * tile
        owner = jnp.sum(jnp.where(tile_start >= start + padded, 1, 0), axis=0, keepdims=True)
        owner = jnp.minimum(owner, ne - 1)
        esub = lax.broadcasted_iota(jnp.int32, (ne, LANES), 0)
        real = jnp.clip(cnt - (tile_start - start), 0.0, float(tile))
        te_ref[...] = owner
        tv_ref[...] = jnp.sum(jnp.where(esub == owner, real, 0.0), axis=0,
                              keepdims=True).astype(jnp.int32)

    @pl.when(phase == 1)
    def _():
        before = _dot(oh.astype(BF16), tri_ref[...])
        slot_e = start_ref[...] + run_ref[...] + before
        s1 = jnp.sum(jnp.where(oh1, slot_e, 0.0), axis=0, keepdims=True)
        s2 = jnp.sum(jnp.where(oh2, slot_e, 0.0), axis=0, keepdims=True)
        slot_ref[0:1, :] = s1.astype(jnp.int32)
        slot_ref[1:2, :] = s2.astype(jnp.int32)
        e21 = jnp.exp(m2 - m1)
        g1 = 1.0 / (1.0 + e21)
        g2 = e21 * g1
        rowi = lax.broadcasted_iota(jnp.int32, (LANES, tr), 0)
        gmat = jnp.where(rowi == 0, g1, jnp.where(rowi == 1, g2, 0.0))
        gate_ref[...] = gmat.T
        run_ref[...] += tile_cnt


def _router(x1, g_ffn, w_router, b_router, tile):
    t, d = x1.shape
    tr = TR
    nt = t // tr
    wp = jnp.pad(w_router, ((0, 0), (0, LANES - N_EXPERTS)))
    wh = wp.astype(BF16)
    wl = (wp - wh.astype(F32)).astype(BF16)
    const = lambda p, j: (0, 0)
    return pl.pallas_call(
        functools.partial(_router_kernel, tile=tile), grid=(2, nt),
        in_specs=[pl.BlockSpec((tr, d), lambda p, j: (j * (1 - p) + (nt - 1) * p, 0)),
                  pl.BlockSpec((1, d), const),
                  pl.BlockSpec(wh.shape, const), pl.BlockSpec(wl.shape, const),
                  pl.BlockSpec((N_EXPERTS, 1), const)],
        out_specs=[pl.BlockSpec((2, tr), lambda p, j: (0, j * p)),
                   pl.BlockSpec((tr, LANES), lambda p, j: (j * p, 0)),
                   pl.BlockSpec((1, LANES), const), pl.BlockSpec((1, LANES), const)],
        out_shape=[jax.ShapeDtypeStruct((2, t), jnp.int32),
                   jax.ShapeDtypeStruct((t, LANES), F32),
                   jax.ShapeDtypeStruct((1, LANES), jnp.int32),
                   jax.ShapeDtypeStruct((1, LANES), jnp.int32)],
        scratch_shapes=[pltpu.VMEM((nt, N_EXPERTS, tr), F32), pltpu.VMEM((tr, tr), BF16),
                        pltpu.VMEM((N_EXPERTS, 1), F32), pltpu.VMEM((N_EXPERTS, 1), F32),
                        pltpu.VMEM((N_EXPERTS, 1), F32)],
        compiler_params=_cparams(("arbitrary", "arbitrary")), name="moe_router",
    )(x1, g_ffn, wh, wl, b_router.reshape(N_EXPERTS, 1))


def _sc_mesh():
    return plsc.VectorSubcoreMesh(core_axis_name="c", subcore_axis_name="s")


def _sc_dispatch(xn, slots, n_slots):
    t, d = xn.shape
    win = SC_WINDOW
    nwin = t // win
    dp = d // SC_PLANES
    idx = slots.reshape(1, 2 * t)

    @functools.partial(pl.kernel,
                       out_type=jax.ShapeDtypeStruct((SC_PLANES, n_slots, dp), xn.dtype),
                       mesh=_sc_mesh(), scratch_types=[], name="moe_dispatch")
    def run(x_hbm, i_hbm, o_hbm):
        for c in range(SC_PLANES):
            def body(x_vmem, i0_vmem, i1_vmem, c=c):
                pltpu.sync_copy(x_vmem, o_hbm.at[c].at[i0_vmem.at[0]])
                pltpu.sync_copy(x_vmem, o_hbm.at[c].at[i1_vmem.at[0]])

            pltpu.emit_pipeline(
                body, grid=(nwin,),
                in_specs=[pl.BlockSpec((win, dp), lambda i, c=c: (i, c)),
                          pl.BlockSpec((1, win), lambda i: (0, i)),
                          pl.BlockSpec((1, win), lambda i: (0, i + nwin))],
                out_specs=[],
                core_axis_name=("c", "s"),
                dimension_semantics=(pltpu.PARALLEL,),
            )(x_hbm, i_hbm, i_hbm)

    return run(xn, idx)


def _sc_combine(y, slots):
    n2 = slots.size
    dp = y.shape[2]
    win = SC_WINDOW
    idx = slots.reshape(1, n2)

    @functools.partial(pl.kernel, out_type=jax.ShapeDtypeStruct((SC_PLANES, n2, dp), y.dtype),
                       mesh=_sc_mesh(), scratch_types=[], name="moe_combine")
    def run(y_hbm, i_hbm, o_hbm):
        for c in range(SC_PLANES):
            def body(i_vmem, o_vmem, c=c):
                pltpu.sync_copy(y_hbm.at[c].at[i_vmem.at[0]], o_vmem)

            pltpu.emit_pipeline(
                body, grid=(n2 // win,),
                in_specs=[pl.BlockSpec((1, win), lambda i: (0, i))],
                out_specs=[pl.BlockSpec((win, dp), lambda i: (i, 0))],
                core_axis_name=("c", "s"),
                dimension_semantics=(pltpu.PARALLEL,),
            )(i_hbm, o_hbm.at[c])

    return run(y, idx)


def _moe_ffn_kernel(te_ref, tv_ref, xs_ref, w1_hbm, w3_hbm, w2_hbm, y_ref,
                    xb_ref, acc_ref, w1b, w3b, w2b, wb1_ref, wb3_ref, wb2_ref, sem, *, layer, tf):
    i = pl.program_id(0)
    n_tiles = pl.num_programs(0)
    valid = tv_ref[i]
    tm, dp = xs_ref.shape[1:]
    ts = TSUB_MOE
    nc = w1_hbm.shape[3] // tf

    def copies(tile, c, slot):
        return _weight_chunk_copies(w1_hbm, w3_hbm, w2_hbm, (layer, te_ref[tile]), c, tf, slot,
                                    w1b, w3b, w2b, sem)

    @pl.when((i == 0) & (valid > 0))
    def _():
        for cp in copies(0, 0, 0):
            cp.start()

    rows = lax.broadcasted_iota(jnp.int32, (tm, 1), 0)
    for c, piece in enumerate(_unpack_planes(xs_ref)):
        xb_ref[:, c * dp:(c + 1) * dp] = jnp.where(rows < valid, piece, 0.0).astype(BF16)
    acc_ref[...] = jnp.zeros_like(acc_ref)

    def rows_block(lo, n, w1, w3, w2):
        acc_ref[lo:lo + n, :] += _swiglu_chunk(xb_ref[lo:lo + n, :], w1, w3, w2)

    @pl.when(valid > 0)
    def _():
        nxt_tile = jnp.minimum(i + 1, n_tiles - 1)
        next_tile_live = (i + 1 < n_tiles) & (tv_ref[nxt_tile] > 0)

        def chunk(c, carry):
            slot = lax.rem(i * nc + c, 2)
            last = c + 1 == nc

            @pl.when(jnp.logical_not(last) | next_tile_live)
            def _():
                for cp in copies(jnp.where(last, nxt_tile, i), jnp.where(last, 0, c + 1), 1 - slot):
                    cp.start()

            for cp in copies(i, c, slot):
                cp.wait()

            @pl.when(valid > tm - ts)
            def _():
                rows_block(0, tm, w1b[slot].astype(BF16), w3b[slot].astype(BF16),
                           w2b[slot].astype(BF16))

            @pl.when(valid <= tm - ts)
            def _():
                wb1_ref[...] = w1b[slot].astype(BF16)
                wb3_ref[...] = w3b[slot].astype(BF16)
                wb2_ref[...] = w2b[slot].astype(BF16)
                weights = lambda: (wb1_ref[...], wb3_ref[...], wb2_ref[...])

                @pl.when(valid <= ts)
                def _():
                    rows_block(0, ts, *weights())

                @pl.when(valid > ts)
                def _():
                    rows_block(0, 2 * ts, *weights())

                @pl.when(valid > 2 * ts)
                def _():
                    rows_block(2 * ts, ts, *weights())

            return carry

        lax.fori_loop(0, nc, chunk, 0)

    packed = _pack_pairs(acc_ref[...])
    for c in range(SC_PLANES):
        y_ref[c] = packed[:, c * dp:(c + 1) * dp]


def _moe_ffn(xs, tile_expert, tile_valid, w1, w3, w2, layer):
    planes, ns, dp = xs.shape
    d = 2 * planes * dp
    tm, tf = TM_MOE, TF_MOE
    assert tm == 4 * TSUB_MOE
    hbm = pl.BlockSpec(memory_space=pl.ANY)
    grid_spec = pltpu.PrefetchScalarGridSpec(
        num_scalar_prefetch=2, grid=(ns // tm,),
        in_specs=[pl.BlockSpec((planes, tm, dp), lambda i, te, tv: (0, i, 0)), hbm, hbm, hbm],
        out_specs=pl.BlockSpec((planes, tm, dp), lambda i, te, tv: (0, i, 0)),
        scratch_shapes=[pltpu.VMEM((tm, d), BF16), pltpu.VMEM((tm, d), F32),
                        pltpu.VMEM((2, d, tf), F32), pltpu.VMEM((2, d, tf), F32),
                        pltpu.VMEM((2, tf, d), F32),
                        pltpu.VMEM((d, tf), BF16), pltpu.VMEM((d, tf), BF16),
                        pltpu.VMEM((tf, d), BF16), pltpu.SemaphoreType.DMA((3, 2))])
    return pl.pallas_call(
        functools.partial(_moe_ffn_kernel, layer=layer, tf=tf), grid_spec=grid_spec,
        out_shape=jax.ShapeDtypeStruct((planes, ns, dp), jnp.int32),
        compiler_params=_cparams(("arbitrary",)), name="moe_ffn",
    )(tile_expert, tile_valid, xs, w1, w3, w2)


def _combine_kernel(x1_ref, gt_ref, y0_ref, y1_ref, o_ref):
    o_ref[...] = _gated_sum(x1_ref, gt_ref, y0_ref, y1_ref)


def _combine(x1, gates_tok, yg):
    t, d = x1.shape
    tm = 512
    nt = t // tm
    row = lambda i: (i, 0)
    return pl.pallas_call(
        _combine_kernel, grid=(nt,),
        in_specs=[pl.BlockSpec((tm, d), row), pl.BlockSpec((tm, LANES), row),
                  pl.BlockSpec((SC_PLANES, tm, d // (2 * SC_PLANES)), lambda i: (0, i, 0)),
                  pl.BlockSpec((SC_PLANES, tm, d // (2 * SC_PLANES)), lambda i: (0, i + nt, 0))],
        out_specs=pl.BlockSpec((tm, d), row),
        out_shape=jax.ShapeDtypeStruct((t, d), F32),
        compiler_params=_cparams(("parallel",)), name="moe_combine_residual",
    )(x1, gates_tok, yg, yg)


def _mla_weights(w_in, w_qb, w_kvb, g_qn, g_kn):
    d = w_in.shape[0]
    h = MLA_HEADS
    lat = Q_LORA + KV_LORA
    win = jnp.zeros((d, lat + HEAD_PAD), F32)
    win = win.at[:, :lat].set(w_in[:, :lat])
    win = win.at[:, lat + LANES:lat + LANES + QK_ROPE].set(w_in[:, lat:])
    wq = w_qb.reshape(Q_LORA, h, QK_HEAD)
    wq = jnp.pad(wq, ((0, 0), (0, 0), (0, HEAD_PAD - QK_HEAD))).reshape(Q_LORA, h * HEAD_PAD)
    wkv = w_kvb.reshape(KV_LORA, h, QK_NOPE + V_HEAD)
    wk = wkv[:, :, :QK_NOPE].reshape(KV_LORA, h * QK_NOPE)
    wvt = wkv[:, :, QK_NOPE:].reshape(KV_LORA, h * V_HEAD).T
    scale = math.log2(math.e) / math.sqrt(QK_HEAD)
    gq = jnp.broadcast_to((g_qn * scale).reshape(QK_HEAD, 1), (QK_HEAD, TQ))
    gk = jnp.pad(g_kn, (0, HEAD_PAD - QK_HEAD)).reshape(1, HEAD_PAD)
    return (win.astype(BF16), wq.T.astype(BF16), wk.astype(BF16), wvt.astype(BF16), gq, gk)


def kernel(x, positions, norm_mix, norm_ffn, mla_w_in, mla_g_qa, mla_w_qb, mla_g_kva, mla_w_kvb, mla_g_qn, mla_g_kn, mla_w_o, lru_w_in, lru_b_in, lru_conv_w, lru_conv_b, lru_w_gate_a, lru_b_gate_a, lru_w_gate_i, lru_b_gate_i, lru_lambda, lru_w_out, lru_b_out, ffn_w1, ffn_w3, ffn_w2, moe_w_router, moe_b_router, moe_w1, moe_w3, moe_w2):
    batch, seq, d = x.shape
    t = batch * seq
    depth = norm_mix.shape[0]
    n_slots = 2 * t + N_EXPERTS * TM_MOE
    n_tiles = n_slots // TM_MOE
    row = lambda v: v.reshape(1, -1)

    cs, cst = _rope_tables(positions)
    parts = (x.reshape(t, d),)
    for i in range(depth):
        j = i // 2
        if i % 2 == 0:
            w_in, w_qbt, w_kb, w_vt, g_qn, g_kn = _mla_weights(
                mla_w_in[j], mla_w_qb[j], mla_w_kvb[j], mla_g_qn[j], mla_g_kn[j])
            xr, q, k, vt = _mla_qkv(parts, cs, cst, row(norm_mix[i]), w_in, row(mla_g_qa[j]),
                                    w_qbt, row(mla_g_kva[j]), w_kb, w_vt, g_qn, g_kn)
            x1, xn = _attention(q, k, vt, xr, mla_w_o[j].astype(BF16), row(norm_ffn[i]),
                                batch, seq)
            parts = (_dense_ffn(xn, x1, ffn_w1, ffn_w3, ffn_w2, j),)
        else:
            (xr,) = parts
            x1, xn = _lru_mixer(
                xr.reshape(batch, seq, d), row(norm_mix[i]), lru_w_in[j].astype(BF16),
                row(lru_b_in[j]), lru_conv_w[j], row(lru_conv_b[j]),
                lru_w_gate_a[j].astype(BF16), lru_b_gate_a[j], lru_w_gate_i[j].astype(BF16),
                lru_b_gate_i[j], row(lru_lambda[j]), lru_w_out[j].astype(BF16),
                row(lru_b_out[j]), row(norm_ffn[i]))
            x1 = x1.reshape(t, d)
            xn = xn.reshape(t, d // 2)
            slots, gates_tok, te, tv = _router(x1, row(norm_ffn[i]), moe_w_router[j],
                                               moe_b_router[j], TM_MOE)
            xs = _sc_dispatch(xn, slots, n_slots)
            y = _moe_ffn(xs, te[0, :n_tiles], tv[0, :n_tiles], moe_w1, moe_w3, moe_w2, j)
            yg = _sc_combine(y, slots)
            parts = (x1, gates_tok, yg)
    if len(parts) == 3:
        out = _combine(*parts)
    else:
        out = parts[0]
    return out.reshape(batch, seq, d)
```

```python
import functools
import math

import numpy as np
import jax
import jax.numpy as jnp
from jax import lax
from jax.experimental import pallas as pl
from jax.experimental.pallas import tpu as pltpu
from jax.experimental.pallas import tpu_sc as plsc

F32 = jnp.float32
BF16 = jnp.bfloat16

NORM_EPS = 1e-6
CHUNK = 64
MLA_HEADS = 8
QK_NOPE = 128
QK_ROPE = 64
QK_HEAD = QK_NOPE + QK_ROPE
V_HEAD = 128
Q_LORA = 384
KV_LORA = 256
ROPE_BASE = 10000.0
LRU_HEADS = 8
CONV_WIDTH = 4
LRU_C = 8.0
N_EXPERTS = 8

LANES = 128
SUBLANES = 8
HEAD_PAD = 2 * LANES
VMEM_LIMIT = 56 * 1024 * 1024

TQ = 256
TM_QKV = 2 * TQ
ATTN_SUM_ROWS = 16
TM_FFN = 1024
TF_FFN = 512
T_LRU = 64
LRU_PITCH = T_LRU + SUBLANES
TR = 1024
TM_MOE = 1024
TSUB_MOE = 256
TF_MOE = 512
SC_WINDOW = 128
SC_PLANES = 2


def _cparams(sem):
    return pltpu.CompilerParams(dimension_semantics=sem, vmem_limit_bytes=VMEM_LIMIT)


def _rms(x, g):
    return x * lax.rsqrt(jnp.mean(x * x, axis=-1, keepdims=True) + NORM_EPS) * g


def _dot(a, b):
    return jnp.dot(a, b, preferred_element_type=F32)


def _dot_nt(a, b):
    return lax.dot_general(a, b, (((1,), (1,)), ((), ())), preferred_element_type=F32)


def _pack_pairs(x):
    n = x.shape[1] // 2
    bits = lax.bitcast_convert_type(x.astype(BF16).astype(F32), jnp.int32)
    return bits[:, :n] | lax.shift_right_logical(bits[:, n:], 16)


def _unpack_pairs(word):
    hi = lax.bitcast_convert_type(word & jnp.int32(-65536), F32)
    lo = lax.bitcast_convert_type(lax.shift_left(word, 16), F32)
    return hi, lo


def _unpack_planes(ref):
    halves = [_unpack_pairs(ref[c]) for c in range(SC_PLANES)]
    return [h for h, _ in halves] + [l for _, l in halves]


def _rope_kernel(pos_ref, inv_ref, cs_ref, cst_ref):
    ang = inv_ref[...] * pos_ref[...].astype(F32)
    half = QK_ROPE // 2
    ct = jnp.cos(ang)
    st = jnp.sin(ang)
    cst_ref[0:half, :] = ct[0:half, :]
    cst_ref[half:QK_ROPE, :] = st[0:half, :]
    pad = jnp.zeros((LANES - QK_ROPE, ang.shape[1]), F32)
    c = jnp.concatenate([ct, pad], axis=0).T
    s = jnp.concatenate([st, pad], axis=0).T
    lane = lax.broadcasted_iota(jnp.int32, c.shape, 1)
    cs_ref[:, 0:LANES] = jnp.where(lane < QK_ROPE, c, 0.0)
    cs_ref[:, LANES:2 * LANES] = jnp.where(lane < half, -s, 0.0)
    cs_ref[:, 2 * LANES:3 * LANES] = jnp.where((lane >= half) & (lane < QK_ROPE), s, 0.0)


def _rope_tables(positions):
    t = positions.size
    tm = 512
    inv = 1.0 / (ROPE_BASE ** (np.arange(0, QK_ROPE, 2, dtype=np.float32) / QK_ROPE))
    inv_col = np.concatenate([inv, inv]).reshape(QK_ROPE, 1).astype(np.float32)
    return pl.pallas_call(
        _rope_kernel,
        grid=(t // tm,),
        in_specs=[pl.BlockSpec((1, tm), lambda i: (0, i)),
                  pl.BlockSpec((QK_ROPE, 1), lambda i: (0, 0))],
        out_specs=[pl.BlockSpec((tm, 3 * LANES), lambda i: (i, 0)),
                   pl.BlockSpec((QK_ROPE, tm), lambda i: (0, i))],
        out_shape=[jax.ShapeDtypeStruct((t, 3 * LANES), F32),
                   jax.ShapeDtypeStruct((QK_ROPE, t), F32)],
        compiler_params=_cparams(("parallel",)),
        name="rope_tables",
    )(positions.reshape(1, t), jnp.asarray(inv_col))


def _rope_rot(x, cs):
    c = cs[:, 0:LANES]
    sa = cs[:, LANES:2 * LANES]
    sb = cs[:, 2 * LANES:3 * LANES]
    return (x * c + pltpu.roll(x, LANES - QK_ROPE // 2, 1) * sa
            + pltpu.roll(x, QK_ROPE // 2, 1) * sb)


def _mla_latent(x, gmix_ref, win_ref):
    return _dot(_rms(x, gmix_ref[...]).astype(BF16), win_ref[...])


def _mla_expand(hcat, gqa_ref, wqb_ref, gkva_ref, wkb_ref, wvt_ref):
    cq = hcat[:, :Q_LORA]
    ckv = hcat[:, Q_LORA:Q_LORA + KV_LORA]
    kr = hcat[:, Q_LORA + KV_LORA + LANES:Q_LORA + KV_LORA + 2 * LANES]
    qft = _dot_nt(wqb_ref[...], _rms(cq, gqa_ref[...]).astype(BF16))
    ckvn = _rms(ckv, gkva_ref[...]).astype(BF16)
    kf = _dot(ckvn, wkb_ref[...])
    vt = _dot_nt(wvt_ref[...], ckvn).astype(BF16)
    return qft, kf, kr, vt


def _mla_q_heads(qft, cst, gqn_ref, qt_ref, s):
    half = QK_ROPE // 2
    c, sn = cst[0:half, :], cst[half:QK_ROPE, :]
    g = gqn_ref[...]
    zero = jnp.zeros((HEAD_PAD - QK_HEAD, qft.shape[1]), BF16)
    for h in range(MLA_HEADS):
        blk = qft[h * HEAD_PAD:h * HEAD_PAD + QK_HEAD, :]
        ss = jnp.sum(blk * blk, axis=0, keepdims=True)
        inv = lax.rsqrt(ss * (1.0 / QK_HEAD) + NORM_EPS)
        y = blk * g * inv
        x1, x2 = y[QK_NOPE:QK_NOPE + half, :], y[QK_NOPE + half:QK_HEAD, :]
        base = h * HEAD_PAD
        qt_ref[s, base:base + QK_NOPE, :] = y[0:QK_NOPE, :].astype(BF16)
        qt_ref[s, base + QK_NOPE:base + QK_NOPE + half, :] = (x1 * c - x2 * sn).astype(BF16)
        qt_ref[s, base + QK_NOPE + half:base + QK_HEAD, :] = (x1 * sn + x2 * c).astype(BF16)
        qt_ref[s, base + QK_HEAD:base + HEAD_PAD, :] = zero


def _mla_k_heads(kf, kr, cs, gkn_ref, k_ref, rows):
    gkn_n, gkn_r = gkn_ref[:, 0:LANES], gkn_ref[:, LANES:2 * LANES]
    kr_ss = jnp.sum(kr * kr, axis=-1, keepdims=True)
    kr_rot = _rope_rot(kr * gkn_r, cs)
    for h in range(MLA_HEADS):
        kn = kf[:, h * LANES:(h + 1) * LANES]
        ssk = jnp.sum(kn * kn, axis=-1, keepdims=True) + kr_ss
        invk = lax.rsqrt(ssk * (1.0 / QK_HEAD) + NORM_EPS)
        k_ref[rows, h * HEAD_PAD:h * HEAD_PAD + LANES] = (kn * invk * gkn_n).astype(BF16)
        k_ref[rows, h * HEAD_PAD + LANES:(h + 1) * HEAD_PAD] = (kr_rot * invk).astype(BF16)


def _mla_qkv_body(x, cs_ref, cst_ref, gmix_ref, win_ref, gqa_ref, wqb_ref, gkva_ref, wkb_ref,
                  wvt_ref, gqn_ref, gkn_ref, qt_ref, k_ref, vt_ref):
    nsub = x.shape[0] // TQ
    rows = [slice(s * TQ, (s + 1) * TQ) for s in range(nsub)]
    hcats = [_mla_latent(x[r], gmix_ref, win_ref) for r in rows]
    mids = [_mla_expand(hc, gqa_ref, wqb_ref, gkva_ref, wkb_ref, wvt_ref) for hc in hcats]
    for s, (qft, kf, kr, vt) in enumerate(mids):
        vt_ref[s] = vt
    for s, (r, (qft, kf, kr, vt)) in enumerate(zip(rows, mids)):
        _mla_q_heads(qft, cst_ref[:, r], gqn_ref, qt_ref, s)
        _mla_k_heads(kf, kr, cs_ref[r, :], gkn_ref, k_ref, r)


def _gated_sum(x1_ref, gt_ref, y0_ref, y1_ref):
    gt = gt_ref[...]
    g0, g1 = gt[:, 0:1], gt[:, 1:2]
    y = jnp.concatenate([g0 * a + g1 * b
                         for a, b in zip(_unpack_planes(y0_ref), _unpack_planes(y1_ref))], axis=1)
    return x1_ref[...] + y


def _mla_qkv_kernel(x_ref, *rest):
    _mla_qkv_body(x_ref[...], *rest)


def _mla_qkv_combine_kernel(x1_ref, gt_ref, y0_ref, y1_ref, *rest):
    *mid, xo_ref, q_ref, k_ref, v_ref = rest
    x = _gated_sum(x1_ref, gt_ref, y0_ref, y1_ref)
    xo_ref[...] = x
    _mla_qkv_body(x, *mid, q_ref, k_ref, v_ref)


def _mla_qkv(x_parts, cs, cst, gmix, w_in, g_qa, w_qbt, g_kva, w_kb, w_vt, g_qn, g_kn):
    t, d = x_parts[0].shape
    tm = TM_QKV
    nsub = tm // TQ
    nt = t // tm
    row = lambda i: (i, 0)
    const = lambda i: (0, 0)
    h = MLA_HEADS
    weights = [gmix, w_in, g_qa, w_qbt, g_kva, w_kb, w_vt, g_qn, g_kn]
    w_specs = [pl.BlockSpec(w.shape, const) for w in weights]
    qkv_specs = [pl.BlockSpec((nsub, h * HEAD_PAD, TQ), lambda i: (i, 0, 0)),
                 pl.BlockSpec((tm, h * HEAD_PAD), row),
                 pl.BlockSpec((nsub, h * V_HEAD, TQ), lambda i: (i, 0, 0))]
    qkv_shapes = [jax.ShapeDtypeStruct((t // TQ, h * HEAD_PAD, TQ), BF16),
                  jax.ShapeDtypeStruct((t, h * HEAD_PAD), BF16),
                  jax.ShapeDtypeStruct((t // TQ, h * V_HEAD, TQ), BF16)]
    cs_specs = [pl.BlockSpec((tm, 3 * LANES), row), pl.BlockSpec((QK_ROPE, tm), lambda i: (0, i))]
    if len(x_parts) == 1:
        q, k, v = pl.pallas_call(
            _mla_qkv_kernel, grid=(nt,),
            in_specs=[pl.BlockSpec((tm, d), row)] + cs_specs + w_specs,
            out_specs=qkv_specs, out_shape=qkv_shapes,
            compiler_params=_cparams(("parallel",)), name="mla_qkv",
        )(x_parts[0], cs, cst, *weights)
        return x_parts[0], q, k, v
    x1, gates_tok, yg = x_parts
    x, q, k, v = pl.pallas_call(
        _mla_qkv_combine_kernel, grid=(nt,),
        in_specs=[pl.BlockSpec((tm, d), row), pl.BlockSpec((tm, LANES), row),
                  pl.BlockSpec((SC_PLANES, tm, d // (2 * SC_PLANES)), lambda i: (0, i, 0)),
                  pl.BlockSpec((SC_PLANES, tm, d // (2 * SC_PLANES)), lambda i: (0, i + nt, 0)),
                  ] + cs_specs + w_specs,
        out_specs=[pl.BlockSpec((tm, d), row)] + qkv_specs,
        out_shape=[jax.ShapeDtypeStruct((t, d), F32)] + qkv_shapes,
        compiler_params=_cparams(("parallel",)), name="mla_qkv_combine",
    )(x1, gates_tok, yg, yg, cs, cst, *weights)
    return x, q, k, v


def _attn_kernel(q_ref, k_ref, vt_ref, x_ref, wo_ref, g_ref, x1_ref, xn_ref,
                 s_ref, m_ref, acc_ref, o_ref):
    i = pl.program_id(1)
    ones = jnp.ones((ATTN_SUM_ROWS, TQ), BF16)

    def scores(h, j):
        off = pl.multiple_of(j * TQ, TQ)
        hs = slice(h * HEAD_PAD, (h + 1) * HEAD_PAD)
        s_ref[h] = _dot(k_ref[pl.ds(off, TQ), hs], q_ref[0, hs, :])

    def weighted(h, j, p):
        lhs = jnp.concatenate([vt_ref[j, h * V_HEAD:(h + 1) * V_HEAD, :], ones], axis=0)
        return _dot(lhs, p.astype(BF16))

    for h in range(MLA_HEADS):
        scores(h, i)
    kc = lax.broadcasted_iota(jnp.int32, (TQ, TQ), 0) // CHUNK
    qc = lax.broadcasted_iota(jnp.int32, (TQ, TQ), 1) // CHUNK
    for h in range(MLA_HEADS):
        st = jnp.where(kc <= qc, s_ref[h], -jnp.inf)
        m = jnp.max(st, axis=0, keepdims=True)
        acc_ref[h] = weighted(h, i, jnp.exp2(st - m))
        m_ref[h] = m

    def update(h, j):
        st = s_ref[h]
        m_old = m_ref[h]
        m_new = jnp.maximum(m_old, jnp.max(st, axis=0, keepdims=True))
        acc_ref[h] = jnp.exp2(m_old - m_new) * acc_ref[h] + weighted(h, j, jnp.exp2(st - m_new))
        m_ref[h] = m_new

    def full_tile(j, carry):
        for h in range(MLA_HEADS):
            scores(h, j)
        for h in range(MLA_HEADS):
            update(h, j)
        return carry

    lax.fori_loop(0, i, full_tile, 0)
    for h in range(MLA_HEADS):
        acc = acc_ref[h]
        o_ref[:, h * V_HEAD:(h + 1) * V_HEAD] = (
            acc[0:V_HEAD, :] / acc[V_HEAD:V_HEAD + 1, :]).T.astype(BF16)

    x1 = x_ref[...] + _dot(o_ref[...], wo_ref[...])
    x1_ref[...] = x1
    xn_ref[...] = _rms(x1, g_ref[...]).astype(BF16)


def _attention(q, k, vt, x, w_o, g_ffn, batch, seq):
    h = MLA_HEADS
    nq = seq // TQ
    t, d = x.shape
    tile = lambda b, i: (b * nq + i, 0)
    const = lambda b, i: (0, 0)
    return pl.pallas_call(
        _attn_kernel, grid=(batch, nq),
        in_specs=[pl.BlockSpec((1, h * HEAD_PAD, TQ), lambda b, i: (b * nq + i, 0, 0)),
                  pl.BlockSpec((seq, h * HEAD_PAD), lambda b, i: (b, 0)),
                  pl.BlockSpec((nq, h * V_HEAD, TQ), lambda b, i: (b, 0, 0)),
                  pl.BlockSpec((TQ, d), tile), pl.BlockSpec(w_o.shape, const),
                  pl.BlockSpec((1, d), const)],
        out_specs=[pl.BlockSpec((TQ, d), tile), pl.BlockSpec((TQ, d), tile)],
        out_shape=[jax.ShapeDtypeStruct((t, d), F32), jax.ShapeDtypeStruct((t, d), BF16)],
        scratch_shapes=[pltpu.VMEM((h, TQ, TQ), F32),
                        pltpu.VMEM((h, 1, TQ), F32),
                        pltpu.VMEM((h, V_HEAD + ATTN_SUM_ROWS, TQ), F32),
                        pltpu.VMEM((TQ, h * V_HEAD), BF16)],
        compiler_params=_cparams(("parallel", "arbitrary")), name="attention",
    )(q, k, vt, x, w_o, g_ffn)


def _swiglu_chunk(x, w1, w3, w2):
    a = _dot(x, w1)
    b = _dot(x, w3)
    hid = (a * jax.nn.sigmoid(a)) * b
    return _dot(hid.astype(BF16), w2)


def _weight_chunk_copies(w1_hbm, w3_hbm, w2_hbm, lead, c, tf, slot, w1b, w3b, w2b, sem):
    cols = pl.ds(pl.multiple_of(c * tf, tf), tf)
    return (pltpu.make_async_copy(w1_hbm.at[lead + (slice(None), cols)], w1b.at[slot], sem.at[0, slot]),
            pltpu.make_async_copy(w3_hbm.at[lead + (slice(None), cols)], w3b.at[slot], sem.at[1, slot]),
            pltpu.make_async_copy(w2_hbm.at[lead + (cols, slice(None))], w2b.at[slot], sem.at[2, slot]))


def _ffn_kernel(xn_ref, x1_ref, w1_hbm, w3_hbm, w2_hbm, o_ref, w1b, w3b, w2b, sem, *, layer, tf):
    i = pl.program_id(0)
    n_tiles = pl.num_programs(0)
    ff = w1_hbm.shape[2]
    chunks = [(lo, min(tf, ff - lo)) for lo in range(0, ff, tf)]
    assert len(chunks) % 2 == 0

    def copies(c):
        lo, width = chunks[c]
        slot = c % 2
        cols = pl.ds(lo, width)
        return (pltpu.make_async_copy(w1_hbm.at[layer, :, cols], w1b.at[slot, :, 0:width],
                                      sem.at[0, slot]),
                pltpu.make_async_copy(w3_hbm.at[layer, :, cols], w3b.at[slot, :, 0:width],
                                      sem.at[1, slot]),
                pltpu.make_async_copy(w2_hbm.at[layer, cols, :], w2b.at[slot, 0:width, :],
                                      sem.at[2, slot]))

    @pl.when(i == 0)
    def _():
        for cp in copies(0):
            cp.start()

    o_ref[...] = x1_ref[...]
    for c, (lo, width) in enumerate(chunks):
        if c + 1 < len(chunks):
            for cp in copies(c + 1):
                cp.start()
        else:
            @pl.when(i + 1 < n_tiles)
            def _():
                for cp in copies(0):
                    cp.start()
        for cp in copies(c):
            cp.wait()
        slot = c % 2
        o_ref[...] += _swiglu_chunk(xn_ref[...], w1b[slot, :, 0:width].astype(BF16),
                                    w3b[slot, :, 0:width].astype(BF16),
                                    w2b[slot, 0:width, :].astype(BF16))


def _dense_ffn(xn, x1, w1, w3, w2, layer):
    t, d = x1.shape
    tm, tf = TM_FFN, TF_FFN
    row = lambda i: (i, 0)
    hbm = pl.BlockSpec(memory_space=pl.ANY)
    return pl.pallas_call(
        functools.partial(_ffn_kernel, layer=layer, tf=tf), grid=(t // tm,),
        in_specs=[pl.BlockSpec((tm, d), row), pl.BlockSpec((tm, d), row), hbm, hbm, hbm],
        out_specs=pl.BlockSpec((tm, d), row),
        out_shape=jax.ShapeDtypeStruct((t, d), F32),
        scratch_shapes=[pltpu.VMEM((2, d, tf), F32), pltpu.VMEM((2, d, tf), F32),
                        pltpu.VMEM((2, tf, d), F32), pltpu.SemaphoreType.DMA((3, 2))],
        compiler_params=_cparams(("arbitrary",)), name="dense_ffn",
    )(xn, x1, w1, w3, w2)


def _lru_kernel(x_ref, gmix_ref, win_ref, bin_ref, cw_ref, cb_ref, wga_ref, bga_ref,
                wgi_ref, bgi_ref, lam_ref, wout_ref, bout_ref, gffn_ref,
                x1_ref, xn_ref,
                xpad_ref, a_ref, u_ref, gate_ref, yg_ref, h_ref):
    nb, tt, d = x_ref.shape
    w = lam_ref.shape[1]
    m = nb * tt

    @pl.when(pl.program_id(0) == 0)
    def _():
        h_ref[...] = jnp.zeros_like(h_ref)
        xpad_ref[:, 0:SUBLANES, :] = jnp.zeros((nb, SUBLANES, w), F32)

    x = x_ref[...].reshape(m, d)
    xn = _rms(x, gmix_ref[...]).astype(BF16)
    hcat = _dot(xn, win_ref[...]) + bin_ref[...]
    gate_ref[...] = jax.nn.gelu(hcat[:, :w], approximate=True)
    xpad_ref[:, SUBLANES:, :] = hcat[:, w:].reshape(nb, tt, w)

    xc = cb_ref[...].reshape(1, 1, w) + jnp.zeros((nb, tt, w), F32)
    for j in range(CONV_WIDTH):
        lo = SUBLANES - (CONV_WIDTH - 1) + j
        xc = xc + xpad_ref[:, lo:lo + tt, :] * cw_ref[j:j + 1, :].reshape(1, 1, w)
    xpad_ref[:, 0:SUBLANES, :] = xpad_ref[:, tt:tt + SUBLANES, :]
    xc = xc.reshape(m, w)

    lam = lam_ref[...]
    log_sig = jnp.minimum(lam, 0.0) - jnp.log1p(jnp.exp(-jnp.abs(lam)))
    for hh in range(LRU_HEADS):
        sl = slice(hh * LANES, (hh + 1) * LANES)
        xh = xc[:, sl]
        xhb = xh.astype(BF16)
        r = jax.nn.sigmoid(_dot(xhb, wga_ref[hh]) + bga_ref[hh:hh + 1, :])
        ig = jax.nn.sigmoid(_dot(xhb, wgi_ref[hh]) + bgi_ref[hh:hh + 1, :])
        log_a = LRU_C * r * log_sig[:, sl]
        a = jnp.exp(log_a)
        z = 1.0 - a * a
        u = (z * lax.rsqrt(jnp.maximum(z, 1e-30))) * (ig * xh)
        for b in range(nb):
            a_ref[hh, b * LRU_PITCH:b * LRU_PITCH + tt, :] = a[b * tt:(b + 1) * tt, :]
            u_ref[hh, b * LRU_PITCH:b * LRU_PITCH + tt, :] = u[b * tt:(b + 1) * tt, :]

    def scan_step(t, hs):
        new = []
        for hh in range(LRU_HEADS):
            rows = pl.ds(t, nb, stride=LRU_PITCH)
            hv = a_ref[hh, rows, :] * hs[hh] + u_ref[hh, rows, :]
            u_ref[hh, rows, :] = hv
            new.append(hv)
        return tuple(new)

    hs = lax.fori_loop(0, tt, scan_step, tuple(h_ref[hh] for hh in range(LRU_HEADS)), unroll=4)
    for hh in range(LRU_HEADS):
        h_ref[hh] = hs[hh]

    for hh in range(LRU_HEADS):
        sl = slice(hh * LANES, (hh + 1) * LANES)
        for b in range(nb):
            y = u_ref[hh, b * LRU_PITCH:b * LRU_PITCH + tt, :]
            yg_ref[b * tt:(b + 1) * tt, sl] = (y * gate_ref[b * tt:(b + 1) * tt, sl]).astype(BF16)

    x1 = x + _dot(yg_ref[...], wout_ref[...]) + bout_ref[...]
    x1_ref[...] = x1.reshape(nb, tt, d)
    xn_ref[...] = _pack_pairs(_rms(x1, gffn_ref[...])).reshape(nb, tt, d // 2)


def _lru_mixer(x3, gmix, w_in, b_in, conv_w, conv_b, wga, bga, wgi, bgi, lam, w_out, b_out, g_ffn):
    nb, seq, d = x3.shape
    w = lam.shape[1]
    tt = T_LRU
    m = nb * tt
    consts = [gmix, w_in, b_in, conv_w, conv_b, wga, bga, wgi, bgi, lam, w_out, b_out, g_ffn]
    const_specs = [pl.BlockSpec(c.shape, (lambda i, n=c.ndim: (0,) * n)) for c in consts]
    blk = pl.BlockSpec((nb, tt, d), lambda i: (0, i, 0))
    return pl.pallas_call(
        _lru_kernel, grid=(seq // tt,),
        in_specs=[blk] + const_specs,
        out_specs=[blk, pl.BlockSpec((nb, tt, d // 2), lambda i: (0, i, 0))],
        out_shape=[jax.ShapeDtypeStruct((nb, seq, d), F32),
                   jax.ShapeDtypeStruct((nb, seq, d // 2), jnp.int32)],
        scratch_shapes=[pltpu.VMEM((nb, tt + SUBLANES, w), F32),
                        pltpu.VMEM((LRU_HEADS, nb * LRU_PITCH, LANES), F32),
                        pltpu.VMEM((LRU_HEADS, nb * LRU_PITCH, LANES), F32),
                        pltpu.VMEM((m, w), F32),
                        pltpu.VMEM((m, w), BF16),
                        pltpu.VMEM((LRU_HEADS, nb, LANES), F32)],
        compiler_params=_cparams(("arbitrary",)), name="rglru_mixer",
    )(x3, *consts)


def _router_kernel(x1_ref, g_ref, wh_ref, wl_ref, br_ref, slot_ref, gate_ref, te_ref, tv_ref,
                   lg_ref, tri_ref, cnt_ref, run_ref, start_ref, *, tile):
    phase = pl.program_id(0)
    j = pl.program_id(1)
    ne = N_EXPERTS
    tr = x1_ref.shape[0]

    @pl.when((phase == 0) & (j == 0))
    def _():
        cnt_ref[...] = jnp.zeros_like(cnt_ref)
        tri_ref[...] = jnp.where(lax.broadcasted_iota(jnp.int32, (tr, tr), 0)
                                 < lax.broadcasted_iota(jnp.int32, (tr, tr), 1),
                                 1.0, 0.0).astype(BF16)

    @pl.when(phase == 0)
    def _():
        x = _rms(x1_ref[...], g_ref[...])
        xh = x.astype(BF16)
        xl = (x - xh.astype(F32)).astype(BF16)
        wh = wh_ref[...]
        nat = _dot(xh, wh) + _dot(xl, wh) + _dot(xh, wl_ref[...])
        lg_ref[j] = nat.T[0:ne, :] + br_ref[...]

    logits = lg_ref[j]
    eidx = lax.broadcasted_iota(jnp.int32, (ne, tr), 0)
    m1 = jnp.max(logits, axis=0, keepdims=True)
    i1 = jnp.min(jnp.where(logits == m1, eidx, ne), axis=0, keepdims=True)
    oh1 = eidx == i1
    rest = jnp.where(oh1, -jnp.inf, logits)
    m2 = jnp.max(rest, axis=0, keepdims=True)
    i2 = jnp.min(jnp.where(rest == m2, eidx, ne), axis=0, keepdims=True)
    oh2 = eidx == i2
    oh = jnp.where(oh1 | oh2, 1.0, 0.0)
    tile_cnt = jnp.sum(oh, axis=1, keepdims=True)

    @pl.when(phase == 0)
    def _():
        cnt_ref[...] += tile_cnt

    @pl.when((phase == 1) & (j == 0))
    def _():
        cnt = cnt_ref[...]
        padded = jnp.ceil(cnt * (1.0 / tile)) * tile
        sub = lax.broadcasted_iota(jnp.int32, (ne, 1), 0)
        start = jnp.zeros((ne, 1), F32)
        for e in range(ne - 1):
            start = start + jnp.where(sub > e, padded[e:e + 1, :], 0.0)
        start_ref[...] = start
        run_ref[...] = jnp.zeros_like(run_ref)
        tile_start = lax.broadcasted_iota(jnp.int32, (ne, LANES), 1).astype(F32) * tile
        owner = jnp.sum(jnp.where(tile_start >= start + padded, 1, 0), axis=0, keepdims=True)
        owner = jnp.minimum(owner, ne - 1)
        esub = lax.broadcasted_iota(jnp.int32, (ne, LANES), 0)
        real = jnp.clip(cnt - (tile_start - start), 0.0, float(tile))
        te_ref[...] = owner
        tv_ref[...] = jnp.sum(jnp.where(esub == owner, real, 0.0), axis=0,
                              keepdims=True).astype(jnp.int32)

    @pl.when(phase == 1)
    def _():
        before = _dot(oh.astype(BF16), tri_ref[...])
        slot_e = start_ref[...] + run_ref[...] + before
        s1 = jnp.sum(jnp.where(oh1, slot_e, 0.0), axis=0, keepdims=True)
        s2 = jnp.sum(jnp.where(oh2, slot_e, 0.0), axis=0, keepdims=True)
        slot_ref[0:1, :] = s1.astype(jnp.int32)
        slot_ref[1:2, :] = s2.astype(jnp.int32)
        e21 = jnp.exp(m2 - m1)
        g1 = 1.0 / (1.0 + e21)
        g2 = e21 * g1
        rowi = lax.broadcasted_iota(jnp.int32, (LANES, tr), 0)
        gmat = jnp.where(rowi == 0, g1, jnp.where(rowi == 1, g2, 0.0))
        gate_ref[...] = gmat.T
        run_ref[...] += tile_cnt


def _router(x1, g_ffn, w_router, b_router, tile):
    t, d = x1.shape
    tr = TR
    nt = t // tr
    wp = jnp.pad(w_router, ((0, 0), (0, LANES - N_EXPERTS)))
    wh = wp.astype(BF16)
    wl = (wp - wh.astype(F32)).astype(BF16)
    const = lambda p, j: (0, 0)
    return pl.pallas_call(
        functools.partial(_router_kernel, tile=tile), grid=(2, nt),
        in_specs=[pl.BlockSpec((tr, d), lambda p, j: (j * (1 - p) + (nt - 1) * p, 0)),
                  pl.BlockSpec((1, d), const),
                  pl.BlockSpec(wh.shape, const), pl.BlockSpec(wl.shape, const),
                  pl.BlockSpec((N_EXPERTS, 1), const)],
        out_specs=[pl.BlockSpec((2, tr), lambda p, j: (0, j * p)),
                   pl.BlockSpec((tr, LANES), lambda p, j: (j * p, 0)),
                   pl.BlockSpec((1, LANES), const), pl.BlockSpec((1, LANES), const)],
        out_shape=[jax.ShapeDtypeStruct((2, t), jnp.int32),
                   jax.ShapeDtypeStruct((t, LANES), F32),
                   jax.ShapeDtypeStruct((1, LANES), jnp.int32),
                   jax.ShapeDtypeStruct((1, LANES), jnp.int32)],
        scratch_shapes=[pltpu.VMEM((nt, N_EXPERTS, tr), F32), pltpu.VMEM((tr, tr), BF16),
                        pltpu.VMEM((N_EXPERTS, 1), F32), pltpu.VMEM((N_EXPERTS, 1), F32),
                        pltpu.VMEM((N_EXPERTS, 1), F32)],
        compiler_params=_cparams(("arbitrary", "arbitrary")), name="moe_router",
    )(x1, g_ffn, wh, wl, b_router.reshape(N_EXPERTS, 1))


def _sc_mesh():
    return plsc.VectorSubcoreMesh(core_axis_name="c", subcore_axis_name="s")


def _sc_dispatch(xn, slots, n_slots):
    t, d = xn.shape
    win = SC_WINDOW
    nwin = t // win
    dp = d // SC_PLANES
    idx = slots.reshape(1, 2 * t)

    @functools.partial(pl.kernel,
                       out_type=jax.ShapeDtypeStruct((SC_PLANES, n_slots, dp), xn.dtype),
                       mesh=_sc_mesh(), scratch_types=[], name="moe_dispatch")
    def run(x_hbm, i_hbm, o_hbm):
        for c in range(SC_PLANES):
            def body(x_vmem, i0_vmem, i1_vmem, c=c):
                pltpu.sync_copy(x_vmem, o_hbm.at[c].at[i0_vmem.at[0]])
                pltpu.sync_copy(x_vmem, o_hbm.at[c].at[i1_vmem.at[0]])

            pltpu.emit_pipeline(
                body, grid=(nwin,),
                in_specs=[pl.BlockSpec((win, dp), lambda i, c=c: (i, c)),
                          pl.BlockSpec((1, win), lambda i: (0, i)),
                          pl.BlockSpec((1, win), lambda i: (0, i + nwin))],
                out_specs=[],
                core_axis_name=("c", "s"),
                dimension_semantics=(pltpu.PARALLEL,),
            )(x_hbm, i_hbm, i_hbm)

    return run(xn, idx)


def _sc_combine(y, slots):
    n2 = slots.size
    dp = y.shape[2]
    win = SC_WINDOW
    idx = slots.reshape(1, n2)

    @functools.partial(pl.kernel, out_type=jax.ShapeDtypeStruct((SC_PLANES, n2, dp), y.dtype),
                       mesh=_sc_mesh(), scratch_types=[], name="moe_combine")
    def run(y_hbm, i_hbm, o_hbm):
        for c in range(SC_PLANES):
            def body(i_vmem, o_vmem, c=c):
                pltpu.sync_copy(y_hbm.at[c].at[i_vmem.at[0]], o_vmem)

            pltpu.emit_pipeline(
                body, grid=(n2 // win,),
                in_specs=[pl.BlockSpec((1, win), lambda i: (0, i))],
                out_specs=[pl.BlockSpec((win, dp), lambda i: (i, 0))],
                core_axis_name=("c", "s"),
                dimension_semantics=(pltpu.PARALLEL,),
            )(i_hbm, o_hbm.at[c])

    return run(y, idx)


def _moe_ffn_kernel(te_ref, tv_ref, xs_ref, w1_hbm, w3_hbm, w2_hbm, y_ref,
                    xb_ref, acc_ref, w1b, w3b, w2b, wb1_ref, wb3_ref, wb2_ref, sem, *, layer, tf):
    i = pl.program_id(0)
    n_tiles = pl.num_programs(0)
    valid = tv_ref[i]
    tm, dp = xs_ref.shape[1:]
    ts = TSUB_MOE
    nc = w1_hbm.shape[3] // tf

    def copies(tile, c, slot):
        return _weight_chunk_copies(w1_hbm, w3_hbm, w2_hbm, (layer, te_ref[tile]), c, tf, slot,
                                    w1b, w3b, w2b, sem)

    @pl.when((i == 0) & (valid > 0))
    def _():
        for cp in copies(0, 0, 0):
            cp.start()

    rows = lax.broadcasted_iota(jnp.int32, (tm, 1), 0)
    for c, piece in enumerate(_unpack_planes(xs_ref)):
        xb_ref[:, c * dp:(c + 1) * dp] = jnp.where(rows < valid, piece, 0.0).astype(BF16)
    acc_ref[...] = jnp.zeros_like(acc_ref)

    def rows_block(lo, n, w1, w3, w2):
        acc_ref[lo:lo + n, :] += _swiglu_chunk(xb_ref[lo:lo + n, :], w1, w3, w2)

    @pl.when(valid > 0)
    def _():
        nxt_tile = jnp.minimum(i + 1, n_tiles - 1)
        next_tile_live = (i + 1 < n_tiles) & (tv_ref[nxt_tile] > 0)

        def chunk(c, carry):
            slot = lax.rem(i * nc + c, 2)
            last = c + 1 == nc

            @pl.when(jnp.logical_not(last) | next_tile_live)
            def _():
                for cp in copies(jnp.where(last, nxt_tile, i), jnp.where(last, 0, c + 1), 1 - slot):
                    cp.start()

            for cp in copies(i, c, slot):
                cp.wait()

            @pl.when(valid > tm - ts)
            def _():
                rows_block(0, tm, w1b[slot].astype(BF16), w3b[slot].astype(BF16),
                           w2b[slot].astype(BF16))

            @pl.when(valid <= tm - ts)
            def _():
                wb1_ref[...] = w1b[slot].astype(BF16)
                wb3_ref[...] = w3b[slot].astype(BF16)
                wb2_ref[...] = w2b[slot].astype(BF16)
                weights = lambda: (wb1_ref[...], wb3_ref[...], wb2_ref[...])

                @pl.when(valid <= ts)
                def _():
                    rows_block(0, ts, *weights())

                @pl.when(valid > ts)
                def _():
                    rows_block(0, 2 * ts, *weights())

                @pl.when(valid > 2 * ts)
                def _():
                    rows_block(2 * ts, ts, *weights())

            return carry

        lax.fori_loop(0, nc, chunk, 0)

    packed = _pack_pairs(acc_ref[...])
    for c in range(SC_PLANES):
        y_ref[c] = packed[:, c * dp:(c + 1) * dp]


def _moe_ffn(xs, tile_expert, tile_valid, w1, w3, w2, layer):
    planes, ns, dp = xs.shape
    d = 2 * planes * dp
    tm, tf = TM_MOE, TF_MOE
    assert tm == 4 * TSUB_MOE
    hbm = pl.BlockSpec(memory_space=pl.ANY)
    grid_spec = pltpu.PrefetchScalarGridSpec(
        num_scalar_prefetch=2, grid=(ns // tm,),
        in_specs=[pl.BlockSpec((planes, tm, dp), lambda i, te, tv: (0, i, 0)), hbm, hbm, hbm],
        out_specs=pl.BlockSpec((planes, tm, dp), lambda i, te, tv: (0, i, 0)),
        scratch_shapes=[pltpu.VMEM((tm, d), BF16), pltpu.VMEM((tm, d), F32),
                        pltpu.VMEM((2, d, tf), F32), pltpu.VMEM((2, d, tf), F32),
                        pltpu.VMEM((2, tf, d), F32),
                        pltpu.VMEM((d, tf), BF16), pltpu.VMEM((d, tf), BF16),
                        pltpu.VMEM((tf, d), BF16), pltpu.SemaphoreType.DMA((3, 2))])
    return pl.pallas_call(
        functools.partial(_moe_ffn_kernel, layer=layer, tf=tf), grid_spec=grid_spec,
        out_shape=jax.ShapeDtypeStruct((planes, ns, dp), jnp.int32),
        compiler_params=_cparams(("arbitrary",)), name="moe_ffn",
    )(tile_expert, tile_valid, xs, w1, w3, w2)


def _combine_kernel(x1_ref, gt_ref, y0_ref, y1_ref, o_ref):
    o_ref[...] = _gated_sum(x1_ref, gt_ref, y0_ref, y1_ref)


def _combine(x1, gates_tok, yg):
    t, d = x1.shape
    tm = 512
    nt = t // tm
    row = lambda i: (i, 0)
    return pl.pallas_call(
        _combine_kernel, grid=(nt,),
        in_specs=[pl.BlockSpec((tm, d), row), pl.BlockSpec((tm, LANES), row),
                  pl.BlockSpec((SC_PLANES, tm, d // (2 * SC_PLANES)), lambda i: (0, i, 0)),
                  pl.BlockSpec((SC_PLANES, tm, d // (2 * SC_PLANES)), lambda i: (0, i + nt, 0))],
        out_specs=pl.BlockSpec((tm, d), row),
        out_shape=jax.ShapeDtypeStruct((t, d), F32),
        compiler_params=_cparams(("parallel",)), name="moe_combine_residual",
    )(x1, gates_tok, yg, yg)


def _mla_weights(w_in, w_qb, w_kvb, g_qn, g_kn):
    d = w_in.shape[0]
    h = MLA_HEADS
    lat = Q_LORA + KV_LORA
    win = jnp.zeros((d, lat + HEAD_PAD), F32)
    win = win.at[:, :lat].set(w_in[:, :lat])
    win = win.at[:, lat + LANES:lat + LANES + QK_ROPE].set(w_in[:, lat:])
    wq = w_qb.reshape(Q_LORA, h, QK_HEAD)
    wq = jnp.pad(wq, ((0, 0), (0, 0), (0, HEAD_PAD - QK_HEAD))).reshape(Q_LORA, h * HEAD_PAD)
    wkv = w_kvb.reshape(KV_LORA, h, QK_NOPE + V_HEAD)
    wk = wkv[:, :, :QK_NOPE].reshape(KV_LORA, h * QK_NOPE)
    wvt = wkv[:, :, QK_NOPE:].reshape(KV_LORA, h * V_HEAD).T
    scale = math.log2(math.e) / math.sqrt(QK_HEAD)
    gq = jnp.broadcast_to((g_qn * scale).reshape(QK_HEAD, 1), (QK_HEAD, TQ))
    gk = jnp.pad(g_kn, (0, HEAD_PAD - QK_HEAD)).reshape(1, HEAD_PAD)
    return (win.astype(BF16), wq.T.astype(BF16), wk.astype(BF16), wvt.astype(BF16), gq, gk)


def kernel(x, positions, norm_mix, norm_ffn, mla_w_in, mla_g_qa, mla_w_qb, mla_g_kva, mla_w_kvb, mla_g_qn, mla_g_kn, mla_w_o, lru_w_in, lru_b_in, lru_conv_w, lru_conv_b, lru_w_gate_a, lru_b_gate_a, lru_w_gate_i, lru_b_gate_i, lru_lambda, lru_w_out, lru_b_out, ffn_w1, ffn_w3, ffn_w2, moe_w_router, moe_b_router, moe_w1, moe_w3, moe_w2):
    batch, seq, d = x.shape
    t = batch * seq
    depth = norm_mix.shape[0]
    n_slots = 2 * t + N_EXPERTS * TM_MOE
    n_tiles = n_slots // TM_MOE
    row = lambda v: v.reshape(1, -1)

    cs, cst = _rope_tables(positions)
    parts = (x.reshape(t, d),)
    for i in range(depth):
        j = i // 2
        if i % 2 == 0:
            w_in, w_qbt, w_kb, w_vt, g_qn, g_kn = _mla_weights(
                mla_w_in[j], mla_w_qb[j], mla_w_kvb[j], mla_g_qn[j], mla_g_kn[j])
            xr, q, k, vt = _mla_qkv(parts, cs, cst, row(norm_mix[i]), w_in, row(mla_g_qa[j]),
                                    w_qbt, row(mla_g_kva[j]), w_kb, w_vt, g_qn, g_kn)
            x1, xn = _attention(q, k, vt, xr, mla_w_o[j].astype(BF16), row(norm_ffn[i]),
                                batch, seq)
            parts = (_dense_ffn(xn, x1, ffn_w1, ffn_w3, ffn_w2, j),)
        else:
            (xr,) = parts
            x1, xn = _lru_mixer(
                xr.reshape(batch, seq, d), row(norm_mix[i]), lru_w_in[j].astype(BF16),
                row(lru_b_in[j]), lru_conv_w[j], row(lru_conv_b[j]),
                lru_w_gate_a[j].astype(BF16), lru_b_gate_a[j], lru_w_gate_i[j].astype(BF16),
                lru_b_gate_i[j], row(lru_lambda[j]), lru_w_out[j].astype(BF16),
                row(lru_b_out[j]), row(norm_ffn[i]))
            x1 = x1.reshape(t, d)
            xn = xn.reshape(t, d // 2)
            slots, gates_tok, te, tv = _router(x1, row(norm_ffn[i]), moe_w_router[j],
                                               moe_b_router[j], TM_MOE)
            xs = _sc_dispatch(xn, slots, n_slots)
            y = _moe_ffn(xs, te[0, :n_tiles], tv[0, :n_tiles], moe_w1, moe_w3, moe_w2, j)
            yg = _sc_combine(y, slots)
            parts = (x1, gates_tok, yg)
    if len(parts) == 3:
        out = _combine(*parts)
    else:
        out = parts[0]
    return out.reshape(batch, seq, d)
```

```python
import functools
import math

import numpy as np
import jax
import jax.numpy as jnp
from jax import lax
from jax.experimental import pallas as pl
from jax.experimental.pallas import tpu as pltpu
from jax.experimental.pallas import tpu_sc as plsc

F32 = jnp.float32
BF16 = jnp.bfloat16

NORM_EPS = 1e-6
CHUNK = 64
MLA_HEADS = 8
QK_NOPE = 128
QK_ROPE = 64
QK_HEAD = QK_NOPE + QK_ROPE
V_HEAD = 128
Q_LORA = 384
KV_LORA = 256
ROPE_BASE = 10000.0
LRU_HEADS = 8
CONV_WIDTH = 4
LRU_C = 8.0
N_EXPERTS = 8

LANES = 128
SUBLANES = 8
HEAD_PAD = 2 * LANES
VMEM_LIMIT = 56 * 1024 * 1024

TQ = 256
TM_QKV = 2 * TQ
ATTN_SUM_ROWS = 16
TM_FFN = 1024
TF_FFN = 512
T_LRU = 64
LRU_PITCH = T_LRU + SUBLANES
TR = 1024
TM_MOE = 1024
TSUB_MOE = 256
TF_MOE = 512
SC_WINDOW = 128
SC_PLANES = 2


def _cparams(sem):
    return pltpu.CompilerParams(dimension_semantics=sem, vmem_limit_bytes=VMEM_LIMIT)


def _rms(x, g):
    return x * lax.rsqrt(jnp.mean(x * x, axis=-1, keepdims=True) + NORM_EPS) * g


def _dot(a, b):
    return jnp.dot(a, b, preferred_element_type=F32)


def _dot_nt(a, b):
    return lax.dot_general(a, b, (((1,), (1,)), ((), ())), preferred_element_type=F32)


def _pack_pairs(x):
    n = x.shape[1] // 2
    bits = lax.bitcast_convert_type(x.astype(BF16).astype(F32), jnp.int32)
    return bits[:, :n] | lax.shift_right_logical(bits[:, n:], 16)


def _unpack_pairs(word):
    hi = lax.bitcast_convert_type(word & jnp.int32(-65536), F32)
    lo = lax.bitcast_convert_type(lax.shift_left(word, 16), F32)
    return hi, lo


def _unpack_planes(ref):
    halves = [_unpack_pairs(ref[c]) for c in range(SC_PLANES)]
    return [h for h, _ in halves] + [l for _, l in halves]


def _rope_kernel(pos_ref, inv_ref, cs_ref, cst_ref):
    ang = inv_ref[...] * pos_ref[...].astype(F32)
    half = QK_ROPE // 2
    ct = jnp.cos(ang)
    st = jnp.sin(ang)
    cst_ref[0:half, :] = ct[0:half, :]
    cst_ref[half:QK_ROPE, :] = st[0:half, :]
    pad = jnp.zeros((LANES - QK_ROPE, ang.shape[1]), F32)
    c = jnp.concatenate([ct, pad], axis=0).T
    s = jnp.concatenate([st, pad], axis=0).T
    lane = lax.broadcasted_iota(jnp.int32, c.shape, 1)
    cs_ref[:, 0:LANES] = jnp.where(lane < QK_ROPE, c, 0.0)
    cs_ref[:, LANES:2 * LANES] = jnp.where(lane < half, -s, 0.0)
    cs_ref[:, 2 * LANES:3 * LANES] = jnp.where((lane >= half) & (lane < QK_ROPE), s, 0.0)


def _rope_tables(positions):
    t = positions.size
    tm = 512
    inv = 1.0 / (ROPE_BASE ** (np.arange(0, QK_ROPE, 2, dtype=np.float32) / QK_ROPE))
    inv_col = np.concatenate([inv, inv]).reshape(QK_ROPE, 1).astype(np.float32)
    return pl.pallas_call(
        _rope_kernel,
        grid=(t // tm,),
        in_specs=[pl.BlockSpec((1, tm), lambda i: (0, i)),
                  pl.BlockSpec((QK_ROPE, 1), lambda i: (0, 0))],
        out_specs=[pl.BlockSpec((tm, 3 * LANES), lambda i: (i, 0)),
                   pl.BlockSpec((QK_ROPE, tm), lambda i: (0, i))],
        out_shape=[jax.ShapeDtypeStruct((t, 3 * LANES), F32),
                   jax.ShapeDtypeStruct((QK_ROPE, t), F32)],
        compiler_params=_cparams(("parallel",)),
        name="rope_tables",
    )(positions.reshape(1, t), jnp.asarray(inv_col))


def _rope_rot(x, cs):
    c = cs[:, 0:LANES]
    sa = cs[:, LANES:2 * LANES]
    sb = cs[:, 2 * LANES:3 * LANES]
    return (x * c + pltpu.roll(x, LANES - QK_ROPE // 2, 1) * sa
            + pltpu.roll(x, QK_ROPE // 2, 1) * sb)


def _mla_latent(x, gmix_ref, win_ref):
    return _dot(_rms(x, gmix_ref[...]).astype(BF16), win_ref[...])


def _mla_expand(hcat, gqa_ref, wqb_ref, gkva_ref, wkb_ref, wvt_ref):
    cq = hcat[:, :Q_LORA]
    ckv = hcat[:, Q_LORA:Q_LORA + KV_LORA]
    kr = hcat[:, Q_LORA + KV_LORA + LANES:Q_LORA + KV_LORA + 2 * LANES]
    qft = _dot_nt(wqb_ref[...], _rms(cq, gqa_ref[...]).astype(BF16))
    ckvn = _rms(ckv, gkva_ref[...]).astype(BF16)
    kf = _dot(ckvn, wkb_ref[...])
    vt = _dot_nt(wvt_ref[...], ckvn).astype(BF16)
    return qft, kf, kr, vt


def _mla_q_heads(qft, cst, gqn_ref, qt_ref, s):
    half = QK_ROPE // 2
    c, sn = cst[0:half, :], cst[half:QK_ROPE, :]
    g = gqn_ref[...]
    zero = jnp.zeros((HEAD_PAD - QK_HEAD, qft.shape[1]), BF16)
    for h in range(MLA_HEADS):
        blk = qft[h * HEAD_PAD:h * HEAD_PAD + QK_HEAD, :]
        ss = jnp.sum(blk * blk, axis=0, keepdims=True)
        inv = lax.rsqrt(ss * (1.0 / QK_HEAD) + NORM_EPS)
        y = blk * g * inv
        x1, x2 = y[QK_NOPE:QK_NOPE + half, :], y[QK_NOPE + half:QK_HEAD, :]
        base = h * HEAD_PAD
        qt_ref[s, base:base + QK_NOPE, :] = y[0:QK_NOPE, :].astype(BF16)
        qt_ref[s, base + QK_NOPE:base + QK_NOPE + half, :] = (x1 * c - x2 * sn).astype(BF16)
        qt_ref[s, base + QK_NOPE + half:base + QK_HEAD, :] = (x1 * sn + x2 * c).astype(BF16)
        qt_ref[s, base + QK_HEAD:base + HEAD_PAD, :] = zero


def _mla_k_heads(kf, kr, cs, gkn_ref, k_ref, rows):
    gkn_n, gkn_r = gkn_ref[:, 0:LANES], gkn_ref[:, LANES:2 * LANES]
    kr_ss = jnp.sum(kr * kr, axis=-1, keepdims=True)
    kr_rot = _rope_rot(kr * gkn_r, cs)
    for h in range(MLA_HEADS):
        kn = kf[:, h * LANES:(h + 1) * LANES]
        ssk = jnp.sum(kn * kn, axis=-1, keepdims=True) + kr_ss
        invk = lax.rsqrt(ssk * (1.0 / QK_HEAD) + NORM_EPS)
        k_ref[rows, h * HEAD_PAD:h * HEAD_PAD + LANES] = (kn * invk * gkn_n).astype(BF16)
        k_ref[rows, h * HEAD_PAD + LANES:(h + 1) * HEAD_PAD] = (kr_rot * invk).astype(BF16)


def _mla_qkv_body(x, cs_ref, cst_ref, gmix_ref, win_ref, gqa_ref, wqb_ref, gkva_ref, wkb_ref,
                  wvt_ref, gqn_ref, gkn_ref, qt_ref, k_ref, vt_ref):
    nsub = x.shape[0] // TQ
    rows = [slice(s * TQ, (s + 1) * TQ) for s in range(nsub)]
    hcats = [_mla_latent(x[r], gmix_ref, win_ref) for r in rows]
    mids = [_mla_expand(hc, gqa_ref, wqb_ref, gkva_ref, wkb_ref, wvt_ref) for hc in hcats]
    for s, (qft, kf, kr, vt) in enumerate(mids):
        vt_ref[s] = vt
    for s, (r, (qft, kf, kr, vt)) in enumerate(zip(rows, mids)):
        _mla_q_heads(qft, cst_ref[:, r], gqn_ref, qt_ref, s)
        _mla_k_heads(kf, kr, cs_ref[r, :], gkn_ref, k_ref, r)


def _gated_sum(x1_ref, gt_ref, y0_ref, y1_ref):
    gt = gt_ref[...]
    g0, g1 = gt[:, 0:1], gt[:, 1:2]
    y = jnp.concatenate([g0 * a + g1 * b
                         for a, b in zip(_unpack_planes(y0_ref), _unpack_planes(y1_ref))], axis=1)
    return x1_ref[...] + y


def _mla_qkv_kernel(x_ref, *rest):
    _mla_qkv_body(x_ref[...], *rest)


def _mla_qkv_combine_kernel(x1_ref, gt_ref, y0_ref, y1_ref, *rest):
    *mid, xo_ref, q_ref, k_ref, v_ref = rest
    x = _gated_sum(x1_ref, gt_ref, y0_ref, y1_ref)
    xo_ref[...] = x
    _mla_qkv_body(x, *mid, q_ref, k_ref, v_ref)


def _mla_qkv(x_parts, cs, cst, gmix, w_in, g_qa, w_qbt, g_kva, w_kb, w_vt, g_qn, g_kn):
    t, d = x_parts[0].shape
    tm = TM_QKV
    nsub = tm // TQ
    nt = t // tm
    row = lambda i: (i, 0)
    const = lambda i: (0, 0)
    h = MLA_HEADS
    weights = [gmix, w_in, g_qa, w_qbt, g_kva, w_kb, w_vt, g_qn, g_kn]
    w_specs = [pl.BlockSpec(w.shape, const) for w in weights]
    qkv_specs = [pl.BlockSpec((nsub, h * HEAD_PAD, TQ), lambda i: (i, 0, 0)),
                 pl.BlockSpec((tm, h * HEAD_PAD), row),
                 pl.BlockSpec((nsub, h * V_HEAD, TQ), lambda i: (i, 0, 0))]
    qkv_shapes = [jax.ShapeDtypeStruct((t // TQ, h * HEAD_PAD, TQ), BF16),
                  jax.ShapeDtypeStruct((t, h * HEAD_PAD), BF16),
                  jax.ShapeDtypeStruct((t // TQ, h * V_HEAD, TQ), BF16)]
    cs_specs = [pl.BlockSpec((tm, 3 * LANES), row), pl.BlockSpec((QK_ROPE, tm), lambda i: (0, i))]
    if len(x_parts) == 1:
        q, k, v = pl.pallas_call(
            _mla_qkv_kernel, grid=(nt,),
            in_specs=[pl.BlockSpec((tm, d), row)] + cs_specs + w_specs,
            out_specs=qkv_specs, out_shape=qkv_shapes,
            compiler_params=_cparams(("parallel",)), name="mla_qkv",
        )(x_parts[0], cs, cst, *weights)
        return x_parts[0], q, k, v
    x1, gates_tok, yg = x_parts
    x, q, k, v = pl.pallas_call(
        _mla_qkv_combine_kernel, grid=(nt,),
        in_specs=[pl.BlockSpec((tm, d), row), pl.BlockSpec((tm, LANES), row),
                  pl.BlockSpec((SC_PLANES, tm, d // (2 * SC_PLANES)), lambda i: (0, i, 0)),
                  pl.BlockSpec((SC_PLANES, tm, d // (2 * SC_PLANES)), lambda i: (0, i + nt, 0)),
                  ] + cs_specs + w_specs,
        out_specs=[pl.BlockSpec((tm, d), row)] + qkv_specs,
        out_shape=[jax.ShapeDtypeStruct((t, d), F32)] + qkv_shapes,
        compiler_params=_cparams(("parallel",)), name="mla_qkv_combine",
    )(x1, gates_tok, yg, yg, cs, cst, *weights)
    return x, q, k, v


def _attn_kernel(q_ref, k_ref, vt_ref, x_ref, wo_ref, g_ref, x1_ref, xn_ref,
                 s_ref, m_ref, acc_ref, o_ref):
    i = pl.program_id(1)
    ones = jnp.ones((ATTN_SUM_ROWS, TQ), BF16)

    def scores(h, j, n):
        off = pl.multiple_of(j * TQ, TQ)
        hs = slice(h * HEAD_PAD, (h + 1) * HEAD_PAD)
        s_ref[h, 0:n * TQ, :] = _dot(k_ref[pl.ds(off, n * TQ), hs], q_ref[0, hs, :])

    def weighted(h, j, p, n):
        out = None
        for t in range(n):
            lhs = jnp.concatenate([vt_ref[j + t, h * V_HEAD:(h + 1) * V_HEAD, :], ones], axis=0)
            part = _dot(lhs, p[t * TQ:(t + 1) * TQ, :].astype(BF16))
            out = part if out is None else out + part
        return out

    for h in range(MLA_HEADS):
        scores(h, i, 1)
    kc = lax.broadcasted_iota(jnp.int32, (TQ, TQ), 0) // CHUNK
    qc = lax.broadcasted_iota(jnp.int32, (TQ, TQ), 1) // CHUNK
    for h in range(MLA_HEADS):
        st = jnp.where(kc <= qc, s_ref[h, 0:TQ, :], -jnp.inf)
        m = jnp.max(st, axis=0, keepdims=True)
        acc_ref[h] = weighted(h, i, jnp.exp2(st - m), 1)
        m_ref[h] = m

    def full_tiles(j, n):
        for h in range(MLA_HEADS):
            scores(h, j, n)
        for h in range(MLA_HEADS):
            st = s_ref[h, 0:n * TQ, :]
            m_old = m_ref[h]
            m_new = jnp.maximum(m_old, jnp.max(st, axis=0, keepdims=True))
            acc_ref[h] = (jnp.exp2(m_old - m_new) * acc_ref[h]
                          + weighted(h, j, jnp.exp2(st - m_new), n))
            m_ref[h] = m_new

    def pair(jj, carry):
        full_tiles(2 * jj, 2)
        return carry

    lax.fori_loop(0, i // 2, pair, 0)

    @pl.when(i % 2 == 1)
    def _():
        full_tiles(i - 1, 1)

    for h in range(MLA_HEADS):
        acc = acc_ref[h]
        o_ref[:, h * V_HEAD:(h + 1) * V_HEAD] = (
            acc[0:V_HEAD, :] / acc[V_HEAD:V_HEAD + 1, :]).T.astype(BF16)

    x1 = x_ref[...] + _dot(o_ref[...], wo_ref[...])
    x1_ref[...] = x1
    xn_ref[...] = _rms(x1, g_ref[...]).astype(BF16)


def _attention(q, k, vt, x, w_o, g_ffn, batch, seq):
    h = MLA_HEADS
    nq = seq // TQ
    t, d = x.shape
    tile = lambda b, i: (b * nq + i, 0)
    const = lambda b, i: (0, 0)
    return pl.pallas_call(
        _attn_kernel, grid=(batch, nq),
        in_specs=[pl.BlockSpec((1, h * HEAD_PAD, TQ), lambda b, i: (b * nq + i, 0, 0)),
                  pl.BlockSpec((seq, h * HEAD_PAD), lambda b, i: (b, 0)),
                  pl.BlockSpec((nq, h * V_HEAD, TQ), lambda b, i: (b, 0, 0)),
                  pl.BlockSpec((TQ, d), tile), pl.BlockSpec(w_o.shape, const),
                  pl.BlockSpec((1, d), const)],
        out_specs=[pl.BlockSpec((TQ, d), tile), pl.BlockSpec((TQ, d), tile)],
        out_shape=[jax.ShapeDtypeStruct((t, d), F32), jax.ShapeDtypeStruct((t, d), BF16)],
        scratch_shapes=[pltpu.VMEM((h, 2 * TQ, TQ), F32),
                        pltpu.VMEM((h, 1, TQ), F32),
                        pltpu.VMEM((h, V_HEAD + ATTN_SUM_ROWS, TQ), F32),
                        pltpu.VMEM((TQ, h * V_HEAD), BF16)],
        compiler_params=_cparams(("parallel", "arbitrary")), name="attention",
    )(q, k, vt, x, w_o, g_ffn)


def _swiglu_chunk(x, w1, w3, w2):
    a = _dot(x, w1)
    b = _dot(x, w3)
    hid = (a * jax.nn.sigmoid(a)) * b
    return _dot(hid.astype(BF16), w2)


def _weight_chunk_copies(w1_hbm, w3_hbm, w2_hbm, lead, c, tf, slot, w1b, w3b, w2b, sem):
    cols = pl.ds(pl.multiple_of(c * tf, tf), tf)
    return (pltpu.make_async_copy(w1_hbm.at[lead + (slice(None), cols)], w1b.at[slot], sem.at[0, slot]),
            pltpu.make_async_copy(w3_hbm.at[lead + (slice(None), cols)], w3b.at[slot], sem.at[1, slot]),
            pltpu.make_async_copy(w2_hbm.at[lead + (cols, slice(None))], w2b.at[slot], sem.at[2, slot]))


def _ffn_kernel(xn_ref, x1_ref, w1_hbm, w3_hbm, w2_hbm, o_ref, w1b, w3b, w2b, sem, *, layer, tf):
    i = pl.program_id(0)
    n_tiles = pl.num_programs(0)
    ff = w1_hbm.shape[2]
    chunks = [(lo, min(tf, ff - lo)) for lo in range(0, ff, tf)]
    assert len(chunks) % 2 == 0

    def copies(c):
        lo, width = chunks[c]
        slot = c % 2
        cols = pl.ds(lo, width)
        return (pltpu.make_async_copy(w1_hbm.at[layer, :, cols], w1b.at[slot, :, 0:width],
                                      sem.at[0, slot]),
                pltpu.make_async_copy(w3_hbm.at[layer, :, cols], w3b.at[slot, :, 0:width],
                                      sem.at[1, slot]),
                pltpu.make_async_copy(w2_hbm.at[layer, cols, :], w2b.at[slot, 0:width, :],
                                      sem.at[2, slot]))

    @pl.when(i == 0)
    def _():
        for cp in copies(0):
            cp.start()

    o_ref[...] = x1_ref[...]
    for c, (lo, width) in enumerate(chunks):
        if c + 1 < len(chunks):
            for cp in copies(c + 1):
                cp.start()
        else:
            @pl.when(i + 1 < n_tiles)
            def _():
                for cp in copies(0):
                    cp.start()
        for cp in copies(c):
            cp.wait()
        slot = c % 2
        o_ref[...] += _swiglu_chunk(xn_ref[...], w1b[slot, :, 0:width].astype(BF16),
                                    w3b[slot, :, 0:width].astype(BF16),
                                    w2b[slot, 0:width, :].astype(BF16))


def _dense_ffn(xn, x1, w1, w3, w2, layer):
    t, d = x1.shape
    tm, tf = TM_FFN, TF_FFN
    row = lambda i: (i, 0)
    hbm = pl.BlockSpec(memory_space=pl.ANY)
    return pl.pallas_call(
        functools.partial(_ffn_kernel, layer=layer, tf=tf), grid=(t // tm,),
        in_specs=[pl.BlockSpec((tm, d), row), pl.BlockSpec((tm, d), row), hbm, hbm, hbm],
        out_specs=pl.BlockSpec((tm, d), row),
        out_shape=jax.ShapeDtypeStruct((t, d), F32),
        scratch_shapes=[pltpu.VMEM((2, d, tf), F32), pltpu.VMEM((2, d, tf), F32),
                        pltpu.VMEM((2, tf, d), F32), pltpu.SemaphoreType.DMA((3, 2))],
        compiler_params=_cparams(("arbitrary",)), name="dense_ffn",
    )(xn, x1, w1, w3, w2)


def _lru_kernel(x_ref, gmix_ref, win_ref, bin_ref, cw_ref, cb_ref, wga_ref, bga_ref,
                wgi_ref, bgi_ref, lam_ref, wout_ref, bout_ref, gffn_ref,
                x1_ref, xn_ref,
                xpad_ref, a_ref, u_ref, gate_ref, yg_ref, h_ref):
    nb, tt, d = x_ref.shape
    w = lam_ref.shape[1]
    m = nb * tt

    @pl.when(pl.program_id(0) == 0)
    def _():
        h_ref[...] = jnp.zeros_like(h_ref)
        xpad_ref[:, 0:SUBLANES, :] = jnp.zeros((nb, SUBLANES, w), F32)

    x = x_ref[...].reshape(m, d)
    xn = _rms(x, gmix_ref[...]).astype(BF16)
    hcat = _dot(xn, win_ref[...]) + bin_ref[...]
    gate_ref[...] = jax.nn.gelu(hcat[:, :w], approximate=True)
    xpad_ref[:, SUBLANES:, :] = hcat[:, w:].reshape(nb, tt, w)

    xc = cb_ref[...].reshape(1, 1, w) + jnp.zeros((nb, tt, w), F32)
    for j in range(CONV_WIDTH):
        lo = SUBLANES - (CONV_WIDTH - 1) + j
        xc = xc + xpad_ref[:, lo:lo + tt, :] * cw_ref[j:j + 1, :].reshape(1, 1, w)
    xpad_ref[:, 0:SUBLANES, :] = xpad_ref[:, tt:tt + SUBLANES, :]
    xc = xc.reshape(m, w)

    lam = lam_ref[...]
    log_sig = jnp.minimum(lam, 0.0) - jnp.log1p(jnp.exp(-jnp.abs(lam)))
    for hh in range(LRU_HEADS):
        sl = slice(hh * LANES, (hh + 1) * LANES)
        xh = xc[:, sl]
        xhb = xh.astype(BF16)
        r = jax.nn.sigmoid(_dot(xhb, wga_ref[hh]) + bga_ref[hh:hh + 1, :])
        ig = jax.nn.sigmoid(_dot(xhb, wgi_ref[hh]) + bgi_ref[hh:hh + 1, :])
        log_a = LRU_C * r * log_sig[:, sl]
        a = jnp.exp(log_a)
        z = 1.0 - a * a
        u = (z * lax.rsqrt(jnp.maximum(z, 1e-30))) * (ig * xh)
        for b in range(nb):
            a_ref[hh, b * LRU_PITCH:b * LRU_PITCH + tt, :] = a[b * tt:(b + 1) * tt, :]
            u_ref[hh, b * LRU_PITCH:b * LRU_PITCH + tt, :] = u[b * tt:(b + 1) * tt, :]

    def scan_step(t, hs):
        new = []
        for hh in range(LRU_HEADS):
            rows = pl.ds(t, nb, stride=LRU_PITCH)
            hv = a_ref[hh, rows, :] * hs[hh] + u_ref[hh, rows, :]
            u_ref[hh, rows, :] = hv
            new.append(hv)
        return tuple(new)

    hs = lax.fori_loop(0, tt, scan_step, tuple(h_ref[hh] for hh in range(LRU_HEADS)), unroll=4)
    for hh in range(LRU_HEADS):
        h_ref[hh] = hs[hh]

    for hh in range(LRU_HEADS):
        sl = slice(hh * LANES, (hh + 1) * LANES)
        for b in range(nb):
            y = u_ref[hh, b * LRU_PITCH:b * LRU_PITCH + tt, :]
            yg_ref[b * tt:(b + 1) * tt, sl] = (y * gate_ref[b * tt:(b + 1) * tt, sl]).astype(BF16)

    x1 = x + _dot(yg_ref[...], wout_ref[...]) + bout_ref[...]
    x1_ref[...] = x1.reshape(nb, tt, d)
    xn_ref[...] = _pack_pairs(_rms(x1, gffn_ref[...])).reshape(nb, tt, d // 2)


def _lru_mixer(x3, gmix, w_in, b_in, conv_w, conv_b, wga, bga, wgi, bgi, lam, w_out, b_out, g_ffn):
    nb, seq, d = x3.shape
    w = lam.shape[1]
    tt = T_LRU
    m = nb * tt
    consts = [gmix, w_in, b_in, conv_w, conv_b, wga, bga, wgi, bgi, lam, w_out, b_out, g_ffn]
    const_specs = [pl.BlockSpec(c.shape, (lambda i, n=c.ndim: (0,) * n)) for c in consts]
    blk = pl.BlockSpec((nb, tt, d), lambda i: (0, i, 0))
    return pl.pallas_call(
        _lru_kernel, grid=(seq // tt,),
        in_specs=[blk] + const_specs,
        out_specs=[blk, pl.BlockSpec((nb, tt, d // 2), lambda i: (0, i, 0))],
        out_shape=[jax.ShapeDtypeStruct((nb, seq, d), F32),
                   jax.ShapeDtypeStruct((nb, seq, d // 2), jnp.int32)],
        scratch_shapes=[pltpu.VMEM((nb, tt + SUBLANES, w), F32),
                        pltpu.VMEM((LRU_HEADS, nb * LRU_PITCH, LANES), F32),
                        pltpu.VMEM((LRU_HEADS, nb * LRU_PITCH, LANES), F32),
                        pltpu.VMEM((m, w), F32),
                        pltpu.VMEM((m, w), BF16),
                        pltpu.VMEM((LRU_HEADS, nb, LANES), F32)],
        compiler_params=_cparams(("arbitrary",)), name="rglru_mixer",
    )(x3, *consts)


def _router_kernel(x1_ref, g_ref, wh_ref, wl_ref, br_ref, slot_ref, gate_ref, te_ref, tv_ref,
                   lg_ref, tri_ref, cnt_ref, run_ref, start_ref, *, tile):
    phase = pl.program_id(0)
    j = pl.program_id(1)
    ne = N_EXPERTS
    tr = x1_ref.shape[0]

    @pl.when((phase == 0) & (j == 0))
    def _():
        cnt_ref[...] = jnp.zeros_like(cnt_ref)
        tri_ref[...] = jnp.where(lax.broadcasted_iota(jnp.int32, (tr, tr), 0)
                                 < lax.broadcasted_iota(jnp.int32, (tr, tr), 1),
                                 1.0, 0.0).astype(BF16)

    @pl.when(phase == 0)
    def _():
        x = _rms(x1_ref[...], g_ref[...])
        xh = x.astype(BF16)
        xl = (x - xh.astype(F32)).astype(BF16)
        wh = wh_ref[...]
        nat = _dot(xh, wh) + _dot(xl, wh) + _dot(xh, wl_ref[...])
        lg_ref[j] = nat.T[0:ne, :] + br_ref[...]

    logits = lg_ref[j]
    eidx = lax.broadcasted_iota(jnp.int32, (ne, tr), 0)
    m1 = jnp.max(logits, axis=0, keepdims=True)
    i1 = jnp.min(jnp.where(logits == m1, eidx, ne), axis=0, keepdims=True)
    oh1 = eidx == i1
    rest = jnp.where(oh1, -jnp.inf, logits)
    m2 = jnp.max(rest, axis=0, keepdims=True)
    i2 = jnp.min(jnp.where(rest == m2, eidx, ne), axis=0, keepdims=True)
    oh2 = eidx == i2
    oh = jnp.where(oh1 | oh2, 1.0, 0.0)
    tile_cnt = jnp.sum(oh, axis=1, keepdims=True)

    @pl.when(phase == 0)
    def _():
        cnt_ref[...] += tile_cnt

    @pl.when((phase == 1) & (j == 0))
    def _():
        cnt = cnt_ref[...]
        padded = jnp.ceil(cnt * (1.0 / tile)) * tile
        sub = lax.broadcasted_iota(jnp.int32, (ne, 1), 0)
        start = jnp.zeros((ne, 1), F32)
        for e in range(ne - 1):
            start = start + jnp.where(sub > e, padded[e:e + 1, :], 0.0)
        start_ref[...] = start
        run_ref[...] = jnp.zeros_like(run_ref)
        tile_start = lax.broadcasted_iota(jnp.int32, (ne, LANES), 1).astype(F32) * tile
        owner = jnp.sum(jnp.where(tile_start >= start + padded, 1, 0), axis=0, keepdims=True)
        owner = jnp.minimum(owner, ne - 1)
        esub = lax.broadcasted_iota(jnp.int32, (ne, LANES), 0)
        real = jnp.clip(cnt - (tile_start - start), 0.0, float(tile))
        te_ref[...] = owner
        tv_ref[...] = jnp.sum(jnp.where(esub == owner, real, 0.0), axis=0,
                              keepdims=True).astype(jnp.int32)

    @pl.when(phase == 1)
    def _():
        before = _dot(oh.astype(BF16), tri_ref[...])
        slot_e = start_ref[...] + run_ref[...] + before
        s1 = jnp.sum(jnp.where(oh1, slot_e, 0.0), axis=0, keepdims=True)
        s2 = jnp.sum(jnp.where(oh2, slot_e, 0.0), axis=0, keepdims=True)
        slot_ref[0:1, :] = s1.astype(jnp.int32)
        slot_ref[1:2, :] = s2.astype(jnp.int32)
        e21 = jnp.exp(m2 - m1)
        g1 = 1.0 / (1.0 + e21)
        g2 = e21 * g1
        rowi = lax.broadcasted_iota(jnp.int32, (LANES, tr), 0)
        gmat = jnp.where(rowi == 0, g1, jnp.where(rowi == 1, g2, 0.0))
        gate_ref[...] = gmat.T
        run_ref[...] += tile_cnt


def _router(x1, g_ffn, w_router, b_router, tile):
    t, d = x1.shape
    tr = TR
    nt = t // tr
    wp = jnp.pad(w_router, ((0, 0), (0, LANES - N_EXPERTS)))
    wh = wp.astype(BF16)
    wl = (wp - wh.astype(F32)).astype(BF16)
    const = lambda p, j: (0, 0)
    return pl.pallas_call(
        functools.partial(_router_kernel, tile=tile), grid=(2, nt),
        in_specs=[pl.BlockSpec((tr, d), lambda p, j: (j * (1 - p) + (nt - 1) * p, 0)),
                  pl.BlockSpec((1, d), const),
                  pl.BlockSpec(wh.shape, const), pl.BlockSpec(wl.shape, const),
                  pl.BlockSpec((N_EXPERTS, 1), const)],
        out_specs=[pl.BlockSpec((2, tr), lambda p, j: (0, j * p)),
                   pl.BlockSpec((tr, LANES), lambda p, j: (j * p, 0)),
                   pl.BlockSpec((1, LANES), const), pl.BlockSpec((1, LANES), const)],
        out_shape=[jax.ShapeDtypeStruct((2, t), jnp.int32),
                   jax.ShapeDtypeStruct((t, LANES), F32),
                   jax.ShapeDtypeStruct((1, LANES), jnp.int32),
                   jax.ShapeDtypeStruct((1, LANES), jnp.int32)],
        scratch_shapes=[pltpu.VMEM((nt, N_EXPERTS, tr), F32), pltpu.VMEM((tr, tr), BF16),
                        pltpu.VMEM((N_EXPERTS, 1), F32), pltpu.VMEM((N_EXPERTS, 1), F32),
                        pltpu.VMEM((N_EXPERTS, 1), F32)],
        compiler_params=_cparams(("arbitrary", "arbitrary")), name="moe_router",
    )(x1, g_ffn, wh, wl, b_router.reshape(N_EXPERTS, 1))


def _sc_mesh():
    return plsc.VectorSubcoreMesh(core_axis_name="c", subcore_axis_name="s")


def _sc_dispatch(xn, slots, n_slots):
    t, d = xn.shape
    win = SC_WINDOW
    nwin = t // win
    dp = d // SC_PLANES
    idx = slots.reshape(1, 2 * t)

    @functools.partial(pl.kernel,
                       out_type=jax.ShapeDtypeStruct((SC_PLANES, n_slots, dp), xn.dtype),
                       mesh=_sc_mesh(), scratch_types=[], name="moe_dispatch")
    def run(x_hbm, i_hbm, o_hbm):
        for c in range(SC_PLANES):
            def body(x_vmem, i0_vmem, i1_vmem, c=c):
                pltpu.sync_copy(x_vmem, o_hbm.at[c].at[i0_vmem.at[0]])
                pltpu.sync_copy(x_vmem, o_hbm.at[c].at[i1_vmem.at[0]])

            pltpu.emit_pipeline(
                body, grid=(nwin,),
                in_specs=[pl.BlockSpec((win, dp), lambda i, c=c: (i, c)),
                          pl.BlockSpec((1, win), lambda i: (0, i)),
                          pl.BlockSpec((1, win), lambda i: (0, i + nwin))],
                out_specs=[],
                core_axis_name=("c", "s"),
                dimension_semantics=(pltpu.PARALLEL,),
            )(x_hbm, i_hbm, i_hbm)

    return run(xn, idx)


def _sc_combine(y, slots):
    n2 = slots.size
    dp = y.shape[2]
    win = SC_WINDOW
    idx = slots.reshape(1, n2)

    @functools.partial(pl.kernel, out_type=jax.ShapeDtypeStruct((SC_PLANES, n2, dp), y.dtype),
                       mesh=_sc_mesh(), scratch_types=[], name="moe_combine")
    def run(y_hbm, i_hbm, o_hbm):
        for c in range(SC_PLANES):
            def body(i_vmem, o_vmem, c=c):
                pltpu.sync_copy(y_hbm.at[c].at[i_vmem.at[0]], o_vmem)

            pltpu.emit_pipeline(
                body, grid=(n2 // win,),
                in_specs=[pl.BlockSpec((1, win), lambda i: (0, i))],
                out_specs=[pl.BlockSpec((win, dp), lambda i: (i, 0))],
                core_axis_name=("c", "s"),
                dimension_semantics=(pltpu.PARALLEL,),
            )(i_hbm, o_hbm.at[c])

    return run(y, idx)


def _moe_ffn_kernel(te_ref, tv_ref, xs_ref, w1_hbm, w3_hbm, w2_hbm, y_ref,
                    xb_ref, acc_ref, w1b, w3b, w2b, wb1_ref, wb3_ref, wb2_ref, sem, *, layer, tf):
    i = pl.program_id(0)
    n_tiles = pl.num_programs(0)
    valid = tv_ref[i]
    tm, dp = xs_ref.shape[1:]
    ts = TSUB_MOE
    nc = w1_hbm.shape[3] // tf

    def copies(tile, c, slot):
        return _weight_chunk_copies(w1_hbm, w3_hbm, w2_hbm, (layer, te_ref[tile]), c, tf, slot,
                                    w1b, w3b, w2b, sem)

    @pl.when((i == 0) & (valid > 0))
    def _():
        for cp in copies(0, 0, 0):
            cp.start()

    rows = lax.broadcasted_iota(jnp.int32, (tm, 1), 0)
    for c, piece in enumerate(_unpack_planes(xs_ref)):
        xb_ref[:, c * dp:(c + 1) * dp] = jnp.where(rows < valid, piece, 0.0).astype(BF16)
    acc_ref[...] = jnp.zeros_like(acc_ref)

    def rows_block(lo, n, w1, w3, w2):
        acc_ref[lo:lo + n, :] += _swiglu_chunk(xb_ref[lo:lo + n, :], w1, w3, w2)

    @pl.when(valid > 0)
    def _():
        nxt_tile = jnp.minimum(i + 1, n_tiles - 1)
        next_tile_live = (i + 1 < n_tiles) & (tv_ref[nxt_tile] > 0)

        def chunk(c, carry):
            slot = lax.rem(i * nc + c, 2)
            last = c + 1 == nc

            @pl.when(jnp.logical_not(last) | next_tile_live)
            def _():
                for cp in copies(jnp.where(last, nxt_tile, i), jnp.where(last, 0, c + 1), 1 - slot):
                    cp.start()

            for cp in copies(i, c, slot):
                cp.wait()

            @pl.when(valid > tm - ts)
            def _():
                rows_block(0, tm, w1b[slot].astype(BF16), w3b[slot].astype(BF16),
                           w2b[slot].astype(BF16))

            @pl.when(valid <= tm - ts)
            def _():
                wb1_ref[...] = w1b[slot].astype(BF16)
                wb3_ref[...] = w3b[slot].astype(BF16)
                wb2_ref[...] = w2b[slot].astype(BF16)
                weights = lambda: (wb1_ref[...], wb3_ref[...], wb2_ref[...])

                @pl.when(valid <= ts)
                def _():
                    rows_block(0, ts, *weights())

                @pl.when(valid > ts)
                def _():
                    rows_block(0, 2 * ts, *weights())

                @pl.when(valid > 2 * ts)
                def _():
                    rows_block(2 * ts, ts, *weights())

            return carry

        lax.fori_loop(0, nc, chunk, 0)

    packed = _pack_pairs(acc_ref[...])
    for c in range(SC_PLANES):
        y_ref[c] = packed[:, c * dp:(c + 1) * dp]


def _moe_ffn(xs, tile_expert, tile_valid, w1, w3, w2, layer):
    planes, ns, dp = xs.shape
    d = 2 * planes * dp
    tm, tf = TM_MOE, TF_MOE
    assert tm == 4 * TSUB_MOE
    hbm = pl.BlockSpec(memory_space=pl.ANY)
    grid_spec = pltpu.PrefetchScalarGridSpec(
        num_scalar_prefetch=2, grid=(ns // tm,),
        in_specs=[pl.BlockSpec((planes, tm, dp), lambda i, te, tv: (0, i, 0)), hbm, hbm, hbm],
        out_specs=pl.BlockSpec((planes, tm, dp), lambda i, te, tv: (0, i, 0)),
        scratch_shapes=[pltpu.VMEM((tm, d), BF16), pltpu.VMEM((tm, d), F32),
                        pltpu.VMEM((2, d, tf), F32), pltpu.VMEM((2, d, tf), F32),
                        pltpu.VMEM((2, tf, d), F32),
                        pltpu.VMEM((d, tf), BF16), pltpu.VMEM((d, tf), BF16),
                        pltpu.VMEM((tf, d), BF16), pltpu.SemaphoreType.DMA((3, 2))])
    return pl.pallas_call(
        functools.partial(_moe_ffn_kernel, layer=layer, tf=tf), grid_spec=grid_spec,
        out_shape=jax.ShapeDtypeStruct((planes, ns, dp), jnp.int32),
        compiler_params=_cparams(("arbitrary",)), name="moe_ffn",
    )(tile_expert, tile_valid, xs, w1, w3, w2)


def _combine_kernel(x1_ref, gt_ref, y0_ref, y1_ref, o_ref):
    o_ref[...] = _gated_sum(x1_ref, gt_ref, y0_ref, y1_ref)


def _combine(x1, gates_tok, yg):
    t, d = x1.shape
    tm = 512
    nt = t // tm
    row = lambda i: (i, 0)
    return pl.pallas_call(
        _combine_kernel, grid=(nt,),
        in_specs=[pl.BlockSpec((tm, d), row), pl.BlockSpec((tm, LANES), row),
                  pl.BlockSpec((SC_PLANES, tm, d // (2 * SC_PLANES)), lambda i: (0, i, 0)),
                  pl.BlockSpec((SC_PLANES, tm, d // (2 * SC_PLANES)), lambda i: (0, i + nt, 0))],
        out_specs=pl.BlockSpec((tm, d), row),
        out_shape=jax.ShapeDtypeStruct((t, d), F32),
        compiler_params=_cparams(("parallel",)), name="moe_combine_residual",
    )(x1, gates_tok, yg, yg)


def _mla_weights(w_in, w_qb, w_kvb, g_qn, g_kn):
    d = w_in.shape[0]
    h = MLA_HEADS
    lat = Q_LORA + KV_LORA
    win = jnp.zeros((d, lat + HEAD_PAD), F32)
    win = win.at[:, :lat].set(w_in[:, :lat])
    win = win.at[:, lat + LANES:lat + LANES + QK_ROPE].set(w_in[:, lat:])
    wq = w_qb.reshape(Q_LORA, h, QK_HEAD)
    wq = jnp.pad(wq, ((0, 0), (0, 0), (0, HEAD_PAD - QK_HEAD))).reshape(Q_LORA, h * HEAD_PAD)
    wkv = w_kvb.reshape(KV_LORA, h, QK_NOPE + V_HEAD)
    wk = wkv[:, :, :QK_NOPE].reshape(KV_LORA, h * QK_NOPE)
    wvt = wkv[:, :, QK_NOPE:].reshape(KV_LORA, h * V_HEAD).T
    scale = math.log2(math.e) / math.sqrt(QK_HEAD)
    gq = jnp.broadcast_to((g_qn * scale).reshape(QK_HEAD, 1), (QK_HEAD, TQ))
    gk = jnp.pad(g_kn, (0, HEAD_PAD - QK_HEAD)).reshape(1, HEAD_PAD)
    return (win.astype(BF16), wq.T.astype(BF16), wk.astype(BF16), wvt.astype(BF16), gq, gk)


def kernel(x, positions, norm_mix, norm_ffn, mla_w_in, mla_g_qa, mla_w_qb, mla_g_kva, mla_w_kvb, mla_g_qn, mla_g_kn, mla_w_o, lru_w_in, lru_b_in, lru_conv_w, lru_conv_b, lru_w_gate_a, lru_b_gate_a, lru_w_gate_i, lru_b_gate_i, lru_lambda, lru_w_out, lru_b_out, ffn_w1, ffn_w3, ffn_w2, moe_w_router, moe_b_router, moe_w1, moe_w3, moe_w2):
    batch, seq, d = x.shape
    t = batch * seq
    depth = norm_mix.shape[0]
    n_slots = 2 * t + N_EXPERTS * TM_MOE
    n_tiles = n_slots // TM_MOE
    row = lambda v: v.reshape(1, -1)

    cs, cst = _rope_tables(positions)
    parts = (x.reshape(t, d),)
    for i in range(depth):
        j = i // 2
        if i % 2 == 0:
            w_in, w_qbt, w_kb, w_vt, g_qn, g_kn = _mla_weights(
                mla_w_in[j], mla_w_qb[j], mla_w_kvb[j], mla_g_qn[j], mla_g_kn[j])
            xr, q, k, vt = _mla_qkv(parts, cs, cst, row(norm_mix[i]), w_in, row(mla_g_qa[j]),
                                    w_qbt, row(mla_g_kva[j]), w_kb, w_vt, g_qn, g_kn)
            x1, xn = _attention(q, k, vt, xr, mla_w_o[j].astype(BF16), row(norm_ffn[i]),
                                batch, seq)
            parts = (_dense_ffn(xn, x1, ffn_w1, ffn_w3, ffn_w2, j),)
        else:
            (xr,) = parts
            x1, xn = _lru_mixer(
                xr.reshape(batch, seq, d), row(norm_mix[i]), lru_w_in[j].astype(BF16),
                row(lru_b_in[j]), lru_conv_w[j], row(lru_conv_b[j]),
                lru_w_gate_a[j].astype(BF16), lru_b_gate_a[j], lru_w_gate_i[j].astype(BF16),
                lru_b_gate_i[j], row(lru_lambda[j]), lru_w_out[j].astype(BF16),
                row(lru_b_out[j]), row(norm_ffn[i]))
            x1 = x1.reshape(t, d)
            xn = xn.reshape(t, d // 2)
            slots, gates_tok, te, tv = _router(x1, row(norm_ffn[i]), moe_w_router[j],
                                               moe_b_router[j], TM_MOE)
            xs = _sc_dispatch(xn, slots, n_slots)
            y = _moe_ffn(xs, te[0, :n_tiles], tv[0, :n_tiles], moe_w1, moe_w3, moe_w2, j)
            yg = _sc_combine(y, slots)
            parts = (x1, gates_tok, yg)
    if len(parts) == 3:
        out = _combine(*parts)
    else:
        out = parts[0]
    return out.reshape(batch, seq, d)
```

```python
import functools
import math

import numpy as np
import jax
import jax.numpy as jnp
from jax import lax
from jax.experimental import pallas as pl
from jax.experimental.pallas import tpu as pltpu
from jax.experimental.pallas import tpu_sc as plsc

F32 = jnp.float32
BF16 = jnp.bfloat16

NORM_EPS = 1e-6
CHUNK = 64
MLA_HEADS = 8
QK_NOPE = 128
QK_ROPE = 64
QK_HEAD = QK_NOPE + QK_ROPE
V_HEAD = 128
Q_LORA = 384
KV_LORA = 256
ROPE_BASE = 10000.0
LRU_HEADS = 8
CONV_WIDTH = 4
LRU_C = 8.0
N_EXPERTS = 8

LANES = 128
SUBLANES = 8
HEAD_PAD = 2 * LANES
VMEM_LIMIT = 56 * 1024 * 1024

TQ = 256
TM_QKV = 4 * TQ
TM_QKV_COMBINE = 2 * TQ
ATTN_SUM_ROWS = 16
TM_FFN = 1024
TF_FFN = 512
T_LRU = 64
LRU_PITCH = T_LRU + SUBLANES
TR = 1024
TM_MOE = 1024
TSUB_MOE = 256
TF_MOE = 512
SC_WINDOW = 128
SC_PLANES = 2


def _cparams(sem):
    return pltpu.CompilerParams(dimension_semantics=sem, vmem_limit_bytes=VMEM_LIMIT)


def _rms(x, g):
    return x * lax.rsqrt(jnp.mean(x * x, axis=-1, keepdims=True) + NORM_EPS) * g


def _dot(a, b):
    return jnp.dot(a, b, preferred_element_type=F32)


def _dot_nt(a, b):
    return lax.dot_general(a, b, (((1,), (1,)), ((), ())), preferred_element_type=F32)


def _pack_pairs(x):
    n = x.shape[1] // 2
    bits = lax.bitcast_convert_type(x.astype(BF16).astype(F32), jnp.int32)
    return bits[:, :n] | lax.shift_right_logical(bits[:, n:], 16)


def _unpack_pairs(word):
    hi = lax.bitcast_convert_type(word & jnp.int32(-65536), F32)
    lo = lax.bitcast_convert_type(lax.shift_left(word, 16), F32)
    return hi, lo


def _unpack_planes(ref):
    halves = [_unpack_pairs(ref[c]) for c in range(SC_PLANES)]
    return [h for h, _ in halves] + [l for _, l in halves]


def _rope_kernel(pos_ref, inv_ref, cs_ref, cst_ref):
    ang = inv_ref[...] * pos_ref[...].astype(F32)
    half = QK_ROPE // 2
    ct = jnp.cos(ang)
    st = jnp.sin(ang)
    cst_ref[0:half, :] = ct[0:half, :]
    cst_ref[half:QK_ROPE, :] = st[0:half, :]
    pad = jnp.zeros((LANES - QK_ROPE, ang.shape[1]), F32)
    c = jnp.concatenate([ct, pad], axis=0).T
    s = jnp.concatenate([st, pad], axis=0).T
    lane = lax.broadcasted_iota(jnp.int32, c.shape, 1)
    cs_ref[:, 0:LANES] = jnp.where(lane < QK_ROPE, c, 0.0)
    cs_ref[:, LANES:2 * LANES] = jnp.where(lane < half, -s, 0.0)
    cs_ref[:, 2 * LANES:3 * LANES] = jnp.where((lane >= half) & (lane < QK_ROPE), s, 0.0)


def _rope_tables(positions):
    t = positions.size
    tm = 512
    inv = 1.0 / (ROPE_BASE ** (np.arange(0, QK_ROPE, 2, dtype=np.float32) / QK_ROPE))
    inv_col = np.concatenate([inv, inv]).reshape(QK_ROPE, 1).astype(np.float32)
    return pl.pallas_call(
        _rope_kernel,
        grid=(t // tm,),
        in_specs=[pl.BlockSpec((1, tm), lambda i: (0, i)),
                  pl.BlockSpec((QK_ROPE, 1), lambda i: (0, 0))],
        out_specs=[pl.BlockSpec((tm, 3 * LANES), lambda i: (i, 0)),
                   pl.BlockSpec((QK_ROPE, tm), lambda i: (0, i))],
        out_shape=[jax.ShapeDtypeStruct((t, 3 * LANES), F32),
                   jax.ShapeDtypeStruct((QK_ROPE, t), F32)],
        compiler_params=_cparams(("parallel",)),
        name="rope_tables",
    )(positions.reshape(1, t), jnp.asarray(inv_col))


def _rope_rot(x, cs):
    c = cs[:, 0:LANES]
    sa = cs[:, LANES:2 * LANES]
    sb = cs[:, 2 * LANES:3 * LANES]
    return (x * c + pltpu.roll(x, LANES - QK_ROPE // 2, 1) * sa
            + pltpu.roll(x, QK_ROPE // 2, 1) * sb)


def _mla_latent(x, gmix_ref, win_ref):
    return _dot(_rms(x, gmix_ref[...]).astype(BF16), win_ref[...])


def _mla_expand(hcat, gqa_ref, wqb_ref, gkva_ref, wkb_ref, wvt_ref):
    cq = hcat[:, :Q_LORA]
    ckv = hcat[:, Q_LORA:Q_LORA + KV_LORA]
    kr = hcat[:, Q_LORA + KV_LORA + LANES:Q_LORA + KV_LORA + 2 * LANES]
    qft = _dot_nt(wqb_ref[...], _rms(cq, gqa_ref[...]).astype(BF16))
    ckvn = _rms(ckv, gkva_ref[...]).astype(BF16)
    kf = _dot(ckvn, wkb_ref[...])
    vt = _dot_nt(wvt_ref[...], ckvn).astype(BF16)
    return qft, kf, kr, vt


def _mla_q_heads(qft, cst, gqn_ref, qt_ref, s):
    half = QK_ROPE // 2
    c, sn = cst[0:half, :], cst[half:QK_ROPE, :]
    g = gqn_ref[...]
    zero = jnp.zeros((HEAD_PAD - QK_HEAD, qft.shape[1]), BF16)
    for h in range(MLA_HEADS):
        blk = qft[h * HEAD_PAD:h * HEAD_PAD + QK_HEAD, :]
        ss = jnp.sum(blk * blk, axis=0, keepdims=True)
        inv = lax.rsqrt(ss * (1.0 / QK_HEAD) + NORM_EPS)
        y = blk * g * inv
        x1, x2 = y[QK_NOPE:QK_NOPE + half, :], y[QK_NOPE + half:QK_HEAD, :]
        base = h * HEAD_PAD
        qt_ref[s, base:base + QK_NOPE, :] = y[0:QK_NOPE, :].astype(BF16)
        qt_ref[s, base + QK_NOPE:base + QK_NOPE + half, :] = (x1 * c - x2 * sn).astype(BF16)
        qt_ref[s, base + QK_NOPE + half:base + QK_HEAD, :] = (x1 * sn + x2 * c).astype(BF16)
        qt_ref[s, base + QK_HEAD:base + HEAD_PAD, :] = zero


def _mla_k_heads(kf, kr, cs, gkn_ref, k_ref, rows):
    gkn_n, gkn_r = gkn_ref[:, 0:LANES], gkn_ref[:, LANES:2 * LANES]
    kr_ss = jnp.sum(kr * kr, axis=-1, keepdims=True)
    kr_rot = _rope_rot(kr * gkn_r, cs)
    for h in range(MLA_HEADS):
        kn = kf[:, h * LANES:(h + 1) * LANES]
        ssk = jnp.sum(kn * kn, axis=-1, keepdims=True) + kr_ss
        invk = lax.rsqrt(ssk * (1.0 / QK_HEAD) + NORM_EPS)
        k_ref[rows, h * HEAD_PAD:h * HEAD_PAD + LANES] = (kn * invk * gkn_n).astype(BF16)
        k_ref[rows, h * HEAD_PAD + LANES:(h + 1) * HEAD_PAD] = (kr_rot * invk).astype(BF16)


def _mla_qkv_body(x, cs_ref, cst_ref, gmix_ref, win_ref, gqa_ref, wqb_ref, gkva_ref, wkb_ref,
                  wvt_ref, gqn_ref, gkn_ref, qt_ref, k_ref, vt_ref):
    nsub = x.shape[0] // TQ
    rows = [slice(s * TQ, (s + 1) * TQ) for s in range(nsub)]
    hcats = [_mla_latent(x[r], gmix_ref, win_ref) for r in rows]
    mids = [_mla_expand(hc, gqa_ref, wqb_ref, gkva_ref, wkb_ref, wvt_ref) for hc in hcats]
    for s, (qft, kf, kr, vt) in enumerate(mids):
        vt_ref[s] = vt
    for s, (r, (qft, kf, kr, vt)) in enumerate(zip(rows, mids)):
        _mla_q_heads(qft, cst_ref[:, r], gqn_ref, qt_ref, s)
        _mla_k_heads(kf, kr, cs_ref[r, :], gkn_ref, k_ref, r)


def _gated_sum(x1_ref, gt_ref, y0_ref, y1_ref):
    gt = gt_ref[...]
    g0, g1 = gt[:, 0:1], gt[:, 1:2]
    y = jnp.concatenate([g0 * a + g1 * b
                         for a, b in zip(_unpack_planes(y0_ref), _unpack_planes(y1_ref))], axis=1)
    return x1_ref[...] + y


def _mla_qkv_kernel(x_ref, *rest):
    _mla_qkv_body(x_ref[...], *rest)


def _mla_qkv_combine_kernel(x1_ref, gt_ref, y0_ref, y1_ref, *rest):
    *mid, xo_ref, q_ref, k_ref, v_ref = rest
    x = _gated_sum(x1_ref, gt_ref, y0_ref, y1_ref)
    xo_ref[...] = x
    _mla_qkv_body(x, *mid, q_ref, k_ref, v_ref)


def _mla_qkv(x_parts, cs, cst, gmix, w_in, g_qa, w_qbt, g_kva, w_kb, w_vt, g_qn, g_kn):
    t, d = x_parts[0].shape
    tm = TM_QKV if len(x_parts) == 1 else TM_QKV_COMBINE
    nsub = tm // TQ
    nt = t // tm
    row = lambda i: (i, 0)
    const = lambda i: (0, 0)
    h = MLA_HEADS
    weights = [gmix, w_in, g_qa, w_qbt, g_kva, w_kb, w_vt, g_qn, g_kn]
    w_specs = [pl.BlockSpec(w.shape, const) for w in weights]
    qkv_specs = [pl.BlockSpec((nsub, h * HEAD_PAD, TQ), lambda i: (i, 0, 0)),
                 pl.BlockSpec((tm, h * HEAD_PAD), row),
                 pl.BlockSpec((nsub, h * V_HEAD, TQ), lambda i: (i, 0, 0))]
    qkv_shapes = [jax.ShapeDtypeStruct((t // TQ, h * HEAD_PAD, TQ), BF16),
                  jax.ShapeDtypeStruct((t, h * HEAD_PAD), BF16),
                  jax.ShapeDtypeStruct((t // TQ, h * V_HEAD, TQ), BF16)]
    cs_specs = [pl.BlockSpec((tm, 3 * LANES), row), pl.BlockSpec((QK_ROPE, tm), lambda i: (0, i))]
    if len(x_parts) == 1:
        q, k, v = pl.pallas_call(
            _mla_qkv_kernel, grid=(nt,),
            in_specs=[pl.BlockSpec((tm, d), row)] + cs_specs + w_specs,
            out_specs=qkv_specs, out_shape=qkv_shapes,
            compiler_params=_cparams(("parallel",)), name="mla_qkv",
        )(x_parts[0], cs, cst, *weights)
        return x_parts[0], q, k, v
    x1, gates_tok, yg = x_parts
    x, q, k, v = pl.pallas_call(
        _mla_qkv_combine_kernel, grid=(nt,),
        in_specs=[pl.BlockSpec((tm, d), row), pl.BlockSpec((tm, LANES), row),
                  pl.BlockSpec((SC_PLANES, tm, d // (2 * SC_PLANES)), lambda i: (0, i, 0)),
                  pl.BlockSpec((SC_PLANES, tm, d // (2 * SC_PLANES)), lambda i: (0, i + nt, 0)),
                  ] + cs_specs + w_specs,
        out_specs=[pl.BlockSpec((tm, d), row)] + qkv_specs,
        out_shape=[jax.ShapeDtypeStruct((t, d), F32)] + qkv_shapes,
        compiler_params=_cparams(("parallel",)), name="mla_qkv_combine",
    )(x1, gates_tok, yg, yg, cs, cst, *weights)
    return x, q, k, v


def _attn_kernel(q_ref, k_ref, vt_ref, x_ref, wo_ref, g_ref, x1_ref, xn_ref,
                 s_ref, m_ref, acc_ref, o_ref):
    i = pl.program_id(1)
    ones = jnp.ones((ATTN_SUM_ROWS, TQ), BF16)

    def scores(h, j, n):
        off = pl.multiple_of(j * TQ, TQ)
        hs = slice(h * HEAD_PAD, (h + 1) * HEAD_PAD)
        s_ref[h, 0:n * TQ, :] = _dot(k_ref[pl.ds(off, n * TQ), hs], q_ref[0, hs, :])

    def weighted(h, j, p, n):
        out = None
        for t in range(n):
            lhs = jnp.concatenate([vt_ref[j + t, h * V_HEAD:(h + 1) * V_HEAD, :], ones], axis=0)
            part = _dot(lhs, p[t * TQ:(t + 1) * TQ, :].astype(BF16))
            out = part if out is None else out + part
        return out

    for h in range(MLA_HEADS):
        scores(h, i, 1)
    kc = lax.broadcasted_iota(jnp.int32, (TQ, TQ), 0) // CHUNK
    qc = lax.broadcasted_iota(jnp.int32, (TQ, TQ), 1) // CHUNK
    for h in range(MLA_HEADS):
        st = jnp.where(kc <= qc, s_ref[h, 0:TQ, :], -jnp.inf)
        m = jnp.max(st, axis=0, keepdims=True)
        acc_ref[h] = weighted(h, i, jnp.exp2(st - m), 1)
        m_ref[h] = m

    def full_tiles(j, n):
        for h in range(MLA_HEADS):
            scores(h, j, n)
        for h in range(MLA_HEADS):
            st = s_ref[h, 0:n * TQ, :]
            m_old = m_ref[h]
            m_new = jnp.maximum(m_old, jnp.max(st, axis=0, keepdims=True))
            acc_ref[h] = (jnp.exp2(m_old - m_new) * acc_ref[h]
                          + weighted(h, j, jnp.exp2(st - m_new), n))
            m_ref[h] = m_new

    def quad(jq, carry):
        full_tiles(4 * jq, 4)
        return carry

    lax.fori_loop(0, i // 4, quad, 0)
    rest = i % 4

    @pl.when(rest >= 2)
    def _():
        full_tiles(i - rest, 2)

    @pl.when(rest % 2 == 1)
    def _():
        full_tiles(i - 1, 1)

    for h in range(MLA_HEADS):
        acc = acc_ref[h]
        inv_l = 1.0 / acc[V_HEAD:V_HEAD + 1, :]
        o_ref[:, h * V_HEAD:(h + 1) * V_HEAD] = (acc[0:V_HEAD, :] * inv_l).T.astype(BF16)

    x1 = x_ref[...] + _dot(o_ref[...], wo_ref[...])
    x1_ref[...] = x1
    xn_ref[...] = _rms(x1, g_ref[...]).astype(BF16)


def _attention(q, k, vt, x, w_o, g_ffn, batch, seq):
    h = MLA_HEADS
    nq = seq // TQ
    t, d = x.shape
    tile = lambda b, i: (b * nq + i, 0)
    const = lambda b, i: (0, 0)
    return pl.pallas_call(
        _attn_kernel, grid=(batch, nq),
        in_specs=[pl.BlockSpec((1, h * HEAD_PAD, TQ), lambda b, i: (b * nq + i, 0, 0)),
                  pl.BlockSpec((seq, h * HEAD_PAD), lambda b, i: (b, 0)),
                  pl.BlockSpec((nq, h * V_HEAD, TQ), lambda b, i: (b, 0, 0)),
                  pl.BlockSpec((TQ, d), tile), pl.BlockSpec(w_o.shape, const),
                  pl.BlockSpec((1, d), const)],
        out_specs=[pl.BlockSpec((TQ, d), tile), pl.BlockSpec((TQ, d), tile)],
        out_shape=[jax.ShapeDtypeStruct((t, d), F32), jax.ShapeDtypeStruct((t, d), BF16)],
        scratch_shapes=[pltpu.VMEM((h, 4 * TQ, TQ), F32),
                        pltpu.VMEM((h, 1, TQ), F32),
                        pltpu.VMEM((h, V_HEAD + ATTN_SUM_ROWS, TQ), F32),
                        pltpu.VMEM((TQ, h * V_HEAD), BF16)],
        compiler_params=_cparams(("parallel", "arbitrary")), name="attention",
    )(q, k, vt, x, w_o, g_ffn)


def _swiglu_chunk(x, w1, w3, w2):
    a = _dot(x, w1)
    b = _dot(x, w3)
    hid = (a * jax.nn.sigmoid(a)) * b
    return _dot(hid.astype(BF16), w2)


def _weight_chunk_copies(w1_hbm, w3_hbm, w2_hbm, lead, c, tf, slot, w1b, w3b, w2b, sem):
    cols = pl.ds(pl.multiple_of(c * tf, tf), tf)
    return (pltpu.make_async_copy(w1_hbm.at[lead + (slice(None), cols)], w1b.at[slot], sem.at[0, slot]),
            pltpu.make_async_copy(w3_hbm.at[lead + (slice(None), cols)], w3b.at[slot], sem.at[1, slot]),
            pltpu.make_async_copy(w2_hbm.at[lead + (cols, slice(None))], w2b.at[slot], sem.at[2, slot]))


def _ffn_kernel(xn_ref, x1_ref, w1_hbm, w3_hbm, w2_hbm, o_ref, w1b, w3b, w2b, sem, *, layer, tf):
    i = pl.program_id(0)
    n_tiles = pl.num_programs(0)
    ff = w1_hbm.shape[2]
    chunks = [(lo, min(tf, ff - lo)) for lo in range(0, ff, tf)]
    assert len(chunks) % 2 == 0

    def copies(c):
        lo, width = chunks[c]
        slot = c % 2
        cols = pl.ds(lo, width)
        return (pltpu.make_async_copy(w1_hbm.at[layer, :, cols], w1b.at[slot, :, 0:width],
                                      sem.at[0, slot]),
                pltpu.make_async_copy(w3_hbm.at[layer, :, cols], w3b.at[slot, :, 0:width],
                                      sem.at[1, slot]),
                pltpu.make_async_copy(w2_hbm.at[layer, cols, :], w2b.at[slot, 0:width, :],
                                      sem.at[2, slot]))

    @pl.when(i == 0)
    def _():
        for cp in copies(0):
            cp.start()

    o_ref[...] = x1_ref[...]
    for c, (lo, width) in enumerate(chunks):
        if c + 1 < len(chunks):
            for cp in copies(c + 1):
                cp.start()
        else:
            @pl.when(i + 1 < n_tiles)
            def _():
                for cp in copies(0):
                    cp.start()
        for cp in copies(c):
            cp.wait()
        slot = c % 2
        o_ref[...] += _swiglu_chunk(xn_ref[...], w1b[slot, :, 0:width].astype(BF16),
                                    w3b[slot, :, 0:width].astype(BF16),
                                    w2b[slot, 0:width, :].astype(BF16))


def _dense_ffn(xn, x1, w1, w3, w2, layer):
    t, d = x1.shape
    tm, tf = TM_FFN, TF_FFN
    row = lambda i: (i, 0)
    hbm = pl.BlockSpec(memory_space=pl.ANY)
    return pl.pallas_call(
        functools.partial(_ffn_kernel, layer=layer, tf=tf), grid=(t // tm,),
        in_specs=[pl.BlockSpec((tm, d), row), pl.BlockSpec((tm, d), row), hbm, hbm, hbm],
        out_specs=pl.BlockSpec((tm, d), row),
        out_shape=jax.ShapeDtypeStruct((t, d), F32),
        scratch_shapes=[pltpu.VMEM((2, d, tf), F32), pltpu.VMEM((2, d, tf), F32),
                        pltpu.VMEM((2, tf, d), F32), pltpu.SemaphoreType.DMA((3, 2))],
        compiler_params=_cparams(("arbitrary",)), name="dense_ffn",
    )(xn, x1, w1, w3, w2)


def _lru_kernel(x_ref, gmix_ref, win_ref, bin_ref, cw_ref, cb_ref, wga_ref, bga_ref,
                wgi_ref, bgi_ref, lam_ref, wout_ref, bout_ref, gffn_ref,
                x1_ref, xn_ref,
                xpad_ref, a_ref, u_ref, gate_ref, yg_ref, h_ref):
    nb, tt, d = x_ref.shape
    w = lam_ref.shape[1]
    m = nb * tt

    @pl.when(pl.program_id(0) == 0)
    def _():
        h_ref[...] = jnp.zeros_like(h_ref)
        xpad_ref[:, 0:SUBLANES, :] = jnp.zeros((nb, SUBLANES, w), F32)

    x = x_ref[...].reshape(m, d)
    xn = _rms(x, gmix_ref[...]).astype(BF16)
    hcat = _dot(xn, win_ref[...]) + bin_ref[...]
    gate_ref[...] = jax.nn.gelu(hcat[:, :w], approximate=True)
    xpad_ref[:, SUBLANES:, :] = hcat[:, w:].reshape(nb, tt, w)

    xc = cb_ref[...].reshape(1, 1, w) + jnp.zeros((nb, tt, w), F32)
    for j in range(CONV_WIDTH):
        lo = SUBLANES - (CONV_WIDTH - 1) + j
        xc = xc + xpad_ref[:, lo:lo + tt, :] * cw_ref[j:j + 1, :].reshape(1, 1, w)
    xpad_ref[:, 0:SUBLANES, :] = xpad_ref[:, tt:tt + SUBLANES, :]
    xc = xc.reshape(m, w)

    lam = lam_ref[...]
    log_sig = jnp.minimum(lam, 0.0) - jnp.log1p(jnp.exp(-jnp.abs(lam)))
    for hh in range(LRU_HEADS):
        sl = slice(hh * LANES, (hh + 1) * LANES)
        xh = xc[:, sl]
        xhb = xh.astype(BF16)
        r = jax.nn.sigmoid(_dot(xhb, wga_ref[hh]) + bga_ref[hh:hh + 1, :])
        ig = jax.nn.sigmoid(_dot(xhb, wgi_ref[hh]) + bgi_ref[hh:hh + 1, :])
        log_a = LRU_C * r * log_sig[:, sl]
        a = jnp.exp(log_a)
        z = 1.0 - a * a
        u = (z * lax.rsqrt(jnp.maximum(z, 1e-30))) * (ig * xh)
        for b in range(nb):
            a_ref[hh, b * LRU_PITCH:b * LRU_PITCH + tt, :] = a[b * tt:(b + 1) * tt, :]
            u_ref[hh, b * LRU_PITCH:b * LRU_PITCH + tt, :] = u[b * tt:(b + 1) * tt, :]

    def scan_step(t, hs):
        new = []
        for hh in range(LRU_HEADS):
            rows = pl.ds(t, nb, stride=LRU_PITCH)
            hv = a_ref[hh, rows, :] * hs[hh] + u_ref[hh, rows, :]
            u_ref[hh, rows, :] = hv
            new.append(hv)
        return tuple(new)

    hs = lax.fori_loop(0, tt, scan_step, tuple(h_ref[hh] for hh in range(LRU_HEADS)), unroll=4)
    for hh in range(LRU_HEADS):
        h_ref[hh] = hs[hh]

    for hh in range(LRU_HEADS):
        sl = slice(hh * LANES, (hh + 1) * LANES)
        for b in range(nb):
            y = u_ref[hh, b * LRU_PITCH:b * LRU_PITCH + tt, :]
            yg_ref[b * tt:(b + 1) * tt, sl] = (y * gate_ref[b * tt:(b + 1) * tt, sl]).astype(BF16)

    x1 = x + _dot(yg_ref[...], wout_ref[...]) + bout_ref[...]
    x1_ref[...] = x1.reshape(nb, tt, d)
    xn_ref[...] = _pack_pairs(_rms(x1, gffn_ref[...])).reshape(nb, tt, d // 2)


def _lru_mixer(x3, gmix, w_in, b_in, conv_w, conv_b, wga, bga, wgi, bgi, lam, w_out, b_out, g_ffn):
    nb, seq, d = x3.shape
    w = lam.shape[1]
    tt = T_LRU
    m = nb * tt
    consts = [gmix, w_in, b_in, conv_w, conv_b, wga, bga, wgi, bgi, lam, w_out, b_out, g_ffn]
    const_specs = [pl.BlockSpec(c.shape, (lambda i, n=c.ndim: (0,) * n)) for c in consts]
    blk = pl.BlockSpec((nb, tt, d), lambda i: (0, i, 0))
    return pl.pallas_call(
        _lru_kernel, grid=(seq // tt,),
        in_specs=[blk] + const_specs,
        out_specs=[blk, pl.BlockSpec((nb, tt, d // 2), lambda i: (0, i, 0))],
        out_shape=[jax.ShapeDtypeStruct((nb, seq, d), F32),
                   jax.ShapeDtypeStruct((nb, seq, d // 2), jnp.int32)],
        scratch_shapes=[pltpu.VMEM((nb, tt + SUBLANES, w), F32),
                        pltpu.VMEM((LRU_HEADS, nb * LRU_PITCH, LANES), F32),
                        pltpu.VMEM((LRU_HEADS, nb * LRU_PITCH, LANES), F32),
                        pltpu.VMEM((m, w), F32),
                        pltpu.VMEM((m, w), BF16),
                        pltpu.VMEM((LRU_HEADS, nb, LANES), F32)],
        compiler_params=_cparams(("arbitrary",)), name="rglru_mixer",
    )(x3, *consts)


def _router_kernel(x1_ref, g_ref, wh_ref, wl_ref, br_ref, slot_ref, gate_ref, te_ref, tv_ref,
                   lg_ref, tri_ref, cnt_ref, run_ref, start_ref, *, tile):
    phase = pl.program_id(0)
    j = pl.program_id(1)
    ne = N_EXPERTS
    tr = x1_ref.shape[0]

    @pl.when((phase == 0) & (j == 0))
    def _():
        cnt_ref[...] = jnp.zeros_like(cnt_ref)
        tri_ref[...] = jnp.where(lax.broadcasted_iota(jnp.int32, (tr, tr), 0)
                                 < lax.broadcasted_iota(jnp.int32, (tr, tr), 1),
                                 1.0, 0.0).astype(BF16)

    @pl.when(phase == 0)
    def _():
        x = _rms(x1_ref[...], g_ref[...])
        xh = x.astype(BF16)
        xl = (x - xh.astype(F32)).astype(BF16)
        wh = wh_ref[...]
        nat = _dot(xh, wh) + _dot(xl, wh) + _dot(xh, wl_ref[...])
        lg_ref[j] = nat.T[0:ne, :] + br_ref[...]

    logits = lg_ref[j]
    eidx = lax.broadcasted_iota(jnp.int32, (ne, tr), 0)
    m1 = jnp.max(logits, axis=0, keepdims=True)
    i1 = jnp.min(jnp.where(logits == m1, eidx, ne), axis=0, keepdims=True)
    oh1 = eidx == i1
    rest = jnp.where(oh1, -jnp.inf, logits)
    m2 = jnp.max(rest, axis=0, keepdims=True)
    i2 = jnp.min(jnp.where(rest == m2, eidx, ne), axis=0, keepdims=True)
    oh2 = eidx == i2
    oh = jnp.where(oh1 | oh2, 1.0, 0.0)
    tile_cnt = jnp.sum(oh, axis=1, keepdims=True)

    @pl.when(phase == 0)
    def _():
        cnt_ref[...] += tile_cnt

    @pl.when((phase == 1) & (j == 0))
    def _():
        cnt = cnt_ref[...]
        padded = jnp.ceil(cnt * (1.0 / tile)) * tile
        sub = lax.broadcasted_iota(jnp.int32, (ne, 1), 0)
        start = jnp.zeros((ne, 1), F32)
        for e in range(ne - 1):
            start = start + jnp.where(sub > e, padded[e:e + 1, :], 0.0)
        start_ref[...] = start
        run_ref[...] = jnp.zeros_like(run_ref)
        tile_start = lax.broadcasted_iota(jnp.int32, (ne, LANES), 1).astype(F32) * tile
        owner = jnp.sum(jnp.where(tile_start >= start + padded, 1, 0), axis=0, keepdims=True)
        owner = jnp.minimum(owner, ne - 1)
        esub = lax.broadcasted_iota(jnp.int32, (ne, LANES), 0)
        real = jnp.clip(cnt - (tile_start - start), 0.0, float(tile))
        te_ref[...] = owner
        tv_ref[...] = jnp.sum(jnp.where(esub == owner, real, 0.0), axis=0,
                              keepdims=True).astype(jnp.int32)

    @pl.when(phase == 1)
    def _():
        before = _dot(oh.astype(BF16), tri_ref[...])
        slot_e = start_ref[...] + run_ref[...] + before
        s1 = jnp.sum(jnp.where(oh1, slot_e, 0.0), axis=0, keepdims=True)
        s2 = jnp.sum(jnp.where(oh2, slot_e, 0.0), axis=0, keepdims=True)
        slot_ref[0:1, :] = s1.astype(jnp.int32)
        slot_ref[1:2, :] = s2.astype(jnp.int32)
        e21 = jnp.exp(m2 - m1)
        g1 = 1.0 / (1.0 + e21)
        g2 = e21 * g1
        rowi = lax.broadcasted_iota(jnp.int32, (LANES, tr), 0)
        gmat = jnp.where(rowi == 0, g1, jnp.where(rowi == 1, g2, 0.0))
        gate_ref[...] = gmat.T
        run_ref[...] += tile_cnt


def _router(x1, g_ffn, w_router, b_router, tile):
    t, d = x1.shape
    tr = TR
    nt = t // tr
    wp = jnp.pad(w_router, ((0, 0), (0, LANES - N_EXPERTS)))
    wh = wp.astype(BF16)
    wl = (wp - wh.astype(F32)).astype(BF16)
    const = lambda p, j: (0, 0)
    return pl.pallas_call(
        functools.partial(_router_kernel, tile=tile), grid=(2, nt),
        in_specs=[pl.BlockSpec((tr, d), lambda p, j: (j * (1 - p) + (nt - 1) * p, 0)),
                  pl.BlockSpec((1, d), const),
                  pl.BlockSpec(wh.shape, const), pl.BlockSpec(wl.shape, const),
                  pl.BlockSpec((N_EXPERTS, 1), const)],
        out_specs=[pl.BlockSpec((2, tr), lambda p, j: (0, j * p)),
                   pl.BlockSpec((tr, LANES), lambda p, j: (j * p, 0)),
                   pl.BlockSpec((1, LANES), const), pl.BlockSpec((1, LANES), const)],
        out_shape=[jax.ShapeDtypeStruct((2, t), jnp.int32),
                   jax.ShapeDtypeStruct((t, LANES), F32),
                   jax.ShapeDtypeStruct((1, LANES), jnp.int32),
                   jax.ShapeDtypeStruct((1, LANES), jnp.int32)],
        scratch_shapes=[pltpu.VMEM((nt, N_EXPERTS, tr), F32), pltpu.VMEM((tr, tr), BF16),
                        pltpu.VMEM((N_EXPERTS, 1), F32), pltpu.VMEM((N_EXPERTS, 1), F32),
                        pltpu.VMEM((N_EXPERTS, 1), F32)],
        compiler_params=_cparams(("arbitrary", "arbitrary")), name="moe_router",
    )(x1, g_ffn, wh, wl, b_router.reshape(N_EXPERTS, 1))


def _sc_mesh():
    return plsc.VectorSubcoreMesh(core_axis_name="c", subcore_axis_name="s")


def _sc_dispatch(xn, slots, n_slots):
    t, d = xn.shape
    win = SC_WINDOW
    nwin = t // win
    dp = d // SC_PLANES
    idx = slots.reshape(1, 2 * t)

    @functools.partial(pl.kernel,
                       out_type=jax.ShapeDtypeStruct((SC_PLANES, n_slots, dp), xn.dtype),
                       mesh=_sc_mesh(), scratch_types=[], name="moe_dispatch")
    def run(x_hbm, i_hbm, o_hbm):
        for c in range(SC_PLANES):
            def body(x_vmem, i0_vmem, i1_vmem, c=c):
                pltpu.sync_copy(x_vmem, o_hbm.at[c].at[i0_vmem.at[0]])
                pltpu.sync_copy(x_vmem, o_hbm.at[c].at[i1_vmem.at[0]])

            pltpu.emit_pipeline(
                body, grid=(nwin,),
                in_specs=[pl.BlockSpec((win, dp), lambda i, c=c: (i, c)),
                          pl.BlockSpec((1, win), lambda i: (0, i)),
                          pl.BlockSpec((1, win), lambda i: (0, i + nwin))],
                out_specs=[],
                core_axis_name=("c", "s"),
                dimension_semantics=(pltpu.PARALLEL,),
            )(x_hbm, i_hbm, i_hbm)

    return run(xn, idx)


def _sc_combine(y, slots):
    n2 = slots.size
    dp = y.shape[2]
    win = SC_WINDOW
    idx = slots.reshape(1, n2)

    @functools.partial(pl.kernel, out_type=jax.ShapeDtypeStruct((SC_PLANES, n2, dp), y.dtype),
                       mesh=_sc_mesh(), scratch_types=[], name="moe_combine")
    def run(y_hbm, i_hbm, o_hbm):
        for c in range(SC_PLANES):
            def body(i_vmem, o_vmem, c=c):
                pltpu.sync_copy(y_hbm.at[c].at[i_vmem.at[0]], o_vmem)

            pltpu.emit_pipeline(
                body, grid=(n2 // win,),
                in_specs=[pl.BlockSpec((1, win), lambda i: (0, i))],
                out_specs=[pl.BlockSpec((win, dp), lambda i: (i, 0))],
                core_axis_name=("c", "s"),
                dimension_semantics=(pltpu.PARALLEL,),
            )(i_hbm, o_hbm.at[c])

    return run(y, idx)


def _moe_ffn_kernel(te_ref, tv_ref, xs_ref, w1_hbm, w3_hbm, w2_hbm, y_ref,
                    xb_ref, acc_ref, w1b, w3b, w2b, wb1_ref, wb3_ref, wb2_ref, sem, *, layer, tf):
    i = pl.program_id(0)
    n_tiles = pl.num_programs(0)
    valid = tv_ref[i]
    tm, dp = xs_ref.shape[1:]
    ts = TSUB_MOE
    nc = w1_hbm.shape[3] // tf

    def copies(tile, c, slot):
        return _weight_chunk_copies(w1_hbm, w3_hbm, w2_hbm, (layer, te_ref[tile]), c, tf, slot,
                                    w1b, w3b, w2b, sem)

    @pl.when((i == 0) & (valid > 0))
    def _():
        for cp in copies(0, 0, 0):
            cp.start()

    rows = lax.broadcasted_iota(jnp.int32, (tm, 1), 0)
    for c, piece in enumerate(_unpack_planes(xs_ref)):
        xb_ref[:, c * dp:(c + 1) * dp] = jnp.where(rows < valid, piece, 0.0).astype(BF16)
    acc_ref[...] = jnp.zeros_like(acc_ref)

    def rows_block(lo, n, w1, w3, w2):
        acc_ref[lo:lo + n, :] += _swiglu_chunk(xb_ref[lo:lo + n, :], w1, w3, w2)

    @pl.when(valid > 0)
    def _():
        nxt_tile = jnp.minimum(i + 1, n_tiles - 1)
        next_tile_live = (i + 1 < n_tiles) & (tv_ref[nxt_tile] > 0)

        def chunk(c, carry):
            slot = lax.rem(i * nc + c, 2)
            last = c + 1 == nc

            @pl.when(jnp.logical_not(last) | next_tile_live)
            def _():
                for cp in copies(jnp.where(last, nxt_tile, i), jnp.where(last, 0, c + 1), 1 - slot):
                    cp.start()

            for cp in copies(i, c, slot):
                cp.wait()

            @pl.when(valid > tm - ts)
            def _():
                rows_block(0, tm, w1b[slot].astype(BF16), w3b[slot].astype(BF16),
                           w2b[slot].astype(BF16))

            @pl.when(valid <= tm - ts)
            def _():
                wb1_ref[...] = w1b[slot].astype(BF16)
                wb3_ref[...] = w3b[slot].astype(BF16)
                wb2_ref[...] = w2b[slot].astype(BF16)
                weights = lambda: (wb1_ref[...], wb3_ref[...], wb2_ref[...])

                @pl.when(valid <= ts)
                def _():
                    rows_block(0, ts, *weights())

                @pl.when(valid > ts)
                def _():
                    rows_block(0, 2 * ts, *weights())

                @pl.when(valid > 2 * ts)
                def _():
                    rows_block(2 * ts, ts, *weights())

            return carry

        lax.fori_loop(0, nc, chunk, 0)

    packed = _pack_pairs(acc_ref[...])
    for c in range(SC_PLANES):
        y_ref[c] = packed[:, c * dp:(c + 1) * dp]


def _moe_ffn(xs, tile_expert, tile_valid, w1, w3, w2, layer):
    planes, ns, dp = xs.shape
    d = 2 * planes * dp
    tm, tf = TM_MOE, TF_MOE
    assert tm == 4 * TSUB_MOE
    hbm = pl.BlockSpec(memory_space=pl.ANY)
    grid_spec = pltpu.PrefetchScalarGridSpec(
        num_scalar_prefetch=2, grid=(ns // tm,),
        in_specs=[pl.BlockSpec((planes, tm, dp), lambda i, te, tv: (0, i, 0)), hbm, hbm, hbm],
        out_specs=pl.BlockSpec((planes, tm, dp), lambda i, te, tv: (0, i, 0)),
        scratch_shapes=[pltpu.VMEM((tm, d), BF16), pltpu.VMEM((tm, d), F32),
                        pltpu.VMEM((2, d, tf), F32), pltpu.VMEM((2, d, tf), F32),
                        pltpu.VMEM((2, tf, d), F32),
                        pltpu.VMEM((d, tf), BF16), pltpu.VMEM((d, tf), BF16),
                        pltpu.VMEM((tf, d), BF16), pltpu.SemaphoreType.DMA((3, 2))])
    return pl.pallas_call(
        functools.partial(_moe_ffn_kernel, layer=layer, tf=tf), grid_spec=grid_spec,
        out_shape=jax.ShapeDtypeStruct((planes, ns, dp), jnp.int32),
        compiler_params=_cparams(("arbitrary",)), name="moe_ffn",
    )(tile_expert, tile_valid, xs, w1, w3, w2)


def _combine_kernel(x1_ref, gt_ref, y0_ref, y1_ref, o_ref):
    o_ref[...] = _gated_sum(x1_ref, gt_ref, y0_ref, y1_ref)


def _combine(x1, gates_tok, yg):
    t, d = x1.shape
    tm = 512
    nt = t // tm
    row = lambda i: (i, 0)
    return pl.pallas_call(
        _combine_kernel, grid=(nt,),
        in_specs=[pl.BlockSpec((tm, d), row), pl.BlockSpec((tm, LANES), row),
                  pl.BlockSpec((SC_PLANES, tm, d // (2 * SC_PLANES)), lambda i: (0, i, 0)),
                  pl.BlockSpec((SC_PLANES, tm, d // (2 * SC_PLANES)), lambda i: (0, i + nt, 0))],
        out_specs=pl.BlockSpec((tm, d), row),
        out_shape=jax.ShapeDtypeStruct((t, d), F32),
        compiler_params=_cparams(("parallel",)), name="moe_combine_residual",
    )(x1, gates_tok, yg, yg)


def _mla_weights(w_in, w_qb, w_kvb, g_qn, g_kn):
    d = w_in.shape[0]
    h = MLA_HEADS
    lat = Q_LORA + KV_LORA
    win = jnp.zeros((d, lat + HEAD_PAD), F32)
    win = win.at[:, :lat].set(w_in[:, :lat])
    win = win.at[:, lat + LANES:lat + LANES + QK_ROPE].set(w_in[:, lat:])
    wq = w_qb.reshape(Q_LORA, h, QK_HEAD)
    wq = jnp.pad(wq, ((0, 0), (0, 0), (0, HEAD_PAD - QK_HEAD))).reshape(Q_LORA, h * HEAD_PAD)
    wkv = w_kvb.reshape(KV_LORA, h, QK_NOPE + V_HEAD)
    wk = wkv[:, :, :QK_NOPE].reshape(KV_LORA, h * QK_NOPE)
    wvt = wkv[:, :, QK_NOPE:].reshape(KV_LORA, h * V_HEAD).T
    scale = math.log2(math.e) / math.sqrt(QK_HEAD)
    gq = jnp.broadcast_to((g_qn * scale).reshape(QK_HEAD, 1), (QK_HEAD, TQ))
    gk = jnp.pad(g_kn, (0, HEAD_PAD - QK_HEAD)).reshape(1, HEAD_PAD)
    return (win.astype(BF16), wq.T.astype(BF16), wk.astype(BF16), wvt.astype(BF16), gq, gk)


def kernel(x, positions, norm_mix, norm_ffn, mla_w_in, mla_g_qa, mla_w_qb, mla_g_kva, mla_w_kvb, mla_g_qn, mla_g_kn, mla_w_o, lru_w_in, lru_b_in, lru_conv_w, lru_conv_b, lru_w_gate_a, lru_b_gate_a, lru_w_gate_i, lru_b_gate_i, lru_lambda, lru_w_out, lru_b_out, ffn_w1, ffn_w3, ffn_w2, moe_w_router, moe_b_router, moe_w1, moe_w3, moe_w2):
    batch, seq, d = x.shape
    t = batch * seq
    depth = norm_mix.shape[0]
    n_slots = 2 * t + N_EXPERTS * TM_MOE
    n_tiles = n_slots // TM_MOE
    row = lambda v: v.reshape(1, -1)

    cs, cst = _rope_tables(positions)
    parts = (x.reshape(t, d),)
    for i in range(depth):
        j = i // 2
        if i % 2 == 0:
            w_in, w_qbt, w_kb, w_vt, g_qn, g_kn = _mla_weights(
                mla_w_in[j], mla_w_qb[j], mla_w_kvb[j], mla_g_qn[j], mla_g_kn[j])
            xr, q, k, vt = _mla_qkv(parts, cs, cst, row(norm_mix[i]), w_in, row(mla_g_qa[j]),
                                    w_qbt, row(mla_g_kva[j]), w_kb, w_vt, g_qn, g_kn)
            x1, xn = _attention(q, k, vt, xr, mla_w_o[j].astype(BF16), row(norm_ffn[i]),
                                batch, seq)
            parts = (_dense_ffn(xn, x1, ffn_w1, ffn_w3, ffn_w2, j),)
        else:
            (xr,) = parts
            x1, xn = _lru_mixer(
                xr.reshape(batch, seq, d), row(norm_mix[i]), lru_w_in[j].astype(BF16),
                row(lru_b_in[j]), lru_conv_w[j], row(lru_conv_b[j]),
                lru_w_gate_a[j].astype(BF16), lru_b_gate_a[j], lru_w_gate_i[j].astype(BF16),
                lru_b_gate_i[j], row(lru_lambda[j]), lru_w_out[j].astype(BF16),
                row(lru_b_out[j]), row(norm_ffn[i]))
            x1 = x1.reshape(t, d)
            xn = xn.reshape(t, d // 2)
            slots, gates_tok, te, tv = _router(x1, row(norm_ffn[i]), moe_w_router[j],
                                               moe_b_router[j], TM_MOE)
            xs = _sc_dispatch(xn, slots, n_slots)
            y = _moe_ffn(xs, te[0, :n_tiles], tv[0, :n_tiles], moe_w1, moe_w3, moe_w2, j)
            yg = _sc_combine(y, slots)
            parts = (x1, gates_tok, yg)
    if len(parts) == 3:
        out = _combine(*parts)
    else:
        out = parts[0]
    return out.reshape(batch, seq, d)
```

```python
import functools
import math

import numpy as np
import jax
import jax.numpy as jnp
from jax import lax
from jax.experimental import pallas as pl
from jax.experimental.pallas import tpu as pltpu
from jax.experimental.pallas import tpu_sc as plsc

F32 = jnp.float32
BF16 = jnp.bfloat16

NORM_EPS = 1e-6
CHUNK = 64
MLA_HEADS = 8
QK_NOPE = 128
QK_ROPE = 64
QK_HEAD = QK_NOPE + QK_ROPE
V_HEAD = 128
Q_LORA = 384
KV_LORA = 256
ROPE_BASE = 10000.0
LRU_HEADS = 8
CONV_WIDTH = 4
LRU_C = 8.0
N_EXPERTS = 8

LANES = 128
SUBLANES = 8
HEAD_PAD = 2 * LANES
VMEM_LIMIT = 56 * 1024 * 1024

TQ = 256
TM_QKV = 4 * TQ
TM_QKV_COMBINE = 2 * TQ
ATTN_SUM_ROWS = 16
TM_FFN = 1024
TF_FFN = 512
T_LRU = 64
LRU_PITCH = T_LRU + SUBLANES
TR = 1024
TM_MOE = 1024
TSUB_MOE = 256
TF_MOE = 512
SC_WINDOW = 128
SC_PLANES = 2


def _cparams(sem):
    return pltpu.CompilerParams(dimension_semantics=sem, vmem_limit_bytes=VMEM_LIMIT)


def _rms(x, g):
    return x * lax.rsqrt(jnp.mean(x * x, axis=-1, keepdims=True) + NORM_EPS) * g


def _dot(a, b):
    return jnp.dot(a, b, preferred_element_type=F32)


def _dot_nt(a, b):
    return lax.dot_general(a, b, (((1,), (1,)), ((), ())), preferred_element_type=F32)


def _pack_pairs(x):
    n = x.shape[1] // 2
    bits = lax.bitcast_convert_type(x.astype(BF16).astype(F32), jnp.int32)
    return bits[:, :n] | lax.shift_right_logical(bits[:, n:], 16)


def _unpack_pairs(word):
    hi = lax.bitcast_convert_type(word & jnp.int32(-65536), F32)
    lo = lax.bitcast_convert_type(lax.shift_left(word, 16), F32)
    return hi, lo


def _unpack_planes(ref):
    halves = [_unpack_pairs(ref[c]) for c in range(SC_PLANES)]
    return [h for h, _ in halves] + [l for _, l in halves]


def _rope_kernel(pos_ref, inv_ref, cs_ref, cst_ref):
    ang = inv_ref[...] * pos_ref[...].astype(F32)
    half = QK_ROPE // 2
    ct = jnp.cos(ang)
    st = jnp.sin(ang)
    cst_ref[0:half, :] = ct[0:half, :]
    cst_ref[half:QK_ROPE, :] = st[0:half, :]
    pad = jnp.zeros((LANES - QK_ROPE, ang.shape[1]), F32)
    c = jnp.concatenate([ct, pad], axis=0).T
    s = jnp.concatenate([st, pad], axis=0).T
    lane = lax.broadcasted_iota(jnp.int32, c.shape, 1)
    cs_ref[:, 0:LANES] = jnp.where(lane < QK_ROPE, c, 0.0)
    cs_ref[:, LANES:2 * LANES] = jnp.where(lane < half, -s, 0.0)
    cs_ref[:, 2 * LANES:3 * LANES] = jnp.where((lane >= half) & (lane < QK_ROPE), s, 0.0)


def _rope_tables(positions):
    t = positions.size
    tm = 512
    inv = 1.0 / (ROPE_BASE ** (np.arange(0, QK_ROPE, 2, dtype=np.float32) / QK_ROPE))
    inv_col = np.concatenate([inv, inv]).reshape(QK_ROPE, 1).astype(np.float32)
    return pl.pallas_call(
        _rope_kernel,
        grid=(t // tm,),
        in_specs=[pl.BlockSpec((1, tm), lambda i: (0, i)),
                  pl.BlockSpec((QK_ROPE, 1), lambda i: (0, 0))],
        out_specs=[pl.BlockSpec((tm, 3 * LANES), lambda i: (i, 0)),
                   pl.BlockSpec((QK_ROPE, tm), lambda i: (0, i))],
        out_shape=[jax.ShapeDtypeStruct((t, 3 * LANES), F32),
                   jax.ShapeDtypeStruct((QK_ROPE, t), F32)],
        compiler_params=_cparams(("parallel",)),
        name="rope_tables",
    )(positions.reshape(1, t), jnp.asarray(inv_col))


def _rope_rot(x, cs):
    c = cs[:, 0:LANES]
    sa = cs[:, LANES:2 * LANES]
    sb = cs[:, 2 * LANES:3 * LANES]
    return (x * c + pltpu.roll(x, LANES - QK_ROPE // 2, 1) * sa
            + pltpu.roll(x, QK_ROPE // 2, 1) * sb)


def _mla_latent(x, gmix_ref, win_ref):
    return _dot(_rms(x, gmix_ref[...]).astype(BF16), win_ref[...])


def _mla_expand(hcat, gqa_ref, wqb_ref, gkva_ref, wkb_ref, wvt_ref):
    cq = hcat[:, :Q_LORA]
    ckv = hcat[:, Q_LORA:Q_LORA + KV_LORA]
    kr = hcat[:, Q_LORA + KV_LORA + LANES:Q_LORA + KV_LORA + 2 * LANES]
    qft = _dot_nt(wqb_ref[...], _rms(cq, gqa_ref[...]).astype(BF16))
    ckvn = _rms(ckv, gkva_ref[...]).astype(BF16)
    kf = _dot(ckvn, wkb_ref[...])
    vt = _dot_nt(wvt_ref[...], ckvn).astype(BF16)
    return qft, kf, kr, vt


def _mla_q_heads(qft, cst, gqn_ref, qt_ref, s):
    half = QK_ROPE // 2
    c, sn = cst[0:half, :], cst[half:QK_ROPE, :]
    g = gqn_ref[...]
    zero = jnp.zeros((HEAD_PAD - QK_HEAD, qft.shape[1]), BF16)
    for h in range(MLA_HEADS):
        blk = qft[h * HEAD_PAD:h * HEAD_PAD + QK_HEAD, :]
        ss = jnp.sum(blk * blk, axis=0, keepdims=True)
        inv = lax.rsqrt(ss * (1.0 / QK_HEAD) + NORM_EPS)
        y = blk * g * inv
        x1, x2 = y[QK_NOPE:QK_NOPE + half, :], y[QK_NOPE + half:QK_HEAD, :]
        base = h * HEAD_PAD
        qt_ref[s, base:base + QK_NOPE, :] = y[0:QK_NOPE, :].astype(BF16)
        qt_ref[s, base + QK_NOPE:base + QK_NOPE + half, :] = (x1 * c - x2 * sn).astype(BF16)
        qt_ref[s, base + QK_NOPE + half:base + QK_HEAD, :] = (x1 * sn + x2 * c).astype(BF16)
        qt_ref[s, base + QK_HEAD:base + HEAD_PAD, :] = zero


def _mla_k_heads(kf, kr, cs, gkn_ref, k_ref, rows):
    gkn_n, gkn_r = gkn_ref[:, 0:LANES], gkn_ref[:, LANES:2 * LANES]
    kr_ss = jnp.sum(kr * kr, axis=-1, keepdims=True)
    kr_rot = _rope_rot(kr * gkn_r, cs)
    for h in range(MLA_HEADS):
        kn = kf[:, h * LANES:(h + 1) * LANES]
        ssk = jnp.sum(kn * kn, axis=-1, keepdims=True) + kr_ss
        invk = lax.rsqrt(ssk * (1.0 / QK_HEAD) + NORM_EPS)
        k_ref[rows, h * HEAD_PAD:h * HEAD_PAD + LANES] = (kn * invk * gkn_n).astype(BF16)
        k_ref[rows, h * HEAD_PAD + LANES:(h + 1) * HEAD_PAD] = (kr_rot * invk).astype(BF16)


def _mla_qkv_body(x, cs_ref, cst_ref, gmix_ref, win_ref, gqa_ref, wqb_ref, gkva_ref, wkb_ref,
                  wvt_ref, gqn_ref, gkn_ref, qt_ref, k_ref, vt_ref):
    nsub = x.shape[0] // TQ
    rows = [slice(s * TQ, (s + 1) * TQ) for s in range(nsub)]
    hcats = [_mla_latent(x[r], gmix_ref, win_ref) for r in rows]
    mids = [_mla_expand(hc, gqa_ref, wqb_ref, gkva_ref, wkb_ref, wvt_ref) for hc in hcats]
    for s, (qft, kf, kr, vt) in enumerate(mids):
        vt_ref[s] = vt
    for s, (r, (qft, kf, kr, vt)) in enumerate(zip(rows, mids)):
        _mla_q_heads(qft, cst_ref[:, r], gqn_ref, qt_ref, s)
        _mla_k_heads(kf, kr, cs_ref[r, :], gkn_ref, k_ref, r)


def _gated_sum(x1_ref, gt_ref, y0_ref, y1_ref):
    gt = gt_ref[...]
    g0, g1 = gt[:, 0:1], gt[:, 1:2]
    y = jnp.concatenate([g0 * a + g1 * b
                         for a, b in zip(_unpack_planes(y0_ref), _unpack_planes(y1_ref))], axis=1)
    return x1_ref[...] + y


def _mla_qkv_kernel(x_ref, *rest):
    _mla_qkv_body(x_ref[...], *rest)


def _mla_qkv_combine_kernel(x1_ref, gt_ref, y0_ref, y1_ref, *rest):
    *mid, xo_ref, q_ref, k_ref, v_ref = rest
    x = _gated_sum(x1_ref, gt_ref, y0_ref, y1_ref)
    xo_ref[...] = x
    _mla_qkv_body(x, *mid, q_ref, k_ref, v_ref)


def _mla_qkv(x_parts, cs, cst, gmix, w_in, g_qa, w_qbt, g_kva, w_kb, w_vt, g_qn, g_kn):
    t, d = x_parts[0].shape
    tm = TM_QKV if len(x_parts) == 1 else TM_QKV_COMBINE
    nsub = tm // TQ
    nt = t // tm
    row = lambda i: (i, 0)
    const = lambda i: (0, 0)
    h = MLA_HEADS
    weights = [gmix, w_in, g_qa, w_qbt, g_kva, w_kb, w_vt, g_qn, g_kn]
    w_specs = [pl.BlockSpec(w.shape, const) for w in weights]
    qkv_specs = [pl.BlockSpec((nsub, h * HEAD_PAD, TQ), lambda i: (i, 0, 0)),
                 pl.BlockSpec((tm, h * HEAD_PAD), row),
                 pl.BlockSpec((nsub, h * V_HEAD, TQ), lambda i: (i, 0, 0))]
    qkv_shapes = [jax.ShapeDtypeStruct((t // TQ, h * HEAD_PAD, TQ), BF16),
                  jax.ShapeDtypeStruct((t, h * HEAD_PAD), BF16),
                  jax.ShapeDtypeStruct((t // TQ, h * V_HEAD, TQ), BF16)]
    cs_specs = [pl.BlockSpec((tm, 3 * LANES), row), pl.BlockSpec((QK_ROPE, tm), lambda i: (0, i))]
    if len(x_parts) == 1:
        q, k, v = pl.pallas_call(
            _mla_qkv_kernel, grid=(nt,),
            in_specs=[pl.BlockSpec((tm, d), row)] + cs_specs + w_specs,
            out_specs=qkv_specs, out_shape=qkv_shapes,
            compiler_params=_cparams(("parallel",)), name="mla_qkv",
        )(x_parts[0], cs, cst, *weights)
        return x_parts[0], q, k, v
    x1, gates_tok, yg = x_parts
    x, q, k, v = pl.pallas_call(
        _mla_qkv_combine_kernel, grid=(nt,),
        in_specs=[pl.BlockSpec((tm, d), row), pl.BlockSpec((tm, LANES), row),
                  pl.BlockSpec((SC_PLANES, tm, d // (2 * SC_PLANES)), lambda i: (0, i, 0)),
                  pl.BlockSpec((SC_PLANES, tm, d // (2 * SC_PLANES)), lambda i: (0, i + nt, 0)),
                  ] + cs_specs + w_specs,
        out_specs=[pl.BlockSpec((tm, d), row)] + qkv_specs,
        out_shape=[jax.ShapeDtypeStruct((t, d), F32)] + qkv_shapes,
        compiler_params=_cparams(("parallel",)), name="mla_qkv_combine",
    )(x1, gates_tok, yg, yg, cs, cst, *weights)
    return x, q, k, v


def _attn_kernel(q_ref, k_ref, vt_ref, x_ref, wo_ref, g_ref, x1_ref, xn_ref,
                 s_ref, m_ref, acc_ref, o_ref):
    i = pl.program_id(1)
    ones = jnp.ones((ATTN_SUM_ROWS, TQ), BF16)

    def scores(h, j, n):
        off = pl.multiple_of(j * TQ, TQ)
        hs = slice(h * HEAD_PAD, (h + 1) * HEAD_PAD)
        s_ref[h, 0:n * TQ, :] = _dot(k_ref[pl.ds(off, n * TQ), hs], q_ref[0, hs, :])

    def weighted(h, j, p, n):
        out = None
        for t in range(n):
            lhs = jnp.concatenate([vt_ref[j + t, h * V_HEAD:(h + 1) * V_HEAD, :], ones], axis=0)
            part = _dot(lhs, p[t * TQ:(t + 1) * TQ, :].astype(BF16))
            out = part if out is None else out + part
        return out

    for h in range(MLA_HEADS):
        scores(h, i, 1)
    kc = lax.broadcasted_iota(jnp.int32, (TQ, TQ), 0) // CHUNK
    qc = lax.broadcasted_iota(jnp.int32, (TQ, TQ), 1) // CHUNK
    for h in range(MLA_HEADS):
        st = jnp.where(kc <= qc, s_ref[h, 0:TQ, :], -jnp.inf)
        m = jnp.max(st, axis=0, keepdims=True)
        acc_ref[h] = weighted(h, i, jnp.exp2(st - m), 1)
        m_ref[h] = m

    def full_tiles(j, n):
        for h in range(MLA_HEADS):
            scores(h, j, n)
        for h in range(MLA_HEADS):
            st = s_ref[h, 0:n * TQ, :]
            m_old = m_ref[h]
            m_new = jnp.maximum(m_old, jnp.max(st, axis=0, keepdims=True))
            acc_ref[h] = (jnp.exp2(m_old - m_new) * acc_ref[h]
                          + weighted(h, j, jnp.exp2(st - m_new), n))
            m_ref[h] = m_new

    def quad(jq, carry):
        full_tiles(4 * jq, 4)
        return carry

    lax.fori_loop(0, i // 4, quad, 0)
    rest = i % 4

    @pl.when(rest >= 2)
    def _():
        full_tiles(i - rest, 2)

    @pl.when(rest % 2 == 1)
    def _():
        full_tiles(i - 1, 1)

    for h in range(MLA_HEADS):
        acc = acc_ref[h]
        inv_l = 1.0 / acc[V_HEAD:V_HEAD + 1, :]
        o_ref[:, h * V_HEAD:(h + 1) * V_HEAD] = (acc[0:V_HEAD, :] * inv_l).T.astype(BF16)

    x1 = x_ref[...] + _dot(o_ref[...], wo_ref[...])
    x1_ref[...] = x1
    xn_ref[...] = _rms(x1, g_ref[...]).astype(BF16)


def _attention(q, k, vt, x, w_o, g_ffn, batch, seq):
    h = MLA_HEADS
    nq = seq // TQ
    t, d = x.shape
    tile = lambda b, i: (b * nq + i, 0)
    const = lambda b, i: (0, 0)
    return pl.pallas_call(
        _attn_kernel, grid=(batch, nq),
        in_specs=[pl.BlockSpec((1, h * HEAD_PAD, TQ), lambda b, i: (b * nq + i, 0, 0)),
                  pl.BlockSpec((seq, h * HEAD_PAD), lambda b, i: (b, 0)),
                  pl.BlockSpec((nq, h * V_HEAD, TQ), lambda b, i: (b, 0, 0)),
                  pl.BlockSpec((TQ, d), tile), pl.BlockSpec(w_o.shape, const),
                  pl.BlockSpec((1, d), const)],
        out_specs=[pl.BlockSpec((TQ, d), tile), pl.BlockSpec((TQ, d), tile)],
        out_shape=[jax.ShapeDtypeStruct((t, d), F32), jax.ShapeDtypeStruct((t, d), BF16)],
        scratch_shapes=[pltpu.VMEM((h, 4 * TQ, TQ), F32),
                        pltpu.VMEM((h, 1, TQ), F32),
                        pltpu.VMEM((h, V_HEAD + ATTN_SUM_ROWS, TQ), F32),
                        pltpu.VMEM((TQ, h * V_HEAD), BF16)],
        compiler_params=_cparams(("parallel", "arbitrary")), name="attention",
    )(q, k, vt, x, w_o, g_ffn)


def _swiglu_chunk(x, w1, w3, w2):
    a = _dot(x, w1)
    b = _dot(x, w3)
    hid = (a * jax.nn.sigmoid(a)) * b
    return _dot(hid.astype(BF16), w2)


def _weight_chunk_copies(w1_hbm, w3_hbm, w2_hbm, lead, c, tf, slot, w1b, w3b, w2b, sem):
    cols = pl.ds(pl.multiple_of(c * tf, tf), tf)
    return (pltpu.make_async_copy(w1_hbm.at[lead + (slice(None), cols)], w1b.at[slot], sem.at[0, slot]),
            pltpu.make_async_copy(w3_hbm.at[lead + (slice(None), cols)], w3b.at[slot], sem.at[1, slot]),
            pltpu.make_async_copy(w2_hbm.at[lead + (cols, slice(None))], w2b.at[slot], sem.at[2, slot]))


def _ffn_kernel(xn_ref, x1_ref, w1_hbm, w3_hbm, w2_hbm, o_ref, w1b, w3b, w2b, sem, *, layer, tf):
    i = pl.program_id(0)
    n_tiles = pl.num_programs(0)
    ff = w1_hbm.shape[2]
    chunks = [(lo, min(tf, ff - lo)) for lo in range(0, ff, tf)]
    assert len(chunks) % 2 == 0

    def copies(c):
        lo, width = chunks[c]
        slot = c % 2
        cols = pl.ds(lo, width)
        return (pltpu.make_async_copy(w1_hbm.at[layer, :, cols], w1b.at[slot, :, 0:width],
                                      sem.at[0, slot]),
                pltpu.make_async_copy(w3_hbm.at[layer, :, cols], w3b.at[slot, :, 0:width],
                                      sem.at[1, slot]),
                pltpu.make_async_copy(w2_hbm.at[layer, cols, :], w2b.at[slot, 0:width, :],
                                      sem.at[2, slot]))

    @pl.when(i == 0)
    def _():
        for cp in copies(0):
            cp.start()

    o_ref[...] = x1_ref[...]
    for c, (lo, width) in enumerate(chunks):
        if c + 1 < len(chunks):
            for cp in copies(c + 1):
                cp.start()
        else:
            @pl.when(i + 1 < n_tiles)
            def _():
                for cp in copies(0):
                    cp.start()
        for cp in copies(c):
            cp.wait()
        slot = c % 2
        o_ref[...] += _swiglu_chunk(xn_ref[...], w1b[slot, :, 0:width].astype(BF16),
                                    w3b[slot, :, 0:width].astype(BF16),
                                    w2b[slot, 0:width, :].astype(BF16))


def _dense_ffn(xn, x1, w1, w3, w2, layer):
    t, d = x1.shape
    tm, tf = TM_FFN, TF_FFN
    row = lambda i: (i, 0)
    hbm = pl.BlockSpec(memory_space=pl.ANY)
    return pl.pallas_call(
        functools.partial(_ffn_kernel, layer=layer, tf=tf), grid=(t // tm,),
        in_specs=[pl.BlockSpec((tm, d), row), pl.BlockSpec((tm, d), row), hbm, hbm, hbm],
        out_specs=pl.BlockSpec((tm, d), row),
        out_shape=jax.ShapeDtypeStruct((t, d), F32),
        scratch_shapes=[pltpu.VMEM((2, d, tf), F32), pltpu.VMEM((2, d, tf), F32),
                        pltpu.VMEM((2, tf, d), F32), pltpu.SemaphoreType.DMA((3, 2))],
        compiler_params=_cparams(("arbitrary",)), name="dense_ffn",
    )(xn, x1, w1, w3, w2)


def _lru_kernel(x_ref, gmix_ref, win_ref, bin_ref, cw_ref, cb_ref, wga_ref, bga_ref,
                wgi_ref, bgi_ref, lam_ref, wout_ref, bout_ref, gffn_ref,
                x1_ref, xn_ref,
                xpad_ref, a_ref, u_ref, gate_ref, yg_ref, h_ref):
    nb, tt, d = x_ref.shape
    w = lam_ref.shape[1]
    m = nb * tt

    @pl.when(pl.program_id(0) == 0)
    def _():
        h_ref[...] = jnp.zeros_like(h_ref)
        xpad_ref[:, 0:SUBLANES, :] = jnp.zeros((nb, SUBLANES, w), F32)

    x = x_ref[...].reshape(m, d)
    xn = _rms(x, gmix_ref[...]).astype(BF16)
    hcat = _dot(xn, win_ref[...]) + bin_ref[...]
    gate_ref[...] = jax.nn.gelu(hcat[:, :w], approximate=True)
    xpad_ref[:, SUBLANES:, :] = hcat[:, w:].reshape(nb, tt, w)

    xc = cb_ref[...].reshape(1, 1, w) + jnp.zeros((nb, tt, w), F32)
    for j in range(CONV_WIDTH):
        lo = SUBLANES - (CONV_WIDTH - 1) + j
        xc = xc + xpad_ref[:, lo:lo + tt, :] * cw_ref[j:j + 1, :].reshape(1, 1, w)
    xpad_ref[:, 0:SUBLANES, :] = xpad_ref[:, tt:tt + SUBLANES, :]
    xc = xc.reshape(m, w)

    lam = lam_ref[...]
    log_sig = jnp.minimum(lam, 0.0) - jnp.log1p(jnp.exp(-jnp.abs(lam)))
    for hh in range(LRU_HEADS):
        sl = slice(hh * LANES, (hh + 1) * LANES)
        xh = xc[:, sl]
        xhb = xh.astype(BF16)
        r = jax.nn.sigmoid(_dot(xhb, wga_ref[hh]) + bga_ref[hh:hh + 1, :])
        ig = jax.nn.sigmoid(_dot(xhb, wgi_ref[hh]) + bgi_ref[hh:hh + 1, :])
        log_a = LRU_C * r * log_sig[:, sl]
        a = jnp.exp(log_a)
        z = 1.0 - a * a
        u = (z * lax.rsqrt(jnp.maximum(z, 1e-30))) * (ig * xh)
        for b in range(nb):
            a_ref[hh, b * LRU_PITCH:b * LRU_PITCH + tt, :] = a[b * tt:(b + 1) * tt, :]
            u_ref[hh, b * LRU_PITCH:b * LRU_PITCH + tt, :] = u[b * tt:(b + 1) * tt, :]

    def scan_step(t, hs):
        new = []
        for hh in range(LRU_HEADS):
            rows = pl.ds(t, nb, stride=LRU_PITCH)
            hv = a_ref[hh, rows, :] * hs[hh] + u_ref[hh, rows, :]
            u_ref[hh, rows, :] = hv
            new.append(hv)
        return tuple(new)

    hs = lax.fori_loop(0, tt, scan_step, tuple(h_ref[hh] for hh in range(LRU_HEADS)), unroll=4)
    for hh in range(LRU_HEADS):
        h_ref[hh] = hs[hh]

    for hh in range(LRU_HEADS):
        sl = slice(hh * LANES, (hh + 1) * LANES)
        for b in range(nb):
            y = u_ref[hh, b * LRU_PITCH:b * LRU_PITCH + tt, :]
            yg_ref[b * tt:(b + 1) * tt, sl] = (y * gate_ref[b * tt:(b + 1) * tt, sl]).astype(BF16)

    x1 = x + _dot(yg_ref[...], wout_ref[...]) + bout_ref[...]
    x1_ref[...] = x1.reshape(nb, tt, d)
    xn_ref[...] = _pack_pairs(_rms(x1, gffn_ref[...])).reshape(nb, tt, d // 2)


def _lru_mixer(x3, gmix, w_in, b_in, conv_w, conv_b, wga, bga, wgi, bgi, lam, w_out, b_out, g_ffn):
    nb, seq, d = x3.shape
    w = lam.shape[1]
    tt = T_LRU
    m = nb * tt
    consts = [gmix, w_in, b_in, conv_w, conv_b, wga, bga, wgi, bgi, lam, w_out, b_out, g_ffn]
    const_specs = [pl.BlockSpec(c.shape, (lambda i, n=c.ndim: (0,) * n)) for c in consts]
    blk = pl.BlockSpec((nb, tt, d), lambda i: (0, i, 0))
    return pl.pallas_call(
        _lru_kernel, grid=(seq // tt,),
        in_specs=[blk] + const_specs,
        out_specs=[blk, pl.BlockSpec((nb, tt, d // 2), lambda i: (0, i, 0))],
        out_shape=[jax.ShapeDtypeStruct((nb, seq, d), F32),
                   jax.ShapeDtypeStruct((nb, seq, d // 2), jnp.int32)],
        scratch_shapes=[pltpu.VMEM((nb, tt + SUBLANES, w), F32),
                        pltpu.VMEM((LRU_HEADS, nb * LRU_PITCH, LANES), F32),
                        pltpu.VMEM((LRU_HEADS, nb * LRU_PITCH, LANES), F32),
                        pltpu.VMEM((m, w), F32),
                        pltpu.VMEM((m, w), BF16),
                        pltpu.VMEM((LRU_HEADS, nb, LANES), F32)],
        compiler_params=_cparams(("arbitrary",)), name="rglru_mixer",
    )(x3, *consts)


def _router_kernel(x1_ref, g_ref, wh_ref, wl_ref, br_ref, slot_ref, gate_ref, te_ref, tv_ref,
                   lg_ref, tri_ref, cnt_ref, run_ref, start_ref, *, tile):
    phase = pl.program_id(0)
    j = pl.program_id(1)
    ne = N_EXPERTS
    tr = x1_ref.shape[0]

    @pl.when((phase == 0) & (j == 0))
    def _():
        cnt_ref[...] = jnp.zeros_like(cnt_ref)
        tri_ref[...] = jnp.where(lax.broadcasted_iota(jnp.int32, (tr, tr), 0)
                                 < lax.broadcasted_iota(jnp.int32, (tr, tr), 1),
                                 1.0, 0.0).astype(BF16)

    @pl.when(phase == 0)
    def _():
        x = _rms(x1_ref[...], g_ref[...])
        xh = x.astype(BF16)
        xl = (x - xh.astype(F32)).astype(BF16)
        wh = wh_ref[...]
        nat = _dot(xh, wh) + _dot(xl, wh) + _dot(xh, wl_ref[...])
        lg_ref[j] = nat.T[0:ne, :] + br_ref[...]

    logits = lg_ref[j]
    eidx = lax.broadcasted_iota(jnp.int32, (ne, tr), 0)
    m1 = jnp.max(logits, axis=0, keepdims=True)
    i1 = jnp.min(jnp.where(logits == m1, eidx, ne), axis=0, keepdims=True)
    oh1 = eidx == i1
    rest = jnp.where(oh1, -jnp.inf, logits)
    m2 = jnp.max(rest, axis=0, keepdims=True)
    i2 = jnp.min(jnp.where(rest == m2, eidx, ne), axis=0, keepdims=True)
    oh2 = eidx == i2
    oh = jnp.where(oh1 | oh2, 1.0, 0.0)
    tile_cnt = jnp.sum(oh, axis=1, keepdims=True)

    @pl.when(phase == 0)
    def _():
        cnt_ref[...] += tile_cnt

    @pl.when((phase == 1) & (j == 0))
    def _():
        cnt = cnt_ref[...]
        padded = jnp.ceil(cnt * (1.0 / tile)) * tile
        sub = lax.broadcasted_iota(jnp.int32, (ne, 1), 0)
        start = jnp.zeros((ne, 1), F32)
        for e in range(ne - 1):
            start = start + jnp.where(sub > e, padded[e:e + 1, :], 0.0)
        start_ref[...] = start
        run_ref[...] = jnp.zeros_like(run_ref)
        tile_start = lax.broadcasted_iota(jnp.int32, (ne, LANES), 1).astype(F32) * tile
        owner = jnp.sum(jnp.where(tile_start >= start + padded, 1, 0), axis=0, keepdims=True)
        owner = jnp.minimum(owner, ne - 1)
        esub = lax.broadcasted_iota(jnp.int32, (ne, LANES), 0)
        real = jnp.clip(cnt - (tile_start - start), 0.0, float(tile))
        te_ref[...] = owner
        tv_ref[...] = jnp.sum(jnp.where(esub == owner, real, 0.0), axis=0,
                              keepdims=True).astype(jnp.int32)

    @pl.when(phase == 1)
    def _():
        before = _dot(oh.astype(BF16), tri_ref[...])
        slot_e = start_ref[...] + run_ref[...] + before
        s1 = jnp.sum(jnp.where(oh1, slot_e, 0.0), axis=0, keepdims=True)
        s2 = jnp.sum(jnp.where(oh2, slot_e, 0.0), axis=0, keepdims=True)
        slot_ref[0:1, :] = s1.astype(jnp.int32)
        slot_ref[1:2, :] = s2.astype(jnp.int32)
        e21 = jnp.exp(m2 - m1)
        g1 = 1.0 / (1.0 + e21)
        g2 = e21 * g1
        rowi = lax.broadcasted_iota(jnp.int32, (LANES, tr), 0)
        gmat = jnp.where(rowi == 0, g1, jnp.where(rowi == 1, g2, 0.0))
        gate_ref[...] = gmat.T
        run_ref[...] += tile_cnt


def _router(x1, g_ffn, w_router, b_router, tile):
    t, d = x1.shape
    tr = TR
    nt = t // tr
    wp = jnp.pad(w_router, ((0, 0), (0, LANES - N_EXPERTS)))
    wh = wp.astype(BF16)
    wl = (wp - wh.astype(F32)).astype(BF16)
    const = lambda p, j: (0, 0)
    return pl.pallas_call(
        functools.partial(_router_kernel, tile=tile), grid=(2, nt),
        in_specs=[pl.BlockSpec((tr, d), lambda p, j: (j * (1 - p) + (nt - 1) * p, 0)),
                  pl.BlockSpec((1, d), const),
                  pl.BlockSpec(wh.shape, const), pl.BlockSpec(wl.shape, const),
                  pl.BlockSpec((N_EXPERTS, 1), const)],
        out_specs=[pl.BlockSpec((2, tr), lambda p, j: (0, j * p)),
                   pl.BlockSpec((tr, LANES), lambda p, j: (j * p, 0)),
                   pl.BlockSpec((1, LANES), const), pl.BlockSpec((1, LANES), const)],
        out_shape=[jax.ShapeDtypeStruct((2, t), jnp.int32),
                   jax.ShapeDtypeStruct((t, LANES), F32),
                   jax.ShapeDtypeStruct((1, LANES), jnp.int32),
                   jax.ShapeDtypeStruct((1, LANES), jnp.int32)],
        scratch_shapes=[pltpu.VMEM((nt, N_EXPERTS, tr), F32), pltpu.VMEM((tr, tr), BF16),
                        pltpu.VMEM((N_EXPERTS, 1), F32), pltpu.VMEM((N_EXPERTS, 1), F32),
                        pltpu.VMEM((N_EXPERTS, 1), F32)],
        compiler_params=_cparams(("arbitrary", "arbitrary")), name="moe_router",
    )(x1, g_ffn, wh, wl, b_router.reshape(N_EXPERTS, 1))


def _sc_mesh():
    return plsc.VectorSubcoreMesh(core_axis_name="c", subcore_axis_name="s")


def _sc_dispatch(xn, slots, n_slots):
    t, d = xn.shape
    win = SC_WINDOW
    nwin = t // win
    dp = d // SC_PLANES
    idx = slots.reshape(1, 2 * t)

    @functools.partial(pl.kernel,
                       out_type=jax.ShapeDtypeStruct((SC_PLANES, n_slots, dp), xn.dtype),
                       mesh=_sc_mesh(), scratch_types=[], name="moe_dispatch")
    def run(x_hbm, i_hbm, o_hbm):
        for c in range(SC_PLANES):
            def body(x_vmem, i0_vmem, i1_vmem, c=c):
                pltpu.sync_copy(x_vmem, o_hbm.at[c].at[i0_vmem.at[0]])
                pltpu.sync_copy(x_vmem, o_hbm.at[c].at[i1_vmem.at[0]])

            pltpu.emit_pipeline(
                body, grid=(nwin,),
                in_specs=[pl.BlockSpec((win, dp), lambda i, c=c: (i, c)),
                          pl.BlockSpec((1, win), lambda i: (0, i)),
                          pl.BlockSpec((1, win), lambda i: (0, i + nwin))],
                out_specs=[],
                core_axis_name=("c", "s"),
                dimension_semantics=(pltpu.PARALLEL,),
            )(x_hbm, i_hbm, i_hbm)

    return run(xn, idx)


def _sc_combine(y, slots):
    n2 = slots.size
    dp = y.shape[2]
    win = SC_WINDOW
    idx = slots.reshape(1, n2)

    @functools.partial(pl.kernel, out_type=jax.ShapeDtypeStruct((SC_PLANES, n2, dp), y.dtype),
                       mesh=_sc_mesh(), scratch_types=[], name="moe_combine")
    def run(y_hbm, i_hbm, o_hbm):
        for c in range(SC_PLANES):
            def body(i_vmem, o_vmem, c=c):
                pltpu.sync_copy(y_hbm.at[c].at[i_vmem.at[0]], o_vmem)

            pltpu.emit_pipeline(
                body, grid=(n2 // win,),
                in_specs=[pl.BlockSpec((1, win), lambda i: (0, i))],
                out_specs=[pl.BlockSpec((win, dp), lambda i: (i, 0))],
                core_axis_name=("c", "s"),
                dimension_semantics=(pltpu.PARALLEL,),
            )(i_hbm, o_hbm.at[c])

    return run(y, idx)


def _moe_ffn_kernel(te_ref, tv_ref, xs_ref, w1_hbm, w3_hbm, w2_hbm, y_ref,
                    xb_ref, acc_ref, w1b, w3b, w2b, wb1_ref, wb3_ref, wb2_ref, sem, *, layer, tf):
    i = pl.program_id(0)
    n_tiles = pl.num_programs(0)
    valid = tv_ref[i]
    tm, dp = xs_ref.shape[1:]
    ts = TSUB_MOE
    nc = w1_hbm.shape[3] // tf

    def copies(tile, c, slot):
        return _weight_chunk_copies(w1_hbm, w3_hbm, w2_hbm, (layer, te_ref[tile]), c, tf, slot,
                                    w1b, w3b, w2b, sem)

    @pl.when((i == 0) & (valid > 0))
    def _():
        for cp in copies(0, 0, 0):
            cp.start()

    rows = lax.broadcasted_iota(jnp.int32, (tm, 1), 0)
    for c, piece in enumerate(_unpack_planes(xs_ref)):
        xb_ref[:, c * dp:(c + 1) * dp] = jnp.where(rows < valid, piece, 0.0).astype(BF16)

    def rows_block(lo, n, w1, w3, w2, assign=False):
        y = _swiglu_chunk(xb_ref[lo:lo + n, :], w1, w3, w2)
        if assign:
            acc_ref[lo:lo + n, :] = y
        else:
            acc_ref[lo:lo + n, :] += y

    @pl.when(valid > 0)
    def _():
        nxt_tile = jnp.minimum(i + 1, n_tiles - 1)
        next_tile_live = (i + 1 < n_tiles) & (tv_ref[nxt_tile] > 0)

        def chunk(c, carry, first=False):
            slot = lax.rem(i * nc + c, 2)
            last = c + 1 == nc

            @pl.when(jnp.logical_not(last) | next_tile_live)
            def _():
                for cp in copies(jnp.where(last, nxt_tile, i), jnp.where(last, 0, c + 1), 1 - slot):
                    cp.start()

            for cp in copies(i, c, slot):
                cp.wait()

            @pl.when(valid > tm - ts)
            def _():
                rows_block(0, tm, w1b[slot].astype(BF16), w3b[slot].astype(BF16),
                           w2b[slot].astype(BF16), assign=first)

            @pl.when(valid <= tm - ts)
            def _():
                if first:
                    acc_ref[...] = jnp.zeros_like(acc_ref)
                wb1_ref[...] = w1b[slot].astype(BF16)
                wb3_ref[...] = w3b[slot].astype(BF16)
                wb2_ref[...] = w2b[slot].astype(BF16)
                weights = lambda: (wb1_ref[...], wb3_ref[...], wb2_ref[...])

                @pl.when(valid <= ts)
                def _():
                    rows_block(0, ts, *weights())

                @pl.when(valid > ts)
                def _():
                    rows_block(0, 2 * ts, *weights())

                @pl.when(valid > 2 * ts)
                def _():
                    rows_block(2 * ts, ts, *weights())

            return carry

        chunk(0, 0, first=True)
        lax.fori_loop(1, nc, chunk, 0)

    packed = _pack_pairs(acc_ref[...])
    for c in range(SC_PLANES):
        y_ref[c] = packed[:, c * dp:(c + 1) * dp]


def _moe_ffn(xs, tile_expert, tile_valid, w1, w3, w2, layer):
    planes, ns, dp = xs.shape
    d = 2 * planes * dp
    tm, tf = TM_MOE, TF_MOE
    assert tm == 4 * TSUB_MOE
    hbm = pl.BlockSpec(memory_space=pl.ANY)
    grid_spec = pltpu.PrefetchScalarGridSpec(
        num_scalar_prefetch=2, grid=(ns // tm,),
        in_specs=[pl.BlockSpec((planes, tm, dp), lambda i, te, tv: (0, i, 0)), hbm, hbm, hbm],
        out_specs=pl.BlockSpec((planes, tm, dp), lambda i, te, tv: (0, i, 0)),
        scratch_shapes=[pltpu.VMEM((tm, d), BF16), pltpu.VMEM((tm, d), F32),
                        pltpu.VMEM((2, d, tf), F32), pltpu.VMEM((2, d, tf), F32),
                        pltpu.VMEM((2, tf, d), F32),
                        pltpu.VMEM((d, tf), BF16), pltpu.VMEM((d, tf), BF16),
                        pltpu.VMEM((tf, d), BF16), pltpu.SemaphoreType.DMA((3, 2))])
    return pl.pallas_call(
        functools.partial(_moe_ffn_kernel, layer=layer, tf=tf), grid_spec=grid_spec,
        out_shape=jax.ShapeDtypeStruct((planes, ns, dp), jnp.int32),
        compiler_params=_cparams(("arbitrary",)), name="moe_ffn",
    )(tile_expert, tile_valid, xs, w1, w3, w2)


def _combine_kernel(x1_ref, gt_ref, y0_ref, y1_ref, o_ref):
    o_ref[...] = _gated_sum(x1_ref, gt_ref, y0_ref, y1_ref)


def _combine(x1, gates_tok, yg):
    t, d = x1.shape
    tm = 512
    nt = t // tm
    row = lambda i: (i, 0)
    return pl.pallas_call(
        _combine_kernel, grid=(nt,),
        in_specs=[pl.BlockSpec((tm, d), row), pl.BlockSpec((tm, LANES), row),
                  pl.BlockSpec((SC_PLANES, tm, d // (2 * SC_PLANES)), lambda i: (0, i, 0)),
                  pl.BlockSpec((SC_PLANES, tm, d // (2 * SC_PLANES)), lambda i: (0, i + nt, 0))],
        out_specs=pl.BlockSpec((tm, d), row),
        out_shape=jax.ShapeDtypeStruct((t, d), F32),
        compiler_params=_cparams(("parallel",)), name="moe_combine_residual",
    )(x1, gates_tok, yg, yg)


def _mla_weights(w_in, w_qb, w_kvb, g_qn, g_kn):
    d = w_in.shape[0]
    h = MLA_HEADS
    lat = Q_LORA + KV_LORA
    win = jnp.zeros((d, lat + HEAD_PAD), F32)
    win = win.at[:, :lat].set(w_in[:, :lat])
    win = win.at[:, lat + LANES:lat + LANES + QK_ROPE].set(w_in[:, lat:])
    wq = w_qb.reshape(Q_LORA, h, QK_HEAD)
    wq = jnp.pad(wq, ((0, 0), (0, 0), (0, HEAD_PAD - QK_HEAD))).reshape(Q_LORA, h * HEAD_PAD)
    wkv = w_kvb.reshape(KV_LORA, h, QK_NOPE + V_HEAD)
    wk = wkv[:, :, :QK_NOPE].reshape(KV_LORA, h * QK_NOPE)
    wvt = wkv[:, :, QK_NOPE:].reshape(KV_LORA, h * V_HEAD).T
    scale = math.log2(math.e) / math.sqrt(QK_HEAD)
    gq = jnp.broadcast_to((g_qn * scale).reshape(QK_HEAD, 1), (QK_HEAD, TQ))
    gk = jnp.pad(g_kn, (0, HEAD_PAD - QK_HEAD)).reshape(1, HEAD_PAD)
    return (win.astype(BF16), wq.T.astype(BF16), wk.astype(BF16), wvt.astype(BF16), gq, gk)


def kernel(x, positions, norm_mix, norm_ffn, mla_w_in, mla_g_qa, mla_w_qb, mla_g_kva, mla_w_kvb, mla_g_qn, mla_g_kn, mla_w_o, lru_w_in, lru_b_in, lru_conv_w, lru_conv_b, lru_w_gate_a, lru_b_gate_a, lru_w_gate_i, lru_b_gate_i, lru_lambda, lru_w_out, lru_b_out, ffn_w1, ffn_w3, ffn_w2, moe_w_router, moe_b_router, moe_w1, moe_w3, moe_w2):
    batch, seq, d = x.shape
    t = batch * seq
    depth = norm_mix.shape[0]
    n_slots = 2 * t + N_EXPERTS * TM_MOE
    n_tiles = n_slots // TM_MOE
    row = lambda v: v.reshape(1, -1)

    cs, cst = _rope_tables(positions)
    parts = (x.reshape(t, d),)
    for i in range(depth):
        j = i // 2
        if i % 2 == 0:
            w_in, w_qbt, w_kb, w_vt, g_qn, g_kn = _mla_weights(
                mla_w_in[j], mla_w_qb[j], mla_w_kvb[j], mla_g_qn[j], mla_g_kn[j])
            xr, q, k, vt = _mla_qkv(parts, cs, cst, row(norm_mix[i]), w_in, row(mla_g_qa[j]),
                                    w_qbt, row(mla_g_kva[j]), w_kb, w_vt, g_qn, g_kn)
            x1, xn = _attention(q, k, vt, xr, mla_w_o[j].astype(BF16), row(norm_ffn[i]),
                                batch, seq)
            parts = (_dense_ffn(xn, x1, ffn_w1, ffn_w3, ffn_w2, j),)
        else:
            (xr,) = parts
            x1, xn = _lru_mixer(
                xr.reshape(batch, seq, d), row(norm_mix[i]), lru_w_in[j].astype(BF16),
                row(lru_b_in[j]), lru_conv_w[j], row(lru_conv_b[j]),
                lru_w_gate_a[j].astype(BF16), lru_b_gate_a[j], lru_w_gate_i[j].astype(BF16),
                lru_b_gate_i[j], row(lru_lambda[j]), lru_w_out[j].astype(BF16),
                row(lru_b_out[j]), row(norm_ffn[i]))
            x1 = x1.reshape(t, d)
            xn = xn.reshape(t, d // 2)
            slots, gates_tok, te, tv = _router(x1, row(norm_ffn[i]), moe_w_router[j],
                                               moe_b_router[j], TM_MOE)
            xs = _sc_dispatch(xn, slots, n_slots)
            y = _moe_ffn(xs, te[0, :n_tiles], tv[0, :n_tiles], moe_w1, moe_w3, moe_w2, j)
            yg = _sc_combine(y, slots)
            parts = (x1, gates_tok, yg)
    if len(parts) == 3:
        out = _combine(*parts)
    else:
        out = parts[0]
    return out.reshape(batch, seq, d)
```
